```python
import math
import jax, jax.numpy as jnp
from jax import lax
import numpy as np

D_MODEL = 1024
BATCH = 8
SEQ = 2048
DEPTH = 4
DEC_BATCH = 128
DEC_SEQ = 8
PAST_LEN = 16384
PAGE_SIZE = 128

N_MIXERS = 3
N_S5 = (DEPTH + 2) // 3
N_GLA = (DEPTH + 1) // 3
N_GDN = DEPTH // 3
NORM_EPS = 1e-6
D_FF = 4 * D_MODEL

S5_GROUP = 16
S5_GROUPS = D_MODEL // S5_GROUP
S5_STATE = 64
S5_DT_MIN = 0.001
S5_DT_MAX = 0.1

GLA_HEADS = 4
GLA_DK = (D_MODEL // 2) // GLA_HEADS
GLA_DV = D_MODEL // GLA_HEADS
GLA_KEY = GLA_HEADS * GLA_DK
GLA_VAL = GLA_HEADS * GLA_DV
GLA_GATE_RANK = 16
GLA_TAU = 16.0
GLA_CHUNK = 64
GLA_IN = 2 * GLA_KEY + 2 * GLA_VAL + GLA_GATE_RANK

GDN_DK = 128
GDN_DV = 128
GDN_QK_HEADS = D_MODEL // GDN_DK
GDN_V_HEADS = 2 * GDN_QK_HEADS
GDN_KEY = GDN_QK_HEADS * GDN_DK
GDN_VAL = GDN_V_HEADS * GDN_DV
GDN_CONV = 4
GDN_CONV_DIM = 2 * GDN_KEY + GDN_VAL
GDN_CHUNK = 64
GDN_IN = GDN_CONV_DIM + GDN_VAL + 2 * GDN_V_HEADS

kernel_name = 'hybrid_s5_gla_gdn_decoder_step'


def rmsnorm(x, gain):
    xf = x.astype(jnp.float32)
    y = xf * lax.rsqrt(jnp.mean(xf * xf, axis=-1, keepdims=True) + NORM_EPS)
    return (y * gain.astype(jnp.float32)).astype(x.dtype)


def l2norm(x):
    return x * lax.rsqrt(jnp.sum(x * x, axis=-1, keepdims=True) + NORM_EPS)


def _chunk_len(L, c):
    return c if L % c == 0 else L


def _to_chunks(t, c):
    B, L = t.shape[:2]
    return jnp.moveaxis(t.reshape((B, L // c, c) + t.shape[2:]), 1, 0)


def _from_chunks(t):
    N, B, c = t.shape[:3]
    return jnp.moveaxis(t, 0, 1).reshape((B, N * c) + t.shape[3:])


def s5_mixer(h, s0_re, s0_im, a_re, a_im, log_dt, b_re, b_im, c_re, c_im, d_skip, w_glu):
    Bsz, L, _ = h.shape
    f32 = jnp.float32
    dt = jnp.exp(log_dt.astype(f32))[:, None]
    A = lax.complex(a_re.astype(f32), a_im.astype(f32))
    a_bar = jnp.exp(A * dt)
    Bm = lax.complex(b_re.astype(f32), b_im.astype(f32))
    b_bar = ((a_bar - 1.0) / A)[..., None] * Bm
    u = h.astype(f32).reshape(Bsz, L, S5_GROUPS, S5_GROUP)
    bu = jnp.einsum('gpc,blgc->blgp', b_bar, u.astype(jnp.complex64))
    s0 = lax.complex(s0_re.astype(f32), s0_im.astype(f32))
    bu = bu.at[:, 0].add(a_bar * s0)
    a_seq = jnp.broadcast_to(a_bar, (1, L) + a_bar.shape)

    def combine(e1, e2):
        a1, x1 = e1
        a2, x2 = e2
        return a2 * a1, a2 * x1 + x2

    _, states = lax.associative_scan(combine, (a_seq, bu), axis=1)
    Cm = lax.complex(c_re.astype(f32), c_im.astype(f32))
    y = jnp.einsum('gcp,blgp->blgc', Cm, states).real.reshape(Bsz, L, D_MODEL)
    y = y + d_skip.astype(f32) * h.astype(f32)
    g = jax.nn.gelu(y).astype(h.dtype)
    gv = g @ w_glu
    out = gv[..., :D_MODEL] * jax.nn.sigmoid(gv[..., D_MODEL:])
    s_last = states[:, -1]
    return out, s_last.real, s_last.imag


def gla_chunk_step(S, inp):
    q, k, v, g = inp
    C = q.shape[1]
    b = jnp.cumsum(g, axis=1)
    causal = jnp.tril(jnp.ones((C, C), bool))[None, :, :, None, None]
    diff = b[:, :, None] - b[:, None, :]
    decay = jnp.exp(jnp.where(causal, diff, -jnp.inf))
    att = jnp.einsum('bthk,bshk,btshk->bhts', q, k, decay)
    o = (jnp.einsum('bthk,bhkv->bthv', q * jnp.exp(b), S)
         + jnp.einsum('bhts,bshv->bthv', att, v))
    b_last = b[:, -1]
    S = (jnp.exp(b_last)[..., None] * S
         + jnp.einsum('bshk,bshv->bhkv', k * jnp.exp(b_last[:, None] - b), v))
    return S, o


def gla_mixer(h, s0, w_in, w_gate_up, b_gate, norm_o, w_out):
    Bsz, L, _ = h.shape
    f32 = jnp.float32
    p = h @ w_in
    q, k, v, r, gl = jnp.split(p, [GLA_KEY, 2 * GLA_KEY, 2 * GLA_KEY + GLA_VAL,
                                   2 * GLA_KEY + 2 * GLA_VAL], axis=-1)
    g = jax.nn.log_sigmoid((gl @ w_gate_up + b_gate).astype(f32)) / GLA_TAU
    q = q.astype(f32).reshape(Bsz, L, GLA_HEADS, GLA_DK) * (GLA_DK ** -0.5)
    k = k.astype(f32).reshape(Bsz, L, GLA_HEADS, GLA_DK)
    v = v.astype(f32).reshape(Bsz, L, GLA_HEADS, GLA_DV)
    g = g.reshape(Bsz, L, GLA_HEADS, GLA_DK)
    c = _chunk_len(L, GLA_CHUNK)
    xs = (_to_chunks(q, c), _to_chunks(k, c), _to_chunks(v, c), _to_chunks(g, c))
    S, o = lax.scan(gla_chunk_step, s0.astype(f32), xs)
    o = rmsnorm(_from_chunks(o), norm_o)
    o = o * jax.nn.silu(r.astype(f32).reshape(Bsz, L, GLA_HEADS, GLA_DV))
    out = o.reshape(Bsz, L, GLA_VAL).astype(h.dtype) @ w_out
    return out, S


def gdn_chunk_step(S, inp):
    q, k, v, beta, g = inp
    C = q.shape[1]
    G = jnp.cumsum(g, axis=1)
    Gh = jnp.moveaxis(G, 2, 1)
    tri = jnp.tril(jnp.ones((C, C), bool))
    strict = jnp.tril(jnp.ones((C, C), bool), -1)
    decay = jnp.exp(jnp.where(tri, Gh[..., :, None] - Gh[..., None, :], -jnp.inf))
    kk = jnp.einsum('bthk,bshk->bhts', k, k)
    bh = jnp.moveaxis(beta, 2, 1)
    M = jnp.where(strict, kk * decay * bh[..., :, None], 0.0)
    lhs = M + jnp.eye(C, dtype=M.dtype)
    rhs = jnp.concatenate([jnp.moveaxis(v * beta[..., None], 1, 2),
                           jnp.moveaxis(k * (beta * jnp.exp(G))[..., None], 1, 2)], axis=-1)
    uw = lax.linalg.triangular_solve(lhs, rhs, left_side=True, lower=True, unit_diagonal=True)
    u, w = uw[..., :GDN_DV], uw[..., GDN_DV:]
    v_new = u - jnp.einsum('bhtk,bhkv->bhtv', w, S)
    qk = jnp.einsum('bthk,bshk->bhts', q, k) * decay
    o = (jnp.einsum('bthk,bhkv->bhtv', q * jnp.exp(G)[..., None], S)
         + jnp.einsum('bhts,bhsv->bhtv', qk, v_new))
    k_dec = k * jnp.exp(G[:, -1:, :] - G)[..., None]
    S = jnp.exp(Gh[..., -1])[..., None, None] * S + jnp.einsum('bshk,bhsv->bhkv', k_dec, v_new)
    return S, jnp.moveaxis(o, 1, 2)


def gdn_mixer(h, s0, conv0, w_in, conv_w, a_log, dt_bias, norm_o, w_out):
    Bsz, L, _ = h.shape
    f32 = jnp.float32
    p = h @ w_in
    qkv, z, b_raw, a_raw = jnp.split(p, [GDN_CONV_DIM, GDN_CONV_DIM + GDN_VAL,
                                         GDN_CONV_DIM + GDN_VAL + GDN_V_HEADS], axis=-1)
    xp = jnp.concatenate([conv0.astype(f32), qkv.astype(f32)], axis=1)
    cw = conv_w.astype(f32)
    conv = jax.nn.silu(sum(xp[:, j:j + L] * cw[j] for j in range(GDN_CONV)))
    new_conv = xp[:, L:]
    q, k, v = jnp.split(conv, [GDN_KEY, 2 * GDN_KEY], axis=-1)
    rep = GDN_V_HEADS // GDN_QK_HEADS
    q = jnp.repeat(l2norm(q.reshape(Bsz, L, GDN_QK_HEADS, GDN_DK)) * (GDN_DK ** -0.5), rep, axis=2)
    k = jnp.repeat(l2norm(k.reshape(Bsz, L, GDN_QK_HEADS, GDN_DK)), rep, axis=2)
    v = v.reshape(Bsz, L, GDN_V_HEADS, GDN_DV)
    beta = jax.nn.sigmoid(b_raw.astype(f32))
    g = -jnp.exp(a_log.astype(f32)) * jax.nn.softplus(a_raw.astype(f32) + dt_bias.astype(f32))
    c = _chunk_len(L, GDN_CHUNK)
    xs = (_to_chunks(q, c), _to_chunks(k, c), _to_chunks(v, c), _to_chunks(beta, c), _to_chunks(g, c))
    S, o = lax.scan(gdn_chunk_step, s0.astype(f32), xs)
    o = rmsnorm(_from_chunks(o), norm_o) * jax.nn.silu(z.astype(f32).reshape(Bsz, L, GDN_V_HEADS, GDN_DV))
    out = o.reshape(Bsz, L, GDN_VAL).astype(h.dtype) @ w_out
    return out, S, new_conv


def trunk(x, s5_re, s5_im, gla_s, gdn_s, gdn_conv, weights):
    (norm_mix, norm_ffn, norm_final, w_up, w_down,
     s5_a_re, s5_a_im, s5_log_dt, s5_b_re, s5_b_im, s5_c_re, s5_c_im, s5_d, s5_w_glu,
     gla_w_in, gla_w_gate_up, gla_b_gate, gla_norm, gla_w_out,
     gdn_w_in, gdn_conv_w, gdn_a_log, gdn_dt_bias, gdn_norm, gdn_w_out) = weights
    h = x
    o_s5_re, o_s5_im, o_gla, o_gdn, o_conv = [], [], [], [], []
    for i in range(DEPTH):
        kind, j = i % N_MIXERS, i // N_MIXERS
        hn = rmsnorm(h, norm_mix[i])
        if kind == 0:
            mix, sr, si = s5_mixer(hn, s5_re[j], s5_im[j], s5_a_re[j], s5_a_im[j], s5_log_dt[j],
                                   s5_b_re[j], s5_b_im[j], s5_c_re[j], s5_c_im[j], s5_d[j], s5_w_glu[j])
            o_s5_re.append(sr.astype(s5_re.dtype))
            o_s5_im.append(si.astype(s5_im.dtype))
        elif kind == 1:
            mix, S = gla_mixer(hn, gla_s[j], gla_w_in[j], gla_w_gate_up[j], gla_b_gate[j],
                               gla_norm[j], gla_w_out[j])
            o_gla.append(S.astype(gla_s.dtype))
        else:
            mix, S, cv = gdn_mixer(hn, gdn_s[j], gdn_conv[j], gdn_w_in[j], gdn_conv_w[j], gdn_a_log[j],
                                   gdn_dt_bias[j], gdn_norm[j], gdn_w_out[j])
            o_gdn.append(S.astype(gdn_s.dtype))
            o_conv.append(cv.astype(gdn_conv.dtype))
        h = h + mix
        hn = rmsnorm(h, norm_ffn[i])
        h = h + jnp.square(jax.nn.relu(hn @ w_up[i])) @ w_down[i]
    y = rmsnorm(h, norm_final)
    return y, jnp.stack(o_s5_re), jnp.stack(o_s5_im), jnp.stack(o_gla), jnp.stack(o_gdn), jnp.stack(o_conv)


def setup_inputs(seed: int = 0) -> dict:
    key = jax.random.key(seed)
    ks = iter(jax.random.split(key, 40))
    nrm = lambda shape, s: jax.random.normal(next(ks), shape, jnp.float32) * s
    n_idx = jnp.arange(S5_STATE, dtype=jnp.float32)
    gdn_dt = jnp.exp(jax.random.uniform(next(ks), (N_GDN, GDN_V_HEADS), jnp.float32,
                                        math.log(0.001), math.log(0.1)))
    return {
        'x_prompt': nrm((BATCH, SEQ, D_MODEL), 1.0),
        'x_sample': nrm((DEC_BATCH, DEC_SEQ, D_MODEL), 1.0),
        'state_s5_re': nrm((N_S5, DEC_BATCH, S5_GROUPS, S5_STATE), 0.5),
        'state_s5_im': nrm((N_S5, DEC_BATCH, S5_GROUPS, S5_STATE), 0.5),
        'state_gla': nrm((N_GLA, DEC_BATCH, GLA_HEADS, GLA_DK, GLA_DV), 0.3),
        'state_gdn': nrm((N_GDN, DEC_BATCH, GDN_V_HEADS, GDN_DK, GDN_DV), 0.3),
        'state_gdn_conv': nrm((N_GDN, DEC_BATCH, GDN_CONV - 1, GDN_CONV_DIM), 1.0),
        'norm_mix': 1.0 + nrm((DEPTH, D_MODEL), 0.01),
        'norm_ffn': 1.0 + nrm((DEPTH, D_MODEL), 0.01),
        'norm_final': 1.0 + nrm((D_MODEL,), 0.01),
        'w_up': nrm((DEPTH, D_MODEL, D_FF), D_MODEL ** -0.5),
        'w_down': nrm((DEPTH, D_FF, D_MODEL), 0.5 * D_FF ** -0.5),
        's5_a_re': -0.5 + nrm((N_S5, S5_GROUPS, S5_STATE), 0.01),
        's5_a_im': math.pi * n_idx + nrm((N_S5, S5_GROUPS, S5_STATE), 0.01),
        's5_log_dt': jax.random.uniform(next(ks), (N_S5, S5_GROUPS), jnp.float32,
                                        math.log(S5_DT_MIN), math.log(S5_DT_MAX)),
        's5_b_re': nrm((N_S5, S5_GROUPS, S5_STATE, S5_GROUP), (2 * S5_GROUP) ** -0.5),
        's5_b_im': nrm((N_S5, S5_GROUPS, S5_STATE, S5_GROUP), (2 * S5_GROUP) ** -0.5),
        's5_c_re': nrm((N_S5, S5_GROUPS, S5_GROUP, S5_STATE), S5_STATE ** -0.5),
        's5_c_im': nrm((N_S5, S5_GROUPS, S5_GROUP, S5_STATE), S5_STATE ** -0.5),
        's5_d': nrm((N_S5, D_MODEL), 1.0),
        's5_w_glu': nrm((N_S5, D_MODEL, 2 * D_MODEL), D_MODEL ** -0.5),
        'gla_w_in': nrm((N_GLA, D_MODEL, GLA_IN), D_MODEL ** -0.5),
        'gla_w_gate_up': nrm((N_GLA, GLA_GATE_RANK, GLA_KEY), GLA_GATE_RANK ** -0.5),
        'gla_b_gate': nrm((N_GLA, GLA_KEY), 0.1),
        'gla_norm': 1.0 + nrm((N_GLA, GLA_DV), 0.01),
        'gla_w_out': nrm((N_GLA, GLA_VAL, D_MODEL), GLA_VAL ** -0.5),
        'gdn_w_in': nrm((N_GDN, D_MODEL, GDN_IN), D_MODEL ** -0.5),
        'gdn_conv_w': nrm((N_GDN, GDN_CONV, GDN_CONV_DIM), GDN_CONV ** -0.5),
        'gdn_a_log': jnp.log(jax.random.uniform(next(ks), (N_GDN, GDN_V_HEADS), jnp.float32, 1.0, 16.0)),
        'gdn_dt_bias': gdn_dt + jnp.log(-jnp.expm1(-gdn_dt)),
        'gdn_norm': 1.0 + nrm((N_GDN, GDN_DV), 0.01),
        'gdn_w_out': nrm((N_GDN, GDN_VAL, D_MODEL), GDN_VAL ** -0.5),
    }


def reference(x_prompt, x_sample, state_s5_re, state_s5_im, state_gla, state_gdn, state_gdn_conv,
              norm_mix, norm_ffn, norm_final, w_up, w_down,
              s5_a_re, s5_a_im, s5_log_dt, s5_b_re, s5_b_im, s5_c_re, s5_c_im, s5_d, s5_w_glu,
              gla_w_in, gla_w_gate_up, gla_b_gate, gla_norm, gla_w_out,
              gdn_w_in, gdn_conv_w, gdn_a_log, gdn_dt_bias, gdn_norm, gdn_w_out):
    weights = (norm_mix, norm_ffn, norm_final, w_up, w_down,
               s5_a_re, s5_a_im, s5_log_dt, s5_b_re, s5_b_im, s5_c_re, s5_c_im, s5_d, s5_w_glu,
               gla_w_in, gla_w_gate_up, gla_b_gate, gla_norm, gla_w_out,
               gdn_w_in, gdn_conv_w, gdn_a_log, gdn_dt_bias, gdn_norm, gdn_w_out)
    Bp = x_prompt.shape[0]
    dt = x_prompt.dtype
    z_s5 = jnp.zeros((N_S5, Bp, S5_GROUPS, S5_STATE), dt)
    z_gla = jnp.zeros((N_GLA, Bp, GLA_HEADS, GLA_DK, GLA_DV), dt)
    z_gdn = jnp.zeros((N_GDN, Bp, GDN_V_HEADS, GDN_DK, GDN_DV), dt)
    z_conv = jnp.zeros((N_GDN, Bp, GDN_CONV - 1, GDN_CONV_DIM), dt)
    y_p, s5r_p, s5i_p, gla_p, gdn_p, conv_p = trunk(x_prompt, z_s5, z_s5, z_gla, z_gdn, z_conv, weights)
    y_s, s5r_s, s5i_s, gla_s, gdn_s, conv_s = trunk(x_sample, state_s5_re, state_s5_im, state_gla,
                                                    state_gdn, state_gdn_conv, weights)
    return (y_p, y_s, s5r_p, s5i_p, gla_p, gdn_p, conv_p, s5r_s, s5i_s, gla_s, gdn_s, conv_s)
```

```python
import functools
import math

import numpy as np
import jax
import jax.numpy as jnp
from jax import lax
from jax.experimental import pallas as pl
from jax.experimental.pallas import tpu as pltpu

F32 = jnp.float32
BF16 = jnp.bfloat16
HIGHEST = lax.Precision.HIGHEST
SDS = jax.ShapeDtypeStruct

D_MODEL = 1024
D_FF = 4 * D_MODEL
NORM_EPS = 1e-6

S5_GROUP = 16
S5_STATE = 64
S5_GROUPS = D_MODEL // S5_GROUP
S5_GB = 16
S5_NGB = S5_GROUPS // S5_GB
S5_BC = S5_GB * S5_GROUP
S5_BS = S5_GB * S5_STATE

GLA_HEADS = 4
GLA_DK = 128
GLA_DV = 256
GLA_KEY = GLA_HEADS * GLA_DK
GLA_VAL = GLA_HEADS * GLA_DV
GLA_RANK = 16
GLA_TAU = 16.0
GLA_TN = 512

GDN_DK = 128
GDN_DV = 128
GDN_QK_HEADS = 8
GDN_V_HEADS = 16
GDN_KEY = GDN_QK_HEADS * GDN_DK
GDN_VAL = GDN_V_HEADS * GDN_DV
GDN_CONV = 4
GDN_CONV_DIM = 2 * GDN_KEY + GDN_VAL
GDN_TN = 1024

CHUNK = 64
LANES = 128
SUBLANES = 8
TM = 1024
FFN_TF = 512
MIB = 1024 * 1024


def _cparams(n_axes, vmem_mib=48):
    return pltpu.CompilerParams(dimension_semantics=("arbitrary",) * n_axes,
                                vmem_limit_bytes=vmem_mib * MIB)


def _rms(x, gain):
    ms = jnp.mean(x * x, axis=-1, keepdims=True)
    return x * lax.rsqrt(ms + NORM_EPS) * gain


def _sigmoid(x):
    return 1.0 / (1.0 + jnp.exp(-x))


def _softplus(x):
    return jnp.maximum(x, 0.0) + jnp.log1p(jnp.exp(-jnp.abs(x)))


def _gelu_tanh(x):
    c = math.sqrt(2.0 / math.pi)
    return x * (0.5 * (1.0 + jnp.tanh(c * (x + 0.044715 * (x * x * x)))))


def _dot(a, b):
    return jnp.dot(a, b, preferred_element_type=F32)


def _dot_nt(a, b, precision=None):
    return lax.dot_general(a, b, (((1,), (1,)), ((), ())), precision=precision,
                           preferred_element_type=F32)


def _dot_tn(a, b):
    return lax.dot_general(a, b, (((0,), (0,)), ((), ())), preferred_element_type=F32)


def _rmsnorm_kernel(h_ref, g_ref, o_ref):
    o_ref[...] = _rms(h_ref[...], g_ref[...])


def rmsnorm_rows(h, gain):
    rows, d = h.shape
    return pl.pallas_call(
        _rmsnorm_kernel, grid=(rows // TM,),
        in_specs=[pl.BlockSpec((TM, d), lambda i: (i, 0)), pl.BlockSpec((1, d), lambda i: (0, 0))],
        out_specs=pl.BlockSpec((TM, d), lambda i: (i, 0)),
        out_shape=SDS((rows, d), F32), compiler_params=_cparams(1), name="rmsnorm",
    )(h, gain.reshape(1, d))


def _norm_matmul_kernel(h_ref, g_ref, w_ref, o_ref, hn_ref):
    @pl.when(pl.program_id(1) == 0)
    def _():
        hn_ref[...] = _rms(h_ref[...], g_ref[...]).astype(BF16)

    o_ref[0] = _dot(hn_ref[...], w_ref[...])


def norm_matmul(h, gain, w, tn):
    rows, d = h.shape
    n = w.shape[1]
    return pl.pallas_call(
        _norm_matmul_kernel, grid=(rows // TM, n // tn),
        in_specs=[pl.BlockSpec((TM, d), lambda i, j: (i, 0)),
                  pl.BlockSpec((1, d), lambda i, j: (0, 0)),
                  pl.BlockSpec((d, tn), lambda i, j: (0, j))],
        out_specs=pl.BlockSpec((1, TM, tn), lambda i, j: (j, i, 0)),
        out_shape=SDS((n // tn, rows, tn), F32),
        scratch_shapes=[pltpu.VMEM((TM, d), BF16)],
        compiler_params=_cparams(2), name="norm_matmul",
    )(h, gain.reshape(1, d), w)


def _ffn_kernel(h_ref, g_ref, wu_ref, wd_ref, fg_ref, o_ref, hn_ref, acc_ref, *, final_norm):
    j = pl.program_id(1)

    @pl.when(j == 0)
    def _():
        hn_ref[...] = _rms(h_ref[...], g_ref[...]).astype(BF16)
        acc_ref[...] = jnp.zeros_like(acc_ref)

    a = jnp.square(jnp.maximum(_dot(hn_ref[...], wu_ref[...]), 0.0)).astype(BF16)
    acc_ref[...] += _dot(a, wd_ref[...])

    @pl.when(j == pl.num_programs(1) - 1)
    def _():
        hnew = h_ref[...] + acc_ref[...]
        o_ref[...] = _rms(hnew, fg_ref[...]) if final_norm else hnew


def ffn(h, gain, w_up, w_down, final_gain, final_norm):
    rows, d = h.shape
    f = w_up.shape[1]
    return pl.pallas_call(
        functools.partial(_ffn_kernel, final_norm=final_norm), grid=(rows // TM, f // FFN_TF),
        in_specs=[pl.BlockSpec((TM, d), lambda i, j: (i, 0)),
                  pl.BlockSpec((1, d), lambda i, j: (0, 0)),
                  pl.BlockSpec((d, FFN_TF), lambda i, j: (0, j)),
                  pl.BlockSpec((FFN_TF, d), lambda i, j: (j, 0)),
                  pl.BlockSpec((1, d), lambda i, j: (0, 0))],
        out_specs=pl.BlockSpec((TM, d), lambda i, j: (i, 0)),
        out_shape=SDS((rows, d), F32),
        scratch_shapes=[pltpu.VMEM((TM, d), BF16), pltpu.VMEM((TM, d), F32)],
        compiler_params=_cparams(2), name="ffn",
    )(h, gain.reshape(1, d), w_up, w_down, final_gain.reshape(1, d))


def _gated_out_kernel(o_ref, z0_ref, z1_ref, gn_ref, w_ref, h_ref, out_ref, a_ref, *, nheads, hd):
    half = nheads // 2
    for hh in range(nheads):
        z_ref = z0_ref if hh < half else z1_ref
        c = (hh % half) * hd
        z = z_ref[0, :, c:c + hd]
        y = _rms(o_ref[:, hh * hd:(hh + 1) * hd], gn_ref[...])
        a_ref[:, hh * hd:(hh + 1) * hd] = (y * (z * _sigmoid(z))).astype(BF16)
    out_ref[...] = h_ref[...] + _dot(a_ref[...], w_ref[...])


def gated_out(o, p_tiles, z_tile0, gain, w_out, h, nheads, hd, tm):
    rows, d = h.shape
    kdim = nheads * hd
    tn = p_tiles.shape[2]
    return pl.pallas_call(
        functools.partial(_gated_out_kernel, nheads=nheads, hd=hd), grid=(rows // tm,),
        in_specs=[pl.BlockSpec((tm, kdim), lambda i: (i, 0)),
                  pl.BlockSpec((1, tm, tn), lambda i: (z_tile0, i, 0)),
                  pl.BlockSpec((1, tm, tn), lambda i: (z_tile0 + 1, i, 0)),
                  pl.BlockSpec((1, hd), lambda i: (0, 0)),
                  pl.BlockSpec((kdim, d), lambda i: (0, 0)),
                  pl.BlockSpec((tm, d), lambda i: (i, 0))],
        out_specs=pl.BlockSpec((tm, d), lambda i: (i, 0)),
        out_shape=SDS((rows, d), F32),
        scratch_shapes=[pltpu.VMEM((tm, kdim), BF16)],
        compiler_params=_cparams(1), name="gated_out",
    )(o, p_tiles, p_tiles, gain.reshape(1, hd), w_out, h)


def _s5_discretize_kernel(are_ref, aim_ref, ldt_ref, bre_ref, bim_ref,
                          abr_ref, abi_ref, bbr_ref, bbi_ref):
    a_re, a_im = are_ref[...], aim_ref[...]
    dt = jnp.exp(ldt_ref[...])
    mag = jnp.exp(a_re * dt)
    ab_re = mag * jnp.cos(a_im * dt)
    ab_im = mag * jnp.sin(a_im * dt)
    den = a_re * a_re + a_im * a_im
    c_re = ((ab_re - 1.0) * a_re + ab_im * a_im) / den
    c_im = (ab_im * a_re - (ab_re - 1.0) * a_im) / den
    abr_ref[...] = ab_re
    abi_ref[...] = ab_im
    bbr_ref[...] = c_re * bre_ref[...] - c_im * bim_ref[...]
    bbi_ref[...] = c_re * bim_ref[...] + c_im * bre_ref[...]


def s5_discretize(a_re, a_im, log_dt, b_re, b_im):
    g, p, c = S5_GROUPS, S5_STATE, S5_GROUP
    expand = lambda v: jnp.broadcast_to(v[..., None], (g, p, c)).reshape(g, p * c)
    ldt = jnp.broadcast_to(log_dt[:, None], (g, p * c))
    shp = SDS((g, p * c), F32)
    ab_re, ab_im, bb_re, bb_im = pl.pallas_call(
        _s5_discretize_kernel, out_shape=(shp, shp, shp, shp), name="s5_discretize",
    )(expand(a_re), expand(a_im), ldt, b_re.reshape(g, p * c), b_im.reshape(g, p * c))
    ab_re = ab_re.reshape(g, p, c)[:, :, 0].reshape(S5_NGB, 1, S5_BS)
    ab_im = ab_im.reshape(g, p, c)[:, :, 0].reshape(S5_NGB, 1, S5_BS)
    eye = jnp.eye(S5_GB, dtype=F32)

    def block_diag_in(bb):
        bb = bb.reshape(S5_NGB, S5_GB, p, c)
        return jnp.einsum("bgpc,gh->bgchp", bb, eye).reshape(S5_NGB, S5_BC, S5_BS)

    b_blk = jnp.concatenate([block_diag_in(bb_re.reshape(g, p, c)),
                             block_diag_in(bb_im.reshape(g, p, c))], axis=-1).astype(BF16)
    return ab_re, ab_im, b_blk


def s5_block_diag_out(c_par):
    eye = jnp.eye(S5_GB, dtype=F32)
    cc = c_par.reshape(S5_NGB, S5_GB, S5_GROUP, S5_STATE)
    return jnp.einsum("bgcp,gh->bgphc", cc, eye).reshape(S5_NGB, S5_BS, S5_BC).astype(BF16)


def _s5_scan_kernel(u_ref, b_ref, cre_ref, cim_ref, are_ref, aim_ref, d_ref, s0r_ref, s0i_ref,
                    g_ref, slr_ref, sli_ref, xr_ref, xi_ref, str_ref, sti_ref, *, nb, tc):
    n = pl.program_id(1)

    @pl.when(n == 0)
    def _():
        str_ref[...] = s0r_ref[...]
        sti_ref[...] = s0i_ref[...]

    u = u_ref[...]
    ub = u.astype(BF16)
    xr_ref[...] = _dot(ub, b_ref[0, :, :S5_BS])
    xi_ref[...] = _dot(ub, b_ref[0, :, S5_BS:])

    sub = SUBLANES
    a_re = jnp.broadcast_to(are_ref[0], (sub, S5_BS))
    a_im = jnp.broadcast_to(aim_ref[0], (sub, S5_BS))
    for rb in range(nb // sub):
        def step(t, carry, rb=rb):
            x_re, x_im = carry
            off = pl.multiple_of(t * nb + rb * sub, sub)
            n_re = a_re * x_re - a_im * x_im + xr_ref[pl.ds(off, sub), :]
            n_im = a_re * x_im + a_im * x_re + xi_ref[pl.ds(off, sub), :]
            xr_ref[pl.ds(off, sub), :] = n_re
            xi_ref[pl.ds(off, sub), :] = n_im
            return n_re, n_im

        rows = slice(rb * sub, (rb + 1) * sub)
        x_re, x_im = lax.fori_loop(0, tc, step, (str_ref[rows, :], sti_ref[rows, :]))
        str_ref[rows, :] = x_re
        sti_ref[rows, :] = x_im

    y = _dot(xr_ref[...].astype(BF16), cre_ref[0]) - _dot(xi_ref[...].astype(BF16), cim_ref[0])
    y = y + d_ref[...] * u
    g_ref[...] = _gelu_tanh(y).astype(BF16)

    @pl.when(n == pl.num_programs(1) - 1)
    def _():
        slr_ref[...] = str_ref[...]
        sli_ref[...] = sti_ref[...]


def s5_scan(hn, nb, b_blk, c_re_blk, c_im_blk, ab_re, ab_im, d_skip, s0_re, s0_im):
    rows, d = hn.shape
    tc = TM // nb
    st = SDS((nb, S5_GROUPS * S5_STATE), F32)
    return pl.pallas_call(
        functools.partial(_s5_scan_kernel, nb=nb, tc=tc), grid=(S5_NGB, rows // TM),
        in_specs=[pl.BlockSpec((TM, S5_BC), lambda gb, n: (n, gb)),
                  pl.BlockSpec((1, S5_BC, 2 * S5_BS), lambda gb, n: (gb, 0, 0)),
                  pl.BlockSpec((1, S5_BS, S5_BC), lambda gb, n: (gb, 0, 0)),
                  pl.BlockSpec((1, S5_BS, S5_BC), lambda gb, n: (gb, 0, 0)),
                  pl.BlockSpec((1, 1, S5_BS), lambda gb, n: (gb, 0, 0)),
                  pl.BlockSpec((1, 1, S5_BS), lambda gb, n: (gb, 0, 0)),
                  pl.BlockSpec((1, S5_BC), lambda gb, n: (0, gb)),
                  pl.BlockSpec((nb, S5_BS), lambda gb, n: (0, gb)),
                  pl.BlockSpec((nb, S5_BS), lambda gb, n: (0, gb))],
        out_specs=[pl.BlockSpec((TM, S5_BC), lambda gb, n: (n, gb)),
                   pl.BlockSpec((nb, S5_BS), lambda gb, n: (0, gb)),
                   pl.BlockSpec((nb, S5_BS), lambda gb, n: (0, gb))],
        out_shape=(SDS((rows, d), BF16), st, st),
        scratch_shapes=[pltpu.VMEM((TM, S5_BS), F32), pltpu.VMEM((TM, S5_BS), F32),
                        pltpu.VMEM((nb, S5_BS), F32), pltpu.VMEM((nb, S5_BS), F32)],
        compiler_params=_cparams(2), name="s5_scan",
    )(hn, b_blk, c_re_blk, c_im_blk, ab_re, ab_im, d_skip.reshape(1, d), s0_re, s0_im)


def _glu_out_kernel(g_ref, w_ref, h_ref, o_ref):
    gv = _dot(g_ref[...], w_ref[...])
    o_ref[...] = h_ref[...] + gv[:, :D_MODEL] * _sigmoid(gv[:, D_MODEL:])


def glu_out(g, w_glu, h):
    rows, d = h.shape
    tm = TM // 2
    return pl.pallas_call(
        _glu_out_kernel, grid=(rows // tm,),
        in_specs=[pl.BlockSpec((tm, d), lambda i: (i, 0)),
                  pl.BlockSpec((d, 2 * d), lambda i: (0, 0)),
                  pl.BlockSpec((tm, d), lambda i: (i, 0))],
        out_specs=pl.BlockSpec((tm, d), lambda i: (i, 0)),
        out_shape=SDS((rows, d), F32), compiler_params=_cparams(1), name="glu_out",
    )(g, w_glu, h)


def s5_layer(h, nb, gain, s0_re, s0_im, a_re, a_im, log_dt, b_re, b_im, c_re, c_im, d_skip, w_glu):
    hn = rmsnorm_rows(h, gain)
    ab_re, ab_im, b_blk = s5_discretize(a_re, a_im, log_dt, b_re, b_im)
    g, sl_re, sl_im = s5_scan(hn, nb, b_blk, s5_block_diag_out(c_re), s5_block_diag_out(c_im),
                              ab_re, ab_im, d_skip, s0_re.reshape(nb, -1), s0_im.reshape(nb, -1))
    shape = (nb, S5_GROUPS, S5_STATE)
    return glu_out(g, w_glu.astype(BF16), h), sl_re.reshape(shape), sl_im.reshape(shape)


def _levels(c):
    return [2 ** i for i in range(1, int(math.log2(c)) + 1)]


def _level_masks(c):
    t = np.arange(c)[:, None]
    s = np.arange(c)[None, :]
    ms = [t == s]
    for sz in _levels(c):
        ms.append((t // sz == s // sz) & (t % sz >= sz // 2) & (s % sz < sz // 2))
    return jnp.asarray(np.stack(ms).astype(np.float32))


def _cumsum_pivot_matrix(c):
    t = np.arange(c)[:, None]
    r = np.arange(c)[None, :]
    ws = [r <= t]
    for sz in _levels(c):
        ws.append(r <= (t // sz) * sz + sz // 2 - 1)
    return jnp.asarray(np.concatenate(ws, 0).astype(np.float32))


def _gla_gate_kernel(gl_ref, w_ref, b_ref, o_ref):
    x = _dot(gl_ref[0].astype(BF16), w_ref[...]) + b_ref[...]
    o_ref[...] = -_softplus(-x) * (1.0 / GLA_TAU)


def gla_gate(gl_tiles, w_gate_pad, b_gate):
    rows = gl_tiles.shape[1]
    return pl.pallas_call(
        _gla_gate_kernel, grid=(rows // TM,),
        in_specs=[pl.BlockSpec((1, TM, LANES), lambda i: (0, i, 0)),
                  pl.BlockSpec((LANES, GLA_KEY), lambda i: (0, 0)),
                  pl.BlockSpec((1, GLA_KEY), lambda i: (0, 0))],
        out_specs=pl.BlockSpec((TM, GLA_KEY), lambda i: (i, 0)),
        out_shape=SDS((rows, GLA_KEY), F32), compiler_params=_cparams(1), name="gla_gate",
    )(gl_tiles, w_gate_pad, b_gate.reshape(1, GLA_KEY))


def _gla_chunk_kernel(q_ref, k_ref, v_ref, g_ref, wst_ref, mask_ref, s0_ref, o_ref, sout_ref, s_ref,
                      *, c):
    n = pl.program_id(2)

    @pl.when(n == 0)
    def _():
        s_ref[...] = s0_ref[0, 0]

    q = q_ref[0] * (GLA_DK ** -0.5)
    k = k_ref[0]
    vb = v_ref[0].astype(BF16)
    bp = jnp.dot(wst_ref[...], g_ref[...], precision=HIGHEST, preferred_element_type=F32)
    b = bp[0:c]
    att = mask_ref[0] * _dot_nt(q.astype(BF16), k.astype(BF16))
    for lv in range(1, mask_ref.shape[0]):
        e = jnp.exp(-jnp.abs(b - bp[lv * c:(lv + 1) * c]))
        att = att + mask_ref[lv] * _dot_nt((q * e).astype(BF16), (k * e).astype(BF16))
    s = s_ref[...]
    o_ref[...] = _dot((q * jnp.exp(b)).astype(BF16), s.astype(BF16)) + _dot(att.astype(BF16), vb)

    b_last = b[c - 1:c, :]
    k_dec = (k * jnp.exp(b_last - b)).astype(BF16)
    eye = (lax.broadcasted_iota(jnp.int32, (GLA_DK, GLA_DK), 0)
           == lax.broadcasted_iota(jnp.int32, (GLA_DK, GLA_DK), 1)).astype(F32)
    col = _dot_nt(eye, jnp.broadcast_to(b_last, (GLA_DK, GLA_DK)), precision=HIGHEST)
    dec = jnp.exp(col)
    s_new = jnp.concatenate([dec, dec], axis=1) * s + _dot_tn(k_dec, vb)
    s_ref[...] = s_new

    @pl.when(n == pl.num_programs(2) - 1)
    def _():
        sout_ref[0, 0] = s_new


def gla_chunks(p_tiles, g, s0, nb, seq, c):
    rows = p_tiles.shape[1]
    pv = p_tiles.reshape(p_tiles.shape[0], seq, nb * GLA_TN)
    gv = g.reshape(seq, nb * GLA_KEY)
    nl = len(_levels(c)) + 1
    hq = GLA_TN // GLA_DK
    hv = GLA_TN // GLA_DV
    o, s_out = pl.pallas_call(
        functools.partial(_gla_chunk_kernel, c=c), grid=(nb, GLA_HEADS, seq // c),
        in_specs=[pl.BlockSpec((1, c, GLA_DK), lambda b, h, n: (0, n, b * hq + h)),
                  pl.BlockSpec((1, c, GLA_DK), lambda b, h, n: (1, n, b * hq + h)),
                  pl.BlockSpec((1, c, GLA_DV), lambda b, h, n: (2 + h // hv, n, b * hv + h % hv)),
                  pl.BlockSpec((c, GLA_DK), lambda b, h, n: (n, b * GLA_HEADS + h)),
                  pl.BlockSpec((nl * c, c), lambda b, h, n: (0, 0)),
                  pl.BlockSpec((nl, c, c), lambda b, h, n: (0, 0, 0)),
                  pl.BlockSpec((1, 1, GLA_DK, GLA_DV), lambda b, h, n: (b, h, 0, 0))],
        out_specs=[pl.BlockSpec((c, GLA_DV), lambda b, h, n: (n, b * GLA_HEADS + h)),
                   pl.BlockSpec((1, 1, GLA_DK, GLA_DV), lambda b, h, n: (b, h, 0, 0))],
        out_shape=(SDS((seq, nb * GLA_VAL), F32), SDS((nb, GLA_HEADS, GLA_DK, GLA_DV), F32)),
        scratch_shapes=[pltpu.VMEM((GLA_DK, GLA_DV), F32)],
        compiler_params=_cparams(3), name="gla_chunks",
    )(pv, pv, pv, gv, _cumsum_pivot_matrix(c), _level_masks(c), s0)
    return o.reshape(rows, GLA_VAL), s_out


def gla_layer(h, nb, seq, gain, s0, w_in, w_gate_up, b_gate, norm_o, w_out):
    c = CHUNK if seq % CHUNK == 0 else seq
    n_main = 2 * GLA_KEY + 2 * GLA_VAL
    p_tiles = norm_matmul(h, gain, w_in[:, :n_main].astype(BF16), GLA_TN)
    w_gl = jnp.pad(w_in[:, n_main:], ((0, 0), (0, LANES - GLA_RANK))).astype(BF16)
    gl_tiles = norm_matmul(h, gain, w_gl, LANES)
    w_gate_pad = jnp.pad(w_gate_up, ((0, LANES - GLA_RANK), (0, 0))).astype(BF16)
    g = gla_gate(gl_tiles, w_gate_pad, b_gate)
    o, s_out = gla_chunks(p_tiles, g, s0, nb, seq, c)
    h = gated_out(o, p_tiles, 4, norm_o, w_out.astype(BF16), h, GLA_HEADS, GLA_DV, TM)
    return h, s_out


def _gdn_conv_kernel(x_ref, cw_ref, c0_ref, o_ref, nc_ref, xp_ref, *, nb, tm, normalize):
    i = pl.program_id(1)
    halo = (GDN_CONV - 1) * nb

    @pl.when(i == 0)
    def _():
        xp_ref[0:halo, :] = c0_ref[...]

    xp_ref[halo:halo + tm, :] = x_ref[0]
    acc = cw_ref[0:1, :] * xp_ref[0:tm, :]
    for j in range(1, GDN_CONV):
        acc = acc + cw_ref[j:j + 1, :] * xp_ref[j * nb:j * nb + tm, :]
    y = acc * _sigmoid(acc)
    if normalize:
        scale = jnp.where(pl.program_id(0) == 0, GDN_DK ** -0.5, 1.0)
        for hh in range(y.shape[1] // GDN_DK):
            yh = y[:, hh * GDN_DK:(hh + 1) * GDN_DK]
            inv = lax.rsqrt(jnp.sum(yh * yh, axis=-1, keepdims=True) + NORM_EPS)
            o_ref[0, :, hh * GDN_DK:(hh + 1) * GDN_DK] = yh * (inv * scale)
    else:
        o_ref[0] = y
    tail = xp_ref[tm:tm + halo, :]
    xp_ref[0:halo, :] = tail

    @pl.when(i == pl.num_programs(1) - 1)
    def _():
        nc_ref[...] = tail


def gdn_conv(p_tiles, tile0, conv_w, conv0, nb, normalize):
    _, rows, tn = p_tiles.shape
    halo = (GDN_CONV - 1) * nb
    tm = TM
    assert tm >= halo and rows >= halo
    return pl.pallas_call(
        functools.partial(_gdn_conv_kernel, nb=nb, tm=tm, normalize=normalize), grid=(2, rows // tm),
        in_specs=[pl.BlockSpec((1, tm, tn), lambda j, i: (tile0 + j, i, 0)),
                  pl.BlockSpec((GDN_CONV, tn), lambda j, i: (0, tile0 + j)),
                  pl.BlockSpec((halo, tn), lambda j, i: (0, tile0 + j))],
        out_specs=[pl.BlockSpec((1, tm, tn), lambda j, i: (j, i, 0)),
                   pl.BlockSpec((halo, tn), lambda j, i: (0, j))],
        out_shape=(SDS((2, rows, tn), F32), SDS((halo, 2 * tn), F32)),
        scratch_shapes=[pltpu.VMEM((halo + tm, tn), F32)],
        compiler_params=_cparams(2), name="gdn_conv",
    )(p_tiles, conv_w, conv0)


def _gdn_chunk_kernel(q_ref, k_ref, v_ref, ba_ref, alog_ref, dtb_ref, tri_ref, mask_ref, s0_ref,
                      o_ref, sout_ref, s_ref, *, c):
    hp = pl.program_id(1)
    n = pl.program_id(2)

    @pl.when(n == 0)
    def _():
        s_ref[...] = s0_ref[0]

    ba = ba_ref[...]
    beta_all = _sigmoid(ba)
    g_all = -jnp.exp(alog_ref[...]) * _softplus(ba + dtb_ref[...])
    tri = tri_ref[...]
    gc_all = jnp.dot(tri, g_all, precision=HIGHEST, preferred_element_type=F32)
    eye = (lax.broadcasted_iota(jnp.int32, (LANES, LANES), 0)
           == lax.broadcasted_iota(jnp.int32, (LANES, LANES), 1)).astype(F32)
    g_t = _dot_nt(eye, g_all, precision=HIGHEST)
    gr_all = _dot_nt(g_t, tri, precision=HIGHEST)

    lane = lax.broadcasted_iota(jnp.int32, (c, LANES), 1)
    subl = lax.broadcasted_iota(jnp.int32, (LANES, c), 0)
    qb = q_ref[0].astype(BF16)
    k = k_ref[0]
    kb = k.astype(BF16)
    kk = _dot_nt(kb, kb)
    qk = _dot_nt(qb, kb)
    strict = tri - mask_ref[0]
    nlv = mask_ref.shape[0]
    for j in range(2):
        hh = 2 * hp + j
        beta = jnp.sum(jnp.where(lane == hh, beta_all, 0.0), axis=1, keepdims=True)
        g_col = jnp.sum(jnp.where(lane == GDN_V_HEADS + hh, gc_all, 0.0), axis=1, keepdims=True)
        g_row = jnp.sum(jnp.where(subl == GDN_V_HEADS + hh, gr_all, 0.0), axis=0, keepdims=True)
        decay = tri * jnp.exp(jnp.minimum(g_col - g_row, 0.0))
        m = strict * (kk * decay * beta)
        t_inv = mask_ref[0] - mask_ref[1] * m
        for lv in range(2, nlv):
            mo = mask_ref[lv] * m
            tm_ = jnp.dot(t_inv, mo, precision=HIGHEST, preferred_element_type=F32)
            t_inv = t_inv - jnp.dot(tm_, t_inv, precision=HIGHEST, preferred_element_type=F32)
        v = v_ref[0, :, j * GDN_DV:(j + 1) * GDN_DV]
        e_g = jnp.exp(g_col)
        rhs = jnp.concatenate([v * beta, k * (beta * e_g)], axis=1)
        uw = jnp.dot(t_inv, rhs, precision=HIGHEST, preferred_element_type=F32)
        s = s_ref[j]
        sb = s.astype(BF16)
        v_new = uw[:, :GDN_DV] - _dot(uw[:, GDN_DV:].astype(BF16), sb)
        vnb = v_new.astype(BF16)
        q_dec = (q_ref[0] * e_g).astype(BF16)
        o_ref[:, j * GDN_DV:(j + 1) * GDN_DV] = _dot(q_dec, sb) + _dot((qk * decay).astype(BF16), vnb)
        g_last = g_col[c - 1:c, :]
        k_dec = (k * jnp.exp(g_last - g_col)).astype(BF16)
        s_new = jnp.exp(g_last) * s + _dot_tn(k_dec, vnb)
        s_ref[j] = s_new

    @pl.when(n == pl.num_programs(2) - 1)
    def _():
        sout_ref[0] = s_ref[...]


def gdn_chunks(qk_tiles, v_tiles, ba, a_log, dt_bias, s0, nb, seq, c):
    rows = qk_tiles.shape[1]
    qkv_ = qk_tiles.reshape(2, seq, nb * GDN_TN)
    vv = v_tiles.reshape(2, seq, nb * GDN_TN)
    bav = ba.reshape(seq, nb * LANES)
    pad = lambda x: jnp.pad(x, (GDN_V_HEADS, LANES - 2 * GDN_V_HEADS)).reshape(1, LANES)
    masks = _level_masks(c)
    tri = jnp.asarray(np.tril(np.ones((c, c), np.float32)))
    nl = masks.shape[0]
    pairs = GDN_TN // (2 * GDN_DV)
    o, s_out = pl.pallas_call(
        functools.partial(_gdn_chunk_kernel, c=c), grid=(nb, GDN_QK_HEADS, seq // c),
        in_specs=[pl.BlockSpec((1, c, GDN_DK), lambda b, h, n: (0, n, b * GDN_QK_HEADS + h)),
                  pl.BlockSpec((1, c, GDN_DK), lambda b, h, n: (1, n, b * GDN_QK_HEADS + h)),
                  pl.BlockSpec((1, c, 2 * GDN_DV), lambda b, h, n: (h // pairs, n, b * pairs + h % pairs)),
                  pl.BlockSpec((c, LANES), lambda b, h, n: (n, b)),
                  pl.BlockSpec((1, LANES), lambda b, h, n: (0, 0)),
                  pl.BlockSpec((1, LANES), lambda b, h, n: (0, 0)),
                  pl.BlockSpec((c, c), lambda b, h, n: (0, 0)),
                  pl.BlockSpec((nl, c, c), lambda b, h, n: (0, 0, 0)),
                  pl.BlockSpec((1, 2, GDN_DK, GDN_DV), lambda b, h, n: (b, h, 0, 0))],
        out_specs=[pl.BlockSpec((c, 2 * GDN_DV), lambda b, h, n: (n, b * GDN_QK_HEADS + h)),
                   pl.BlockSpec((1, 2, GDN_DK, GDN_DV), lambda b, h, n: (b, h, 0, 0))],
        out_shape=(SDS((seq, nb * GDN_VAL), F32), SDS((nb, GDN_V_HEADS, GDN_DK, GDN_DV), F32)),
        scratch_shapes=[pltpu.VMEM((2, GDN_DK, GDN_DV), F32)],
        compiler_params=_cparams(3), name="gdn_chunks",
    )(qkv_, qkv_, vv, bav, pad(a_log), pad(dt_bias), tri, masks, s0)
    return o.reshape(rows, GDN_VAL), s_out


def gdn_layer(h, nb, seq, gain, s0, conv0, w_in, conv_w, a_log, dt_bias, norm_o, w_out):
    c = CHUNK if seq % CHUNK == 0 else seq
    n_main = GDN_CONV_DIM + GDN_VAL
    p_tiles = norm_matmul(h, gain, w_in[:, :n_main].astype(BF16), GDN_TN)
    w_ba = jnp.pad(w_in[:, n_main:], ((0, 0), (0, LANES - 2 * GDN_V_HEADS))).astype(BF16)
    ba = norm_matmul(h, gain, w_ba, LANES)[0]
    conv0_tm = conv0.transpose(1, 0, 2).reshape((GDN_CONV - 1) * nb, GDN_CONV_DIM)
    qk_tiles, nc_qk = gdn_conv(p_tiles, 0, conv_w, conv0_tm, nb, True)
    v_tiles, nc_v = gdn_conv(p_tiles, 2, conv_w, conv0_tm, nb, False)
    o, s_out = gdn_chunks(qk_tiles, v_tiles, ba, a_log, dt_bias, s0, nb, seq, c)
    h = gated_out(o, p_tiles, 4, norm_o, w_out.astype(BF16), h, GDN_V_HEADS, GDN_DV, TM // 2)
    new_conv = jnp.concatenate([nc_qk, nc_v], axis=1).reshape(GDN_CONV - 1, nb, GDN_CONV_DIM)
    return h, s_out, new_conv.transpose(1, 0, 2)


def _trunk(x, s5_re, s5_im, gla_s, gdn_s, gdn_conv_s, w):
    nb, seq, d = x.shape
    h = x.transpose(1, 0, 2).reshape(seq * nb, d)
    h, s5r0, s5i0 = s5_layer(h, nb, w["norm_mix"][0], s5_re[0], s5_im[0], w["s5_a_re"][0], w["s5_a_im"][0],
                             w["s5_log_dt"][0], w["s5_b_re"][0], w["s5_b_im"][0], w["s5_c_re"][0],
                             w["s5_c_im"][0], w["s5_d"][0], w["s5_w_glu"][0])
    h = ffn(h, w["norm_ffn"][0], w["w_up"][0], w["w_down"][0], w["norm_final"], False)
    h, gla_o = gla_layer(h, nb, seq, w["norm_mix"][1], gla_s[0], w["gla_w_in"][0], w["gla_w_gate_up"][0],
                         w["gla_b_gate"][0], w["gla_norm"][0], w["gla_w_out"][0])
    h = ffn(h, w["norm_ffn"][1], w["w_up"][1], w["w_down"][1], w["norm_final"], False)
    h, gdn_o, conv_o = gdn_layer(h, nb, seq, w["norm_mix"][2], gdn_s[0], gdn_conv_s[0], w["gdn_w_in"][0],
                                 w["gdn_conv_w"][0], w["gdn_a_log"][0], w["gdn_dt_bias"][0],
                                 w["gdn_norm"][0], w["gdn_w_out"][0])
    h = ffn(h, w["norm_ffn"][2], w["w_up"][2], w["w_down"][2], w["norm_final"], False)
    h, s5r1, s5i1 = s5_layer(h, nb, w["norm_mix"][3], s5_re[1], s5_im[1], w["s5_a_re"][1], w["s5_a_im"][1],
                             w["s5_log_dt"][1], w["s5_b_re"][1], w["s5_b_im"][1], w["s5_c_re"][1],
                             w["s5_c_im"][1], w["s5_d"][1], w["s5_w_glu"][1])
    y = ffn(h, w["norm_ffn"][3], w["w_up"][3], w["w_down"][3], w["norm_final"], True)
    y = y.reshape(seq, nb, d).transpose(1, 0, 2)
    return (y, jnp.stack([s5r0, s5r1]), jnp.stack([s5i0, s5i1]), gla_o[None], gdn_o[None], conv_o[None])


def kernel(x_prompt, x_sample, state_s5_re, state_s5_im, state_gla, state_gdn, state_gdn_conv, norm_mix, norm_ffn, norm_final, w_up, w_down, s5_a_re, s5_a_im, s5_log_dt, s5_b_re, s5_b_im, s5_c_re, s5_c_im, s5_d, s5_w_glu, gla_w_in, gla_w_gate_up, gla_b_gate, gla_norm, gla_w_out, gdn_w_in, gdn_conv_w, gdn_a_log, gdn_dt_bias, gdn_norm, gdn_w_out):
    w = dict(norm_mix=norm_mix, norm_ffn=norm_ffn, norm_final=norm_final,
             w_up=w_up.astype(BF16), w_down=w_down.astype(BF16),
             s5_a_re=s5_a_re, s5_a_im=s5_a_im, s5_log_dt=s5_log_dt, s5_b_re=s5_b_re, s5_b_im=s5_b_im,
             s5_c_re=s5_c_re, s5_c_im=s5_c_im, s5_d=s5_d, s5_w_glu=s5_w_glu,
             gla_w_in=gla_w_in, gla_w_gate_up=gla_w_gate_up, gla_b_gate=gla_b_gate, gla_norm=gla_norm,
             gla_w_out=gla_w_out, gdn_w_in=gdn_w_in, gdn_conv_w=gdn_conv_w, gdn_a_log=gdn_a_log,
             gdn_dt_bias=gdn_dt_bias, gdn_norm=gdn_norm, gdn_w_out=gdn_w_out)
    bp = x_prompt.shape[0]
    dt = x_prompt.dtype
    z_s5 = jnp.zeros((state_s5_re.shape[0], bp) + state_s5_re.shape[2:], dt)
    z_gla = jnp.zeros((state_gla.shape[0], bp) + state_gla.shape[2:], dt)
    z_gdn = jnp.zeros((state_gdn.shape[0], bp) + state_gdn.shape[2:], dt)
    z_conv = jnp.zeros((state_gdn_conv.shape[0], bp) + state_gdn_conv.shape[2:], dt)
    out_p = _trunk(x_prompt, z_s5, z_s5, z_gla, z_gdn, z_conv, w)
    out_s = _trunk(x_sample, state_s5_re, state_s5_im, state_gla, state_gdn, state_gdn_conv, w)
    return (out_p[0], out_s[0]) + out_p[1:] + out_s[1:]
```

```python
import functools
import math

import numpy as np
import jax
import jax.numpy as jnp
from jax import lax
from jax.experimental import pallas as pl
from jax.experimental.pallas import tpu as pltpu

F32 = jnp.float32
BF16 = jnp.bfloat16
HIGHEST = lax.Precision.HIGHEST
SDS = jax.ShapeDtypeStruct

D_MODEL = 1024
D_FF = 4 * D_MODEL
NORM_EPS = 1e-6

S5_GROUP = 16
S5_STATE = 64
S5_GROUPS = D_MODEL // S5_GROUP
S5_GB = 16
S5_NGB = S5_GROUPS // S5_GB
S5_BC = S5_GB * S5_GROUP
S5_BS = S5_GB * S5_STATE

GLA_HEADS = 4
GLA_DK = 128
GLA_DV = 256
GLA_KEY = GLA_HEADS * GLA_DK
GLA_VAL = GLA_HEADS * GLA_DV
GLA_RANK = 16
GLA_TAU = 16.0
GLA_TN = 512

GDN_DK = 128
GDN_DV = 128
GDN_QK_HEADS = 8
GDN_V_HEADS = 16
GDN_KEY = GDN_QK_HEADS * GDN_DK
GDN_VAL = GDN_V_HEADS * GDN_DV
GDN_CONV = 4
GDN_CONV_DIM = 2 * GDN_KEY + GDN_VAL
GDN_TN = 1024

CHUNK = 64
LANES = 128
SUBLANES = 8
TM = 1024
FFN_TF = 512
MIB = 1024 * 1024


def _cparams(n_axes, vmem_mib=48):
    return pltpu.CompilerParams(dimension_semantics=("arbitrary",) * n_axes,
                                vmem_limit_bytes=vmem_mib * MIB)


def _rms(x, gain):
    ms = jnp.mean(x * x, axis=-1, keepdims=True)
    return x * lax.rsqrt(ms + NORM_EPS) * gain


def _sigmoid(x):
    return 1.0 / (1.0 + jnp.exp(-x))


def _softplus(x):
    return jnp.maximum(x, 0.0) + jnp.log1p(jnp.exp(-jnp.abs(x)))


def _gelu_tanh(x):
    c = math.sqrt(2.0 / math.pi)
    return x * (0.5 * (1.0 + jnp.tanh(c * (x + 0.044715 * (x * x * x)))))


def _dot(a, b):
    return jnp.dot(a, b, preferred_element_type=F32)


def _dot_nt(a, b, precision=None):
    return lax.dot_general(a, b, (((1,), (1,)), ((), ())), precision=precision,
                           preferred_element_type=F32)


def _dot_tn(a, b):
    return lax.dot_general(a, b, (((0,), (0,)), ((), ())), preferred_element_type=F32)


def _eye(n):
    return (lax.broadcasted_iota(jnp.int32, (n, n), 0)
            == lax.broadcasted_iota(jnp.int32, (n, n), 1)).astype(F32)


def _split3(x):
    x1 = x.astype(BF16)
    r1 = x - x1.astype(F32)
    x2 = r1.astype(BF16)
    x3 = (r1 - x2.astype(F32)).astype(BF16)
    return x1, x2, x3


def _dot_exact01(m01, x):
    mb = m01.astype(BF16)
    x1, x2, x3 = _split3(x)
    return _dot(mb, x1) + _dot(mb, x2) + _dot(mb, x3)


def _dot_nt_exact01_rhs(x, m01):
    mb = m01.astype(BF16)
    x1, x2, x3 = _split3(x)
    return _dot_nt(x1, mb) + _dot_nt(x2, mb) + _dot_nt(x3, mb)


def _dot_nt_exact01(m01, x):
    mb = m01.astype(BF16)
    x1, x2, x3 = _split3(x)
    return _dot_nt(mb, x1) + _dot_nt(mb, x2) + _dot_nt(mb, x3)


def _rmsnorm_kernel(h_ref, g_ref, o_ref):
    o_ref[...] = _rms(h_ref[...], g_ref[...])


def rmsnorm_rows(h, gain):
    rows, d = h.shape
    return pl.pallas_call(
        _rmsnorm_kernel, grid=(rows // TM,),
        in_specs=[pl.BlockSpec((TM, d), lambda i: (i, 0)), pl.BlockSpec((1, d), lambda i: (0, 0))],
        out_specs=pl.BlockSpec((TM, d), lambda i: (i, 0)),
        out_shape=SDS((rows, d), F32), compiler_params=_cparams(1), name="rmsnorm",
    )(h, gain.reshape(1, d))


def _norm_matmul_kernel(h_ref, g_ref, w_ref, o_ref, hn_ref):
    @pl.when(pl.program_id(1) == 0)
    def _():
        hn_ref[...] = _rms(h_ref[...], g_ref[...]).astype(BF16)

    o_ref[0] = _dot(hn_ref[...], w_ref[...])


def norm_matmul(h, gain, w, tn):
    rows, d = h.shape
    n = w.shape[1]
    return pl.pallas_call(
        _norm_matmul_kernel, grid=(rows // TM, n // tn),
        in_specs=[pl.BlockSpec((TM, d), lambda i, j: (i, 0)),
                  pl.BlockSpec((1, d), lambda i, j: (0, 0)),
                  pl.BlockSpec((d, tn), lambda i, j: (0, j))],
        out_specs=pl.BlockSpec((1, TM, tn), lambda i, j: (j, i, 0)),
        out_shape=SDS((n // tn, rows, tn), F32),
        scratch_shapes=[pltpu.VMEM((TM, d), BF16)],
        compiler_params=_cparams(2), name="norm_matmul",
    )(h, gain.reshape(1, d), w)


def _ffn_kernel(h_ref, g_ref, wu_ref, wd_ref, fg_ref, o_ref, hn_ref, acc_ref, *, final_norm):
    j = pl.program_id(1)

    @pl.when(j == 0)
    def _():
        hn_ref[...] = _rms(h_ref[...], g_ref[...]).astype(BF16)
        acc_ref[...] = jnp.zeros_like(acc_ref)

    a = jnp.square(jnp.maximum(_dot(hn_ref[...], wu_ref[...]), 0.0)).astype(BF16)
    acc_ref[...] += _dot(a, wd_ref[...])

    @pl.when(j == pl.num_programs(1) - 1)
    def _():
        hnew = h_ref[...] + acc_ref[...]
        o_ref[...] = _rms(hnew, fg_ref[...]) if final_norm else hnew


def ffn(h, gain, w_up, w_down, final_gain, final_norm):
    rows, d = h.shape
    f = w_up.shape[1]
    return pl.pallas_call(
        functools.partial(_ffn_kernel, final_norm=final_norm), grid=(rows // TM, f // FFN_TF),
        in_specs=[pl.BlockSpec((TM, d), lambda i, j: (i, 0)),
                  pl.BlockSpec((1, d), lambda i, j: (0, 0)),
                  pl.BlockSpec((d, FFN_TF), lambda i, j: (0, j)),
                  pl.BlockSpec((FFN_TF, d), lambda i, j: (j, 0)),
                  pl.BlockSpec((1, d), lambda i, j: (0, 0))],
        out_specs=pl.BlockSpec((TM, d), lambda i, j: (i, 0)),
        out_shape=SDS((rows, d), F32),
        scratch_shapes=[pltpu.VMEM((TM, d), BF16), pltpu.VMEM((TM, d), F32)],
        compiler_params=_cparams(2), name="ffn",
    )(h, gain.reshape(1, d), w_up, w_down, final_gain.reshape(1, d))


def _gated_out_kernel(o_ref, z0_ref, z1_ref, gn_ref, w_ref, h_ref, out_ref, a_ref, *, nheads, hd):
    half = nheads // 2
    for hh in range(nheads):
        z_ref = z0_ref if hh < half else z1_ref
        c = (hh % half) * hd
        z = z_ref[0, :, c:c + hd]
        y = _rms(o_ref[:, hh * hd:(hh + 1) * hd], gn_ref[...])
        a_ref[:, hh * hd:(hh + 1) * hd] = (y * (z * _sigmoid(z))).astype(BF16)
    out_ref[...] = h_ref[...] + _dot(a_ref[...], w_ref[...])


def gated_out(o, p_tiles, z_tile0, gain, w_out, h, nheads, hd, tm):
    rows, d = h.shape
    kdim = nheads * hd
    tn = p_tiles.shape[2]
    return pl.pallas_call(
        functools.partial(_gated_out_kernel, nheads=nheads, hd=hd), grid=(rows // tm,),
        in_specs=[pl.BlockSpec((tm, kdim), lambda i: (i, 0)),
                  pl.BlockSpec((1, tm, tn), lambda i: (z_tile0, i, 0)),
                  pl.BlockSpec((1, tm, tn), lambda i: (z_tile0 + 1, i, 0)),
                  pl.BlockSpec((1, hd), lambda i: (0, 0)),
                  pl.BlockSpec((kdim, d), lambda i: (0, 0)),
                  pl.BlockSpec((tm, d), lambda i: (i, 0))],
        out_specs=pl.BlockSpec((tm, d), lambda i: (i, 0)),
        out_shape=SDS((rows, d), F32),
        scratch_shapes=[pltpu.VMEM((tm, kdim), BF16)],
        compiler_params=_cparams(1), name="gated_out",
    )(o, p_tiles, p_tiles, gain.reshape(1, hd), w_out, h)


def _s5_discretize_kernel(are_ref, aim_ref, ldt_ref, bre_ref, bim_ref,
                          abr_ref, abi_ref, bbr_ref, bbi_ref):
    a_re, a_im = are_ref[...], aim_ref[...]
    dt = jnp.exp(ldt_ref[...])
    mag = jnp.exp(a_re * dt)
    ab_re = mag * jnp.cos(a_im * dt)
    ab_im = mag * jnp.sin(a_im * dt)
    den = a_re * a_re + a_im * a_im
    c_re = ((ab_re - 1.0) * a_re + ab_im * a_im) / den
    c_im = (ab_im * a_re - (ab_re - 1.0) * a_im) / den
    abr_ref[...] = ab_re
    abi_ref[...] = ab_im
    bbr_ref[...] = c_re * bre_ref[...] - c_im * bim_ref[...]
    bbi_ref[...] = c_re * bim_ref[...] + c_im * bre_ref[...]


def s5_discretize(a_re, a_im, log_dt, b_re, b_im):
    g, p, c = S5_GROUPS, S5_STATE, S5_GROUP
    expand = lambda v: jnp.broadcast_to(v[..., None], (g, p, c)).reshape(g, p * c)
    ldt = jnp.broadcast_to(log_dt[:, None], (g, p * c))
    shp = SDS((g, p * c), F32)
    ab_re, ab_im, bb_re, bb_im = pl.pallas_call(
        _s5_discretize_kernel, out_shape=(shp, shp, shp, shp), name="s5_discretize",
    )(expand(a_re), expand(a_im), ldt, b_re.reshape(g, p * c), b_im.reshape(g, p * c))
    ab_re = ab_re.reshape(g, p, c)[:, :, 0].reshape(S5_NGB, 1, S5_BS)
    ab_im = ab_im.reshape(g, p, c)[:, :, 0].reshape(S5_NGB, 1, S5_BS)
    eye = jnp.eye(S5_GB, dtype=F32)

    def block_diag_in(bb):
        bb = bb.reshape(S5_NGB, S5_GB, p, c)
        return jnp.einsum("bgpc,gh->bgchp", bb, eye).reshape(S5_NGB, S5_BC, S5_BS)

    b_blk = jnp.concatenate([block_diag_in(bb_re.reshape(g, p, c)),
                             block_diag_in(bb_im.reshape(g, p, c))], axis=-1).astype(BF16)
    return ab_re, ab_im, b_blk


def s5_block_diag_out(c_par):
    eye = jnp.eye(S5_GB, dtype=F32)
    cc = c_par.reshape(S5_NGB, S5_GB, S5_GROUP, S5_STATE)
    return jnp.einsum("bgcp,gh->bgphc", cc, eye).reshape(S5_NGB, S5_BS, S5_BC).astype(BF16)


def _s5_scan_kernel(u_ref, b_ref, cre_ref, cim_ref, are_ref, aim_ref, d_ref, s0r_ref, s0i_ref,
                    g_ref, slr_ref, sli_ref, xr_ref, xi_ref, str_ref, sti_ref, *, nb, tc):
    n = pl.program_id(1)

    @pl.when(n == 0)
    def _():
        str_ref[...] = s0r_ref[...]
        sti_ref[...] = s0i_ref[...]

    u = u_ref[...]
    ub = u.astype(BF16)
    xr_ref[...] = _dot(ub, b_ref[0, :, :S5_BS])
    xi_ref[...] = _dot(ub, b_ref[0, :, S5_BS:])

    sub = SUBLANES
    a_re = jnp.broadcast_to(are_ref[0], (sub, S5_BS))
    a_im = jnp.broadcast_to(aim_ref[0], (sub, S5_BS))
    for rb in range(nb // sub):
        def step(t, carry, rb=rb):
            x_re, x_im = carry
            off = pl.multiple_of(t * nb + rb * sub, sub)
            n_re = a_re * x_re - a_im * x_im + xr_ref[pl.ds(off, sub), :]
            n_im = a_re * x_im + a_im * x_re + xi_ref[pl.ds(off, sub), :]
            xr_ref[pl.ds(off, sub), :] = n_re
            xi_ref[pl.ds(off, sub), :] = n_im
            return n_re, n_im

        rows = slice(rb * sub, (rb + 1) * sub)
        x_re, x_im = lax.fori_loop(0, tc, step, (str_ref[rows, :], sti_ref[rows, :]))
        str_ref[rows, :] = x_re
        sti_ref[rows, :] = x_im

    y = _dot(xr_ref[...].astype(BF16), cre_ref[0]) - _dot(xi_ref[...].astype(BF16), cim_ref[0])
    y = y + d_ref[...] * u
    g_ref[...] = _gelu_tanh(y).astype(BF16)

    @pl.when(n == pl.num_programs(1) - 1)
    def _():
        slr_ref[...] = str_ref[...]
        sli_ref[...] = sti_ref[...]


def s5_scan(hn, nb, b_blk, c_re_blk, c_im_blk, ab_re, ab_im, d_skip, s0_re, s0_im):
    rows, d = hn.shape
    tc = TM // nb
    st = SDS((nb, S5_GROUPS * S5_STATE), F32)
    return pl.pallas_call(
        functools.partial(_s5_scan_kernel, nb=nb, tc=tc), grid=(S5_NGB, rows // TM),
        in_specs=[pl.BlockSpec((TM, S5_BC), lambda gb, n: (n, gb)),
                  pl.BlockSpec((1, S5_BC, 2 * S5_BS), lambda gb, n: (gb, 0, 0)),
                  pl.BlockSpec((1, S5_BS, S5_BC), lambda gb, n: (gb, 0, 0)),
                  pl.BlockSpec((1, S5_BS, S5_BC), lambda gb, n: (gb, 0, 0)),
                  pl.BlockSpec((1, 1, S5_BS), lambda gb, n: (gb, 0, 0)),
                  pl.BlockSpec((1, 1, S5_BS), lambda gb, n: (gb, 0, 0)),
                  pl.BlockSpec((1, S5_BC), lambda gb, n: (0, gb)),
                  pl.BlockSpec((nb, S5_BS), lambda gb, n: (0, gb)),
                  pl.BlockSpec((nb, S5_BS), lambda gb, n: (0, gb))],
        out_specs=[pl.BlockSpec((TM, S5_BC), lambda gb, n: (n, gb)),
                   pl.BlockSpec((nb, S5_BS), lambda gb, n: (0, gb)),
                   pl.BlockSpec((nb, S5_BS), lambda gb, n: (0, gb))],
        out_shape=(SDS((rows, d), BF16), st, st),
        scratch_shapes=[pltpu.VMEM((TM, S5_BS), F32), pltpu.VMEM((TM, S5_BS), F32),
                        pltpu.VMEM((nb, S5_BS), F32), pltpu.VMEM((nb, S5_BS), F32)],
        compiler_params=_cparams(2), name="s5_scan",
    )(hn, b_blk, c_re_blk, c_im_blk, ab_re, ab_im, d_skip.reshape(1, d), s0_re, s0_im)


def _glu_out_kernel(g_ref, w_ref, h_ref, o_ref):
    gv = _dot(g_ref[...], w_ref[...])
    o_ref[...] = h_ref[...] + gv[:, :D_MODEL] * _sigmoid(gv[:, D_MODEL:])


def glu_out(g, w_glu, h):
    rows, d = h.shape
    tm = TM // 2
    return pl.pallas_call(
        _glu_out_kernel, grid=(rows // tm,),
        in_specs=[pl.BlockSpec((tm, d), lambda i: (i, 0)),
                  pl.BlockSpec((d, 2 * d), lambda i: (0, 0)),
                  pl.BlockSpec((tm, d), lambda i: (i, 0))],
        out_specs=pl.BlockSpec((tm, d), lambda i: (i, 0)),
        out_shape=SDS((rows, d), F32), compiler_params=_cparams(1), name="glu_out",
    )(g, w_glu, h)


def s5_layer(h, nb, gain, s0_re, s0_im, a_re, a_im, log_dt, b_re, b_im, c_re, c_im, d_skip, w_glu):
    hn = rmsnorm_rows(h, gain)
    ab_re, ab_im, b_blk = s5_discretize(a_re, a_im, log_dt, b_re, b_im)
    g, sl_re, sl_im = s5_scan(hn, nb, b_blk, s5_block_diag_out(c_re), s5_block_diag_out(c_im),
                              ab_re, ab_im, d_skip, s0_re.reshape(nb, -1), s0_im.reshape(nb, -1))
    shape = (nb, S5_GROUPS, S5_STATE)
    return glu_out(g, w_glu.astype(BF16), h), sl_re.reshape(shape), sl_im.reshape(shape)


def _levels(c):
    return [2 ** i for i in range(1, int(math.log2(c)) + 1)]


def _level_masks(c):
    t = np.arange(c)[:, None]
    s = np.arange(c)[None, :]
    ms = [t == s]
    for sz in _levels(c):
        ms.append((t // sz == s // sz) & (t % sz >= sz // 2) & (s % sz < sz // 2))
    return jnp.asarray(np.stack(ms).astype(np.float32))


def _cumsum_pivot_matrix(c):
    t = np.arange(c)[:, None]
    r = np.arange(c)[None, :]
    ws = [r <= t]
    for sz in _levels(c):
        ws.append(r <= (t // sz) * sz + sz // 2 - 1)
    return jnp.asarray(np.concatenate(ws, 0).astype(np.float32))


def _gla_gate_kernel(gl_ref, w_ref, b_ref, o_ref):
    x = _dot(gl_ref[0].astype(BF16), w_ref[...]) + b_ref[...]
    o_ref[...] = -_softplus(-x) * (1.0 / GLA_TAU)


def gla_gate(gl_tiles, w_gate_pad, b_gate):
    rows = gl_tiles.shape[1]
    return pl.pallas_call(
        _gla_gate_kernel, grid=(rows // TM,),
        in_specs=[pl.BlockSpec((1, TM, LANES), lambda i: (0, i, 0)),
                  pl.BlockSpec((LANES, GLA_KEY), lambda i: (0, 0)),
                  pl.BlockSpec((1, GLA_KEY), lambda i: (0, 0))],
        out_specs=pl.BlockSpec((TM, GLA_KEY), lambda i: (i, 0)),
        out_shape=SDS((rows, GLA_KEY), F32), compiler_params=_cparams(1), name="gla_gate",
    )(gl_tiles, w_gate_pad, b_gate.reshape(1, GLA_KEY))


def _gla_chunk_kernel(q_ref, k_ref, v0_ref, v1_ref, g_ref, wst_ref, mask_ref, s0_ref, o_ref, sout_ref,
                      s_ref, *, c):
    n = pl.program_id(1)

    @pl.when(n == 0)
    def _():
        s_ref[...] = s0_ref[0]

    bp_all = _dot_exact01(wst_ref[...], g_ref[...])
    eye = _eye(GLA_DK)
    hv = GLA_TN // GLA_DV
    for hh in range(GLA_HEADS):
        ks = slice(hh * GLA_DK, (hh + 1) * GLA_DK)
        q = q_ref[0, :, ks] * (GLA_DK ** -0.5)
        k = k_ref[0, :, ks]
        v_ref = v0_ref if hh < hv else v1_ref
        vb = v_ref[0, :, (hh % hv) * GLA_DV:(hh % hv + 1) * GLA_DV].astype(BF16)
        bp = bp_all[:, ks]
        b = bp[0:c]
        att = mask_ref[0] * _dot_nt(q.astype(BF16), k.astype(BF16))
        for lv in range(1, mask_ref.shape[0]):
            e = jnp.exp(-jnp.abs(b - bp[lv * c:(lv + 1) * c]))
            att = att + mask_ref[lv] * _dot_nt((q * e).astype(BF16), (k * e).astype(BF16))
        s = s_ref[hh]
        o_ref[:, hh * GLA_DV:(hh + 1) * GLA_DV] = (
            _dot((q * jnp.exp(b)).astype(BF16), s.astype(BF16)) + _dot(att.astype(BF16), vb))
        b_last = b[c - 1:c, :]
        k_dec = (k * jnp.exp(b_last - b)).astype(BF16)
        dec = jnp.exp(_dot_nt_exact01(eye, jnp.broadcast_to(b_last, (GLA_DK, GLA_DK))))
        s_ref[hh] = jnp.concatenate([dec, dec], axis=1) * s + _dot_tn(k_dec, vb)

    @pl.when(n == pl.num_programs(1) - 1)
    def _():
        sout_ref[0] = s_ref[...]


def gla_chunks(p_tiles, g, s0, nb, seq, c):
    rows = p_tiles.shape[1]
    pv = p_tiles.reshape(p_tiles.shape[0], seq, nb * GLA_TN)
    gv = g.reshape(seq, nb * GLA_KEY)
    nl = len(_levels(c)) + 1
    tile = lambda t: pl.BlockSpec((1, c, GLA_TN), lambda b, n: (t, n, b))
    state = pl.BlockSpec((1, GLA_HEADS, GLA_DK, GLA_DV), lambda b, n: (b, 0, 0, 0))
    o, s_out = pl.pallas_call(
        functools.partial(_gla_chunk_kernel, c=c), grid=(nb, seq // c),
        in_specs=[tile(0), tile(1), tile(2), tile(3),
                  pl.BlockSpec((c, GLA_KEY), lambda b, n: (n, b)),
                  pl.BlockSpec((nl * c, c), lambda b, n: (0, 0)),
                  pl.BlockSpec((nl, c, c), lambda b, n: (0, 0, 0)),
                  state],
        out_specs=[pl.BlockSpec((c, GLA_VAL), lambda b, n: (n, b)), state],
        out_shape=(SDS((seq, nb * GLA_VAL), F32), SDS((nb, GLA_HEADS, GLA_DK, GLA_DV), F32)),
        scratch_shapes=[pltpu.VMEM((GLA_HEADS, GLA_DK, GLA_DV), F32)],
        compiler_params=_cparams(2), name="gla_chunks",
    )(pv, pv, pv, pv, gv, _cumsum_pivot_matrix(c), _level_masks(c), s0)
    return o.reshape(rows, GLA_VAL), s_out


def gla_layer(h, nb, seq, gain, s0, w_in, w_gate_up, b_gate, norm_o, w_out):
    c = CHUNK if seq % CHUNK == 0 else seq
    n_main = 2 * GLA_KEY + 2 * GLA_VAL
    p_tiles = norm_matmul(h, gain, w_in[:, :n_main].astype(BF16), GLA_TN)
    w_gl = jnp.pad(w_in[:, n_main:], ((0, 0), (0, LANES - GLA_RANK))).astype(BF16)
    gl_tiles = norm_matmul(h, gain, w_gl, LANES)
    w_gate_pad = jnp.pad(w_gate_up, ((0, LANES - GLA_RANK), (0, 0))).astype(BF16)
    g = gla_gate(gl_tiles, w_gate_pad, b_gate)
    o, s_out = gla_chunks(p_tiles, g, s0, nb, seq, c)
    h = gated_out(o, p_tiles, 4, norm_o, w_out.astype(BF16), h, GLA_HEADS, GLA_DV, TM)
    return h, s_out


def _gdn_conv_kernel(x_ref, cw_ref, c0_ref, o_ref, nc_ref, xp_ref, *, nb, tm, normalize):
    i = pl.program_id(1)
    halo = (GDN_CONV - 1) * nb

    @pl.when(i == 0)
    def _():
        xp_ref[0:halo, :] = c0_ref[...]

    xp_ref[halo:halo + tm, :] = x_ref[0]
    acc = cw_ref[0:1, :] * xp_ref[0:tm, :]
    for j in range(1, GDN_CONV):
        acc = acc + cw_ref[j:j + 1, :] * xp_ref[j * nb:j * nb + tm, :]
    y = acc * _sigmoid(acc)
    if normalize:
        scale = jnp.where(pl.program_id(0) == 0, GDN_DK ** -0.5, 1.0)
        for hh in range(y.shape[1] // GDN_DK):
            yh = y[:, hh * GDN_DK:(hh + 1) * GDN_DK]
            inv = lax.rsqrt(jnp.sum(yh * yh, axis=-1, keepdims=True) + NORM_EPS)
            o_ref[0, :, hh * GDN_DK:(hh + 1) * GDN_DK] = yh * (inv * scale)
    else:
        o_ref[0] = y
    tail = xp_ref[tm:tm + halo, :]
    xp_ref[0:halo, :] = tail

    @pl.when(i == pl.num_programs(1) - 1)
    def _():
        nc_ref[...] = tail


def gdn_conv(p_tiles, tile0, conv_w, conv0, nb, normalize):
    _, rows, tn = p_tiles.shape
    halo = (GDN_CONV - 1) * nb
    tm = TM
    assert tm >= halo and rows >= halo
    return pl.pallas_call(
        functools.partial(_gdn_conv_kernel, nb=nb, tm=tm, normalize=normalize), grid=(2, rows // tm),
        in_specs=[pl.BlockSpec((1, tm, tn), lambda j, i: (tile0 + j, i, 0)),
                  pl.BlockSpec((GDN_CONV, tn), lambda j, i: (0, tile0 + j)),
                  pl.BlockSpec((halo, tn), lambda j, i: (0, tile0 + j))],
        out_specs=[pl.BlockSpec((1, tm, tn), lambda j, i: (j, i, 0)),
                   pl.BlockSpec((halo, tn), lambda j, i: (0, j))],
        out_shape=(SDS((2, rows, tn), F32), SDS((halo, 2 * tn), F32)),
        scratch_shapes=[pltpu.VMEM((halo + tm, tn), F32)],
        compiler_params=_cparams(2), name="gdn_conv",
    )(p_tiles, conv_w, conv0)


def _gdn_chunk_kernel(q_ref, k_ref, v0_ref, v1_ref, ba_ref, alog_ref, dtb_ref, tri_ref, mask_ref, s0_ref,
                      o_ref, sout_ref, s_ref, *, c):
    n = pl.program_id(1)

    @pl.when(n == 0)
    def _():
        s_ref[...] = s0_ref[0]

    ba = ba_ref[...]
    beta_all = _sigmoid(ba)
    g_all = -jnp.exp(alog_ref[...]) * _softplus(ba + dtb_ref[...])
    tri = tri_ref[...]
    gc_all = _dot_exact01(tri, g_all)
    g_t = _dot_nt_exact01(_eye(LANES), g_all)
    gr_all = _dot_nt_exact01_rhs(g_t, tri)
    eye_c = mask_ref[0]
    strict = tri - eye_c
    nlv = mask_ref.shape[0]
    rep = GDN_V_HEADS // GDN_QK_HEADS
    hpt = GDN_TN // GDN_DV

    heads = []
    for qh in range(GDN_QK_HEADS):
        ks = slice(qh * GDN_DK, (qh + 1) * GDN_DK)
        q = q_ref[0, :, ks]
        k = k_ref[0, :, ks]
        kb = k.astype(BF16)
        kk = _dot_nt(kb, kb)
        qk = _dot_nt(q.astype(BF16), kb)
        for j in range(rep):
            hh = qh * rep + j
            beta = beta_all[:, hh:hh + 1]
            g_col = gc_all[:, GDN_V_HEADS + hh:GDN_V_HEADS + hh + 1]
            g_row = gr_all[GDN_V_HEADS + hh:GDN_V_HEADS + hh + 1, :]
            decay = tri * jnp.exp(jnp.minimum(g_col - g_row, 0.0))
            m = strict * (kk * decay * beta)
            heads.append(dict(hh=hh, q=q, k=k, qk=qk, beta=beta, g_col=g_col, decay=decay, m=m,
                              t=eye_c - mask_ref[1] * m))
    for lv in range(2, nlv):
        for hd in heads:
            mo = (mask_ref[lv] * hd["m"]).astype(BF16)
            tb = hd["t"].astype(BF16)
            hd["t"] = hd["t"] - _dot(_dot(tb, mo).astype(BF16), tb)
    for hd in heads:
        hh, k, beta, g_col = hd["hh"], hd["k"], hd["beta"], hd["g_col"]
        v_ref = v0_ref if hh < hpt else v1_ref
        v = v_ref[0, :, (hh % hpt) * GDN_DV:(hh % hpt + 1) * GDN_DV]
        e_g = jnp.exp(g_col)
        rhs = jnp.concatenate([v * beta, k * (beta * e_g)], axis=1).astype(BF16)
        uw = _dot(hd["t"].astype(BF16), rhs)
        s = s_ref[hh]
        sb = s.astype(BF16)
        v_new = uw[:, :GDN_DV] - _dot(uw[:, GDN_DV:].astype(BF16), sb)
        vnb = v_new.astype(BF16)
        q_dec = (hd["q"] * e_g).astype(BF16)
        o_ref[:, hh * GDN_DV:(hh + 1) * GDN_DV] = (
            _dot(q_dec, sb) + _dot((hd["qk"] * hd["decay"]).astype(BF16), vnb))
        g_last = g_col[c - 1:c, :]
        k_dec = (k * jnp.exp(g_last - g_col)).astype(BF16)
        s_ref[hh] = jnp.exp(g_last) * s + _dot_tn(k_dec, vnb)

    @pl.when(n == pl.num_programs(1) - 1)
    def _():
        sout_ref[0] = s_ref[...]


def gdn_chunks(qk_tiles, v_tiles, ba, a_log, dt_bias, s0, nb, seq, c):
    rows = qk_tiles.shape[1]
    qkv_ = qk_tiles.reshape(2, seq, nb * GDN_TN)
    vv = v_tiles.reshape(2, seq, nb * GDN_TN)
    bav = ba.reshape(seq, nb * LANES)
    pad = lambda x: jnp.pad(x, (GDN_V_HEADS, LANES - 2 * GDN_V_HEADS)).reshape(1, LANES)
    masks = _level_masks(c)
    tri = jnp.asarray(np.tril(np.ones((c, c), np.float32)))
    nl = masks.shape[0]
    tile = lambda t: pl.BlockSpec((1, c, GDN_TN), lambda b, n: (t, n, b))
    const = lambda shape: pl.BlockSpec(shape, lambda b, n: (0,) * len(shape))
    state = pl.BlockSpec((1, GDN_V_HEADS, GDN_DK, GDN_DV), lambda b, n: (b, 0, 0, 0))
    o, s_out = pl.pallas_call(
        functools.partial(_gdn_chunk_kernel, c=c), grid=(nb, seq // c),
        in_specs=[tile(0), tile(1), tile(0), tile(1),
                  pl.BlockSpec((c, LANES), lambda b, n: (n, b)),
                  const((1, LANES)), const((1, LANES)), const((c, c)), const((nl, c, c)), state],
        out_specs=[pl.BlockSpec((c, GDN_VAL), lambda b, n: (n, b)), state],
        out_shape=(SDS((seq, nb * GDN_VAL), F32), SDS((nb, GDN_V_HEADS, GDN_DK, GDN_DV), F32)),
        scratch_shapes=[pltpu.VMEM((GDN_V_HEADS, GDN_DK, GDN_DV), F32)],
        compiler_params=_cparams(2), name="gdn_chunks",
    )(qkv_, qkv_, vv, vv, bav, pad(a_log), pad(dt_bias), tri, masks, s0)
    return o.reshape(rows, GDN_VAL), s_out


def gdn_layer(h, nb, seq, gain, s0, conv0, w_in, conv_w, a_log, dt_bias, norm_o, w_out):
    c = CHUNK if seq % CHUNK == 0 else seq
    n_main = GDN_CONV_DIM + GDN_VAL
    p_tiles = norm_matmul(h, gain, w_in[:, :n_main].astype(BF16), GDN_TN)
    w_ba = jnp.pad(w_in[:, n_main:], ((0, 0), (0, LANES - 2 * GDN_V_HEADS))).astype(BF16)
    ba = norm_matmul(h, gain, w_ba, LANES)[0]
    conv0_tm = conv0.transpose(1, 0, 2).reshape((GDN_CONV - 1) * nb, GDN_CONV_DIM)
    qk_tiles, nc_qk = gdn_conv(p_tiles, 0, conv_w, conv0_tm, nb, True)
    v_tiles, nc_v = gdn_conv(p_tiles, 2, conv_w, conv0_tm, nb, False)
    o, s_out = gdn_chunks(qk_tiles, v_tiles, ba, a_log, dt_bias, s0, nb, seq, c)
    h = gated_out(o, p_tiles, 4, norm_o, w_out.astype(BF16), h, GDN_V_HEADS, GDN_DV, TM // 2)
    new_conv = jnp.concatenate([nc_qk, nc_v], axis=1).reshape(GDN_CONV - 1, nb, GDN_CONV_DIM)
    return h, s_out, new_conv.transpose(1, 0, 2)


def _trunk(x, s5_re, s5_im, gla_s, gdn_s, gdn_conv_s, w):
    nb, seq, d = x.shape
    h = x.transpose(1, 0, 2).reshape(seq * nb, d)
    h, s5r0, s5i0 = s5_layer(h, nb, w["norm_mix"][0], s5_re[0], s5_im[0], w["s5_a_re"][0], w["s5_a_im"][0],
                             w["s5_log_dt"][0], w["s5_b_re"][0], w["s5_b_im"][0], w["s5_c_re"][0],
                             w["s5_c_im"][0], w["s5_d"][0], w["s5_w_glu"][0])
    h = ffn(h, w["norm_ffn"][0], w["w_up"][0], w["w_down"][0], w["norm_final"], False)
    h, gla_o = gla_layer(h, nb, seq, w["norm_mix"][1], gla_s[0], w["gla_w_in"][0], w["gla_w_gate_up"][0],
                         w["gla_b_gate"][0], w["gla_norm"][0], w["gla_w_out"][0])
    h = ffn(h, w["norm_ffn"][1], w["w_up"][1], w["w_down"][1], w["norm_final"], False)
    h, gdn_o, conv_o = gdn_layer(h, nb, seq, w["norm_mix"][2], gdn_s[0], gdn_conv_s[0], w["gdn_w_in"][0],
                                 w["gdn_conv_w"][0], w["gdn_a_log"][0], w["gdn_dt_bias"][0],
                                 w["gdn_norm"][0], w["gdn_w_out"][0])
    h = ffn(h, w["norm_ffn"][2], w["w_up"][2], w["w_down"][2], w["norm_final"], False)
    h, s5r1, s5i1 = s5_layer(h, nb, w["norm_mix"][3], s5_re[1], s5_im[1], w["s5_a_re"][1], w["s5_a_im"][1],
                             w["s5_log_dt"][1], w["s5_b_re"][1], w["s5_b_im"][1], w["s5_c_re"][1],
                             w["s5_c_im"][1], w["s5_d"][1], w["s5_w_glu"][1])
    y = ffn(h, w["norm_ffn"][3], w["w_up"][3], w["w_down"][3], w["norm_final"], True)
    y = y.reshape(seq, nb, d).transpose(1, 0, 2)
    return (y, jnp.stack([s5r0, s5r1]), jnp.stack([s5i0, s5i1]), gla_o[None], gdn_o[None], conv_o[None])


def kernel(x_prompt, x_sample, state_s5_re, state_s5_im, state_gla, state_gdn, state_gdn_conv, norm_mix, norm_ffn, norm_final, w_up, w_down, s5_a_re, s5_a_im, s5_log_dt, s5_b_re, s5_b_im, s5_c_re, s5_c_im, s5_d, s5_w_glu, gla_w_in, gla_w_gate_up, gla_b_gate, gla_norm, gla_w_out, gdn_w_in, gdn_conv_w, gdn_a_log, gdn_dt_bias, gdn_norm, gdn_w_out):
    w = dict(norm_mix=norm_mix, norm_ffn=norm_ffn, norm_final=norm_final,
             w_up=w_up.astype(BF16), w_down=w_down.astype(BF16),
             s5_a_re=s5_a_re, s5_a_im=s5_a_im, s5_log_dt=s5_log_dt, s5_b_re=s5_b_re, s5_b_im=s5_b_im,
             s5_c_re=s5_c_re, s5_c_im=s5_c_im, s5_d=s5_d, s5_w_glu=s5_w_glu,
             gla_w_in=gla_w_in, gla_w_gate_up=gla_w_gate_up, gla_b_gate=gla_b_gate, gla_norm=gla_norm,
             gla_w_out=gla_w_out, gdn_w_in=gdn_w_in, gdn_conv_w=gdn_conv_w, gdn_a_log=gdn_a_log,
             gdn_dt_bias=gdn_dt_bias, gdn_norm=gdn_norm, gdn_w_out=gdn_w_out)
    bp = x_prompt.shape[0]
    dt = x_prompt.dtype
    z_s5 = jnp.zeros((state_s5_re.shape[0], bp) + state_s5_re.shape[2:], dt)
    z_gla = jnp.zeros((state_gla.shape[0], bp) + state_gla.shape[2:], dt)
    z_gdn = jnp.zeros((state_gdn.shape[0], bp) + state_gdn.shape[2:], dt)
    z_conv = jnp.zeros((state_gdn_conv.shape[0], bp) + state_gdn_conv.shape[2:], dt)
    out_p = _trunk(x_prompt, z_s5, z_s5, z_gla, z_gdn, z_conv, w)
    out_s = _trunk(x_sample, state_s5_re, state_s5_im, state_gla, state_gdn, state_gdn_conv, w)
    return (out_p[0], out_s[0]) + out_p[1:] + out_s[1:]
```

```python
import functools
import math

import numpy as np
import jax
import jax.numpy as jnp
from jax import lax
from jax.experimental import pallas as pl
from jax.experimental.pallas import tpu as pltpu

F32 = jnp.float32
BF16 = jnp.bfloat16
SDS = jax.ShapeDtypeStruct

D_MODEL = 1024
NORM_EPS = 1e-6

S5_GROUP = 16
S5_STATE = 64
S5_GROUPS = D_MODEL // S5_GROUP
S5_GB = 16
S5_NGB = S5_GROUPS // S5_GB
S5_BC = S5_GB * S5_GROUP
S5_BS = S5_GB * S5_STATE

GLA_HEADS = 4
GLA_DK = 128
GLA_DV = 256
GLA_KEY = GLA_HEADS * GLA_DK
GLA_VAL = GLA_HEADS * GLA_DV
GLA_RANK = 16
GLA_TAU = 16.0
GLA_TN = 512

GDN_DK = 128
GDN_DV = 128
GDN_QK_HEADS = 8
GDN_V_HEADS = 16
GDN_KEY = GDN_QK_HEADS * GDN_DK
GDN_VAL = GDN_V_HEADS * GDN_DV
GDN_HALF = GDN_V_HEADS // 2
GDN_CONV = 4
GDN_CONV_DIM = 2 * GDN_KEY + GDN_VAL
GDN_TN = 1024

CHUNK = 128
LANES = 128
SUBLANES = 8
TM = 1024
FFN_TF = 512
MIB = 1024 * 1024


def _cparams(n_axes, vmem_mib=48):
    return pltpu.CompilerParams(dimension_semantics=("arbitrary",) * n_axes,
                                vmem_limit_bytes=vmem_mib * MIB)


def _rms(x, gain):
    ms = jnp.mean(x * x, axis=-1, keepdims=True)
    return x * lax.rsqrt(ms + NORM_EPS) * gain


def _sigmoid(x):
    return 1.0 / (1.0 + jnp.exp(-x))


def _softplus(x):
    return jnp.maximum(x, 0.0) + jnp.log1p(jnp.exp(-jnp.abs(x)))


def _gelu_tanh(x):
    c = math.sqrt(2.0 / math.pi)
    return x * (0.5 * (1.0 + jnp.tanh(c * (x + 0.044715 * (x * x * x)))))


def _dot(a, b):
    return jnp.dot(a, b, preferred_element_type=F32)


def _dot_nt(a, b):
    return lax.dot_general(a, b, (((1,), (1,)), ((), ())), preferred_element_type=F32)


def _dot_tn(a, b):
    return lax.dot_general(a, b, (((0,), (0,)), ((), ())), preferred_element_type=F32)


def _eye(n):
    return (lax.broadcasted_iota(jnp.int32, (n, n), 0)
            == lax.broadcasted_iota(jnp.int32, (n, n), 1)).astype(F32)


def _split3(x):
    x1 = x.astype(BF16)
    r1 = x - x1.astype(F32)
    x2 = r1.astype(BF16)
    x3 = (r1 - x2.astype(F32)).astype(BF16)
    return x1, x2, x3


def _dot_exact01(m01, x):
    mb = m01.astype(BF16)
    x1, x2, x3 = _split3(x)
    return _dot(mb, x1) + _dot(mb, x2) + _dot(mb, x3)


def _dot_nt_exact01(m01, x):
    mb = m01.astype(BF16)
    x1, x2, x3 = _split3(x)
    return _dot_nt(mb, x1) + _dot_nt(mb, x2) + _dot_nt(mb, x3)


def _rmsnorm_kernel(h_ref, g_ref, o_ref):
    o_ref[...] = _rms(h_ref[...], g_ref[...])


def rmsnorm_rows(h, gain):
    rows, d = h.shape
    return pl.pallas_call(
        _rmsnorm_kernel, grid=(rows // TM,),
        in_specs=[pl.BlockSpec((TM, d), lambda i: (i, 0)), pl.BlockSpec((1, d), lambda i: (0, 0))],
        out_specs=pl.BlockSpec((TM, d), lambda i: (i, 0)),
        out_shape=SDS((rows, d), F32), compiler_params=_cparams(1), name="rmsnorm",
    )(h, gain.reshape(1, d))


def _norm_in_kernel(x_ref, g_ref, h_ref, hn_ref):
    for j in range(SUBLANES):
        x = x_ref[j]
        h_ref[:, j, :] = x
        hn_ref[:, j, :] = _rms(x, g_ref[...])


def norm_in(x, gain):
    nb, seq, d = x.shape
    tt = min(seq, TM // SUBLANES)
    nblk = seq // tt
    out = SDS((nb // SUBLANES * seq, SUBLANES, d), F32)
    h, hn = pl.pallas_call(
        _norm_in_kernel, grid=(nb // SUBLANES, nblk),
        in_specs=[pl.BlockSpec((SUBLANES, tt, d), lambda g, i: (g, i, 0)),
                  pl.BlockSpec((1, d), lambda g, i: (0, 0))],
        out_specs=[pl.BlockSpec((tt, SUBLANES, d), lambda g, i: (g * nblk + i, 0, 0))] * 2,
        out_shape=(out, out), compiler_params=_cparams(2), name="norm_in",
    )(x, gain.reshape(1, d))
    return h.reshape(seq * nb, d), hn.reshape(seq * nb, d)


def _norm_out_kernel(h_ref, g_ref, y_ref):
    for j in range(SUBLANES):
        y_ref[j] = _rms(h_ref[:, j, :], g_ref[...])


def norm_out(h, gain, nb, seq):
    d = h.shape[1]
    tt = min(seq, TM // SUBLANES)
    nblk = seq // tt
    return pl.pallas_call(
        _norm_out_kernel, grid=(nb // SUBLANES, nblk),
        in_specs=[pl.BlockSpec((tt, SUBLANES, d), lambda g, i: (g * nblk + i, 0, 0)),
                  pl.BlockSpec((1, d), lambda g, i: (0, 0))],
        out_specs=pl.BlockSpec((SUBLANES, tt, d), lambda g, i: (g, i, 0)),
        out_shape=SDS((nb, seq, d), F32), compiler_params=_cparams(2), name="norm_out",
    )(h.reshape(nb // SUBLANES * seq, SUBLANES, d), gain.reshape(1, d))


def _norm_matmul_kernel(h_ref, g_ref, w_ref, o_ref, hn_ref):
    @pl.when(pl.program_id(1) == 0)
    def _():
        hn_ref[...] = _rms(h_ref[...], g_ref[...]).astype(BF16)

    res = _dot(hn_ref[...], w_ref[...])
    for t in range(o_ref.shape[0]):
        o_ref[t] = res[:, t * LANES:(t + 1) * LANES]


def norm_matmul(h, gain, w, tn):
    rows, d = h.shape
    n = w.shape[1]
    return pl.pallas_call(
        _norm_matmul_kernel, grid=(rows // TM, n // tn),
        in_specs=[pl.BlockSpec((TM, d), lambda i, j: (i, 0)),
                  pl.BlockSpec((1, d), lambda i, j: (0, 0)),
                  pl.BlockSpec((d, tn), lambda i, j: (0, j))],
        out_specs=pl.BlockSpec((tn // LANES, TM, LANES), lambda i, j: (j, i, 0)),
        out_shape=SDS((n // LANES, rows, LANES), F32),
        scratch_shapes=[pltpu.VMEM((TM, d), BF16)],
        compiler_params=_cparams(2), name="norm_matmul",
    )(h, gain.reshape(1, d), w)


def _ffn_kernel(h_ref, g_ref, wu_ref, wd_ref, o_ref, hn_ref, acc_ref):
    j = pl.program_id(1)

    @pl.when(j == 0)
    def _():
        hn_ref[...] = _rms(h_ref[...], g_ref[...]).astype(BF16)
        acc_ref[...] = jnp.zeros_like(acc_ref)

    a = jnp.square(jnp.maximum(_dot(hn_ref[...], wu_ref[...]), 0.0)).astype(BF16)
    acc_ref[...] += _dot(a, wd_ref[...])

    @pl.when(j == pl.num_programs(1) - 1)
    def _():
        o_ref[...] = h_ref[...] + acc_ref[...]


def ffn(h, gain, w_up, w_down):
    rows, d = h.shape
    f = w_up.shape[1]
    return pl.pallas_call(
        _ffn_kernel, grid=(rows // TM, f // FFN_TF),
        in_specs=[pl.BlockSpec((TM, d), lambda i, j: (i, 0)),
                  pl.BlockSpec((1, d), lambda i, j: (0, 0)),
                  pl.BlockSpec((d, FFN_TF), lambda i, j: (0, j)),
                  pl.BlockSpec((FFN_TF, d), lambda i, j: (j, 0))],
        out_specs=pl.BlockSpec((TM, d), lambda i, j: (i, 0)),
        out_shape=SDS((rows, d), F32),
        scratch_shapes=[pltpu.VMEM((TM, d), BF16), pltpu.VMEM((TM, d), F32)],
        compiler_params=_cparams(2), name="ffn",
    )(h, gain.reshape(1, d), w_up, w_down)


def _gated_out_kernel(o_ref, z_ref, gn_ref, w_ref, h_ref, out_ref, a_ref, *, nheads, hd):
    tph = hd // LANES
    for hh in range(nheads):
        tiles = range(hh * tph, (hh + 1) * tph)
        o = jnp.concatenate([o_ref[t] for t in tiles], axis=1)
        z = jnp.concatenate([z_ref[t] for t in tiles], axis=1)
        a_ref[:, hh * hd:(hh + 1) * hd] = (_rms(o, gn_ref[...]) * (z * _sigmoid(z))).astype(BF16)
    out_ref[...] = h_ref[...] + _dot(a_ref[...], w_ref[...])


def gated_out(o_tiles, p_tiles, z_block, gain, w_out, h, nheads, hd, tm):
    rows, d = h.shape
    kdim = nheads * hd
    nt = kdim // LANES
    return pl.pallas_call(
        functools.partial(_gated_out_kernel, nheads=nheads, hd=hd), grid=(rows // tm,),
        in_specs=[pl.BlockSpec((nt, tm, LANES), lambda i: (0, i, 0)),
                  pl.BlockSpec((nt, tm, LANES), lambda i: (z_block, i, 0)),
                  pl.BlockSpec((1, hd), lambda i: (0, 0)),
                  pl.BlockSpec((kdim, d), lambda i: (0, 0)),
                  pl.BlockSpec((tm, d), lambda i: (i, 0))],
        out_specs=pl.BlockSpec((tm, d), lambda i: (i, 0)),
        out_shape=SDS((rows, d), F32),
        scratch_shapes=[pltpu.VMEM((tm, kdim), BF16)],
        compiler_params=_cparams(1), name="gated_out",
    )(o_tiles, p_tiles, gain.reshape(1, hd), w_out, h)


def _s5_discretize_kernel(are_ref, aim_ref, ldt_ref, bre_ref, bim_ref,
                          abr_ref, abi_ref, bbr_ref, bbi_ref):
    a_re, a_im = are_ref[...], aim_ref[...]
    dt = jnp.exp(ldt_ref[...])
    mag = jnp.exp(a_re * dt)
    ab_re = mag * jnp.cos(a_im * dt)
    ab_im = mag * jnp.sin(a_im * dt)
    den = a_re * a_re + a_im * a_im
    c_re = ((ab_re - 1.0) * a_re + ab_im * a_im) / den
    c_im = (ab_im * a_re - (ab_re - 1.0) * a_im) / den
    abr_ref[...] = ab_re
    abi_ref[...] = ab_im
    bbr_ref[...] = c_re * bre_ref[...] - c_im * bim_ref[...]
    bbi_ref[...] = c_re * bim_ref[...] + c_im * bre_ref[...]


def s5_discretize(a_re, a_im, log_dt, b_re, b_im):
    g, p, c = S5_GROUPS, S5_STATE, S5_GROUP
    expand = lambda v: jnp.broadcast_to(v[..., None], (g, p, c)).reshape(g, p * c)
    ldt = jnp.broadcast_to(log_dt[:, None], (g, p * c))
    shp = SDS((g, p * c), F32)
    ab_re, ab_im, bb_re, bb_im = pl.pallas_call(
        _s5_discretize_kernel, out_shape=(shp, shp, shp, shp), name="s5_discretize",
    )(expand(a_re), expand(a_im), ldt, b_re.reshape(g, p * c), b_im.reshape(g, p * c))
    ab_re = ab_re.reshape(g, p, c)[:, :, 0].reshape(S5_NGB, 1, S5_BS)
    ab_im = ab_im.reshape(g, p, c)[:, :, 0].reshape(S5_NGB, 1, S5_BS)
    eye = jnp.eye(S5_GB, dtype=F32)

    def block_diag_in(bb):
        bb = bb.reshape(S5_NGB, S5_GB, p, c)
        return jnp.einsum("bgpc,gh->bgchp", bb, eye).reshape(S5_NGB, S5_BC, S5_BS)

    b_blk = jnp.concatenate([block_diag_in(bb_re.reshape(g, p, c)),
                             block_diag_in(bb_im.reshape(g, p, c))], axis=-1).astype(BF16)
    return ab_re, ab_im, b_blk


def s5_block_diag_out(c_par):
    eye = jnp.eye(S5_GB, dtype=F32)
    cc = c_par.reshape(S5_NGB, S5_GB, S5_GROUP, S5_STATE)
    return jnp.einsum("bgcp,gh->bgphc", cc, eye).reshape(S5_NGB, S5_BS, S5_BC).astype(BF16)


def _s5_scan_kernel(u_ref, b_ref, cre_ref, cim_ref, are_ref, aim_ref, d_ref, s0r_ref, s0i_ref,
                    g_ref, slr_ref, sli_ref, xr_ref, xi_ref, str_ref, sti_ref, *, ngrp, tc):
    n = pl.program_id(1)

    @pl.when(n == 0)
    def _():
        str_ref[...] = s0r_ref[...]
        sti_ref[...] = s0i_ref[...]

    u = u_ref[...]
    ub = u.astype(BF16)
    xr_ref[...] = _dot(ub, b_ref[0, :, :S5_BS])
    xi_ref[...] = _dot(ub, b_ref[0, :, S5_BS:])

    sub = SUBLANES
    a_re = jnp.broadcast_to(are_ref[0], (sub, S5_BS))
    a_im = jnp.broadcast_to(aim_ref[0], (sub, S5_BS))
    for rb in range(ngrp):
        def step(t, carry, rb=rb):
            x_re, x_im = carry
            off = pl.multiple_of((rb * tc + t) * sub, sub)
            n_re = a_re * x_re - a_im * x_im + xr_ref[pl.ds(off, sub), :]
            n_im = a_re * x_im + a_im * x_re + xi_ref[pl.ds(off, sub), :]
            xr_ref[pl.ds(off, sub), :] = n_re
            xi_ref[pl.ds(off, sub), :] = n_im
            return n_re, n_im

        rows = slice(rb * sub, (rb + 1) * sub)
        x_re, x_im = lax.fori_loop(0, tc, step, (str_ref[rows, :], sti_ref[rows, :]))
        str_ref[rows, :] = x_re
        sti_ref[rows, :] = x_im

    y = _dot(xr_ref[...].astype(BF16), cre_ref[0]) - _dot(xi_ref[...].astype(BF16), cim_ref[0])
    y = y + d_ref[...] * u
    g_ref[...] = _gelu_tanh(y).astype(BF16)

    @pl.when(n == pl.num_programs(1) - 1)
    def _():
        slr_ref[...] = str_ref[...]
        sli_ref[...] = sti_ref[...]


def s5_scan(hn, nb, b_blk, c_re_blk, c_im_blk, ab_re, ab_im, d_skip, s0_re, s0_im):
    rows, d = hn.shape
    tc = min(rows // nb, TM // SUBLANES)
    ngrp = TM // (tc * SUBLANES)
    assert ngrp == 1 or ngrp * SUBLANES == nb
    st = SDS((nb, S5_GROUPS * S5_STATE), F32)
    return pl.pallas_call(
        functools.partial(_s5_scan_kernel, ngrp=ngrp, tc=tc), grid=(S5_NGB, rows // TM),
        in_specs=[pl.BlockSpec((TM, S5_BC), lambda gb, n: (n, gb)),
                  pl.BlockSpec((1, S5_BC, 2 * S5_BS), lambda gb, n: (gb, 0, 0)),
                  pl.BlockSpec((1, S5_BS, S5_BC), lambda gb, n: (gb, 0, 0)),
                  pl.BlockSpec((1, S5_BS, S5_BC), lambda gb, n: (gb, 0, 0)),
                  pl.BlockSpec((1, 1, S5_BS), lambda gb, n: (gb, 0, 0)),
                  pl.BlockSpec((1, 1, S5_BS), lambda gb, n: (gb, 0, 0)),
                  pl.BlockSpec((1, S5_BC), lambda gb, n: (0, gb)),
                  pl.BlockSpec((nb, S5_BS), lambda gb, n: (0, gb)),
                  pl.BlockSpec((nb, S5_BS), lambda gb, n: (0, gb))],
        out_specs=[pl.BlockSpec((TM, S5_BC), lambda gb, n: (n, gb)),
                   pl.BlockSpec((nb, S5_BS), lambda gb, n: (0, gb)),
                   pl.BlockSpec((nb, S5_BS), lambda gb, n: (0, gb))],
        out_shape=(SDS((rows, d), BF16), st, st),
        scratch_shapes=[pltpu.VMEM((TM, S5_BS), F32), pltpu.VMEM((TM, S5_BS), F32),
                        pltpu.VMEM((nb, S5_BS), F32), pltpu.VMEM((nb, S5_BS), F32)],
        compiler_params=_cparams(2), name="s5_scan",
    )(hn, b_blk, c_re_blk, c_im_blk, ab_re, ab_im, d_skip.reshape(1, d), s0_re, s0_im)


def _glu_out_kernel(g_ref, w_ref, h_ref, o_ref):
    gv = _dot(g_ref[...], w_ref[...])
    o_ref[...] = h_ref[...] + gv[:, :D_MODEL] * _sigmoid(gv[:, D_MODEL:])


def glu_out(g, w_glu, h):
    rows, d = h.shape
    tm = TM // 2
    return pl.pallas_call(
        _glu_out_kernel, grid=(rows // tm,),
        in_specs=[pl.BlockSpec((tm, d), lambda i: (i, 0)),
                  pl.BlockSpec((d, 2 * d), lambda i: (0, 0)),
                  pl.BlockSpec((tm, d), lambda i: (i, 0))],
        out_specs=pl.BlockSpec((tm, d), lambda i: (i, 0)),
        out_shape=SDS((rows, d), F32), compiler_params=_cparams(1), name="glu_out",
    )(g, w_glu, h)


def s5_layer(h, hn, nb, s0_re, s0_im, a_re, a_im, log_dt, b_re, b_im, c_re, c_im, d_skip, w_glu):
    ab_re, ab_im, b_blk = s5_discretize(a_re, a_im, log_dt, b_re, b_im)
    g, sl_re, sl_im = s5_scan(hn, nb, b_blk, s5_block_diag_out(c_re), s5_block_diag_out(c_im),
                              ab_re, ab_im, d_skip, s0_re.reshape(nb, -1), s0_im.reshape(nb, -1))
    shape = (nb, S5_GROUPS, S5_STATE)
    return glu_out(g, w_glu.astype(BF16), h), sl_re.reshape(shape), sl_im.reshape(shape)


def _levels(c):
    return [2 ** i for i in range(1, int(math.log2(c)) + 1)]


def _level_masks(c):
    t = np.arange(c)[:, None]
    s = np.arange(c)[None, :]
    ms = [t == s]
    for sz in _levels(c):
        ms.append((t // sz == s // sz) & (t % sz >= sz // 2) & (s % sz < sz // 2))
    return jnp.asarray(np.stack(ms).astype(np.float32))


def _cumsum_pivot_matrix(c):
    t = np.arange(c)[:, None]
    r = np.arange(c)[None, :]
    ws = [r <= t]
    for sz in _levels(c):
        ws.append(r <= (t // sz) * sz + sz // 2 - 1)
    return jnp.asarray(np.concatenate(ws, 0).astype(np.float32))


def _gla_gate_kernel(gl_ref, w_ref, b_ref, o_ref):
    x = _dot(gl_ref[0].astype(BF16), w_ref[...]) + b_ref[...]
    g = -_softplus(-x) * (1.0 / GLA_TAU)
    for t in range(o_ref.shape[0]):
        o_ref[t] = g[:, t * LANES:(t + 1) * LANES]


def gla_gate(gl_tiles, w_gate_pad, b_gate):
    rows = gl_tiles.shape[1]
    nt = GLA_KEY // LANES
    return pl.pallas_call(
        _gla_gate_kernel, grid=(rows // TM,),
        in_specs=[pl.BlockSpec((1, TM, LANES), lambda i: (0, i, 0)),
                  pl.BlockSpec((LANES, GLA_KEY), lambda i: (0, 0)),
                  pl.BlockSpec((1, GLA_KEY), lambda i: (0, 0))],
        out_specs=pl.BlockSpec((nt, TM, LANES), lambda i: (0, i, 0)),
        out_shape=SDS((nt, rows, LANES), F32), compiler_params=_cparams(1), name="gla_gate",
    )(gl_tiles, w_gate_pad, b_gate.reshape(1, GLA_KEY))


def _gla_chunk_kernel(q_ref, k_ref, v_ref, g_ref, wst_ref, mask_ref, s0_ref, o_ref, s_ref, *, c):
    @pl.when(pl.program_id(1) == 0)
    def _():
        s_ref[...] = s0_ref[...]

    wst = wst_ref[...]
    eye = _eye(GLA_DK)
    tpv = GLA_DV // LANES

    def per_sequence(j, carry):
        rows = pl.ds(j, c, stride=SUBLANES)
        loaded = [(q_ref[hh, rows, :], k_ref[hh, rows, :],
                   [v_ref[hh * tpv + t, rows, :] for t in range(tpv)],
                   g_ref[hh, rows, :], s_ref[j, hh]) for hh in range(GLA_HEADS)]
        results = []
        for q, k, v, g, s in loaded:
            q = q * (GLA_DK ** -0.5)
            vb = jnp.concatenate(v, axis=1).astype(BF16)
            bp = _dot_exact01(wst, g)
            b = bp[0:c]
            att = mask_ref[0] * _dot_nt(q.astype(BF16), k.astype(BF16))
            for lv in range(1, mask_ref.shape[0]):
                e = jnp.exp(-jnp.abs(b - bp[lv * c:(lv + 1) * c]))
                att = att + mask_ref[lv] * _dot_nt((q * e).astype(BF16), (k * e).astype(BF16))
            o = _dot((q * jnp.exp(b)).astype(BF16), s.astype(BF16)) + _dot(att.astype(BF16), vb)
            b_last = b[c - 1:c, :]
            k_dec = (k * jnp.exp(b_last - b)).astype(BF16)
            dec = jnp.exp(_dot_nt_exact01(eye, jnp.broadcast_to(b_last, (GLA_DK, GLA_DK))))
            results.append((o, jnp.concatenate([dec] * tpv, axis=1) * s + _dot_tn(k_dec, vb)))
        for hh, (o, s_new) in enumerate(results):
            for t in range(tpv):
                o_ref[hh * tpv + t, rows, :] = o[:, t * LANES:(t + 1) * LANES]
            s_ref[j, hh] = s_new
        return carry

    lax.fori_loop(0, SUBLANES, per_sequence, 0)


def gla_chunks(p_tiles, g_tiles, s0, nb, seq, c):
    rows = p_tiles.shape[1]
    nl = len(_levels(c)) + 1
    nkt = GLA_KEY // LANES
    nvt = GLA_VAL // LANES
    nc = seq // c
    tiles = lambda nt, blk: pl.BlockSpec((nt, c * SUBLANES, LANES), lambda b, n: (blk, b * nc + n, 0))
    state = pl.BlockSpec((SUBLANES, GLA_HEADS, GLA_DK, GLA_DV), lambda b, n: (b, 0, 0, 0))
    o, s_out = pl.pallas_call(
        functools.partial(_gla_chunk_kernel, c=c), grid=(nb // SUBLANES, seq // c),
        in_specs=[tiles(nkt, 0), tiles(nkt, 1), tiles(nvt, 1), tiles(nkt, 0),
                  pl.BlockSpec((nl * c, c), lambda b, n: (0, 0)),
                  pl.BlockSpec((nl, c, c), lambda b, n: (0, 0, 0)),
                  state],
        out_specs=[tiles(nvt, 0), state],
        out_shape=(SDS((nvt, rows, LANES), F32), SDS((nb, GLA_HEADS, GLA_DK, GLA_DV), F32)),
        compiler_params=_cparams(2), name="gla_chunks",
    )(p_tiles, p_tiles, p_tiles, g_tiles, _cumsum_pivot_matrix(c), _level_masks(c), s0)
    return o, s_out


def gla_layer(h, nb, seq, gain, s0, w_in, w_gate_up, b_gate, norm_o, w_out):
    c = CHUNK if seq % CHUNK == 0 else seq
    n_main = 2 * GLA_KEY + 2 * GLA_VAL
    p_tiles = norm_matmul(h, gain, w_in[:, :n_main].astype(BF16), GLA_TN)
    w_gl = jnp.pad(w_in[:, n_main:], ((0, 0), (0, LANES - GLA_RANK))).astype(BF16)
    gl_tiles = norm_matmul(h, gain, w_gl, LANES)
    w_gate_pad = jnp.pad(w_gate_up, ((0, LANES - GLA_RANK), (0, 0))).astype(BF16)
    g_tiles = gla_gate(gl_tiles, w_gate_pad, b_gate)
    o_tiles, s_out = gla_chunks(p_tiles, g_tiles, s0, nb, seq, c)
    h = gated_out(o_tiles, p_tiles, 2, norm_o, w_out.astype(BF16), h, GLA_HEADS, GLA_DV, TM)
    return h, s_out


def _gdn_conv_kernel(x_ref, cw_ref, c0_ref, o_ref, nc_ref, xp_ref, *, tm, normalize):
    i = pl.program_id(2)
    nb = SUBLANES
    halo = (GDN_CONV - 1) * nb
    scale = jnp.where(pl.program_id(0) == 0, GDN_DK ** -0.5, 1.0)
    for t in range(x_ref.shape[0]):
        lanes = slice(t * LANES, (t + 1) * LANES)

        @pl.when(i == 0)
        def _():
            xp_ref[t, 0:halo, :] = c0_ref[:, lanes]

        xp_ref[t, halo:halo + tm, :] = x_ref[t]
        acc = cw_ref[0:1, lanes] * xp_ref[t, 0:tm, :]
        for j in range(1, GDN_CONV):
            acc = acc + cw_ref[j:j + 1, lanes] * xp_ref[t, j * nb:j * nb + tm, :]
        y = acc * _sigmoid(acc)
        if normalize:
            y = y * (lax.rsqrt(jnp.sum(y * y, axis=-1, keepdims=True) + NORM_EPS) * scale)
        o_ref[t] = y
        tail = xp_ref[t, tm:tm + halo, :]
        xp_ref[t, 0:halo, :] = tail

        @pl.when(i == pl.num_programs(2) - 1)
        def _():
            nc_ref[:, lanes] = tail


def gdn_conv(p_tiles, group0, conv_w, conv0, nb, normalize):
    rows = p_tiles.shape[1]
    ngrp = nb // SUBLANES
    halo = (GDN_CONV - 1) * SUBLANES
    tm = min(TM, rows // ngrp)
    nblk = rows // ngrp // tm
    gt = GDN_TN // LANES
    assert tm >= halo
    return pl.pallas_call(
        functools.partial(_gdn_conv_kernel, tm=tm, normalize=normalize), grid=(2, ngrp, nblk),
        in_specs=[pl.BlockSpec((gt, tm, LANES), lambda j, g, i: (group0 + j, g * nblk + i, 0)),
                  pl.BlockSpec((GDN_CONV, GDN_TN), lambda j, g, i: (0, group0 + j)),
                  pl.BlockSpec((halo, GDN_TN), lambda j, g, i: (g, group0 + j))],
        out_specs=[pl.BlockSpec((gt, tm, LANES), lambda j, g, i: (j, g * nblk + i, 0)),
                   pl.BlockSpec((halo, GDN_TN), lambda j, g, i: (g, j))],
        out_shape=(SDS((2 * gt, rows, LANES), F32), SDS((ngrp * halo, 2 * GDN_TN), F32)),
        scratch_shapes=[pltpu.VMEM((gt, halo + tm, LANES), F32)],
        compiler_params=_cparams(3), name="gdn_conv",
    )(p_tiles, conv_w, conv0)


def _gdn_chunk_kernel(q_ref, k_ref, v_ref, ba_ref, alog_ref, dtb_ref, tri_ref, mask_ref, maskb_ref, s0_ref,
                      o_ref, s_ref, *, c):
    @pl.when(pl.program_id(2) == 0)
    def _():
        s_ref[...] = s0_ref[...]

    tri = tri_ref[...]
    eye_l = _eye(LANES)
    eye_c = mask_ref[0]
    strict = tri - eye_c
    nlv = mask_ref.shape[0]
    rep = GDN_V_HEADS // GDN_QK_HEADS

    def per_sequence(b, carry):
        rows = pl.ds(b, c, stride=SUBLANES)
        ba = ba_ref[0, rows, :]
        beta_all = _sigmoid(ba)
        g_all = -jnp.exp(alog_ref[0]) * _softplus(ba + dtb_ref[0])
        gc_all = _dot_exact01(tri, g_all)
        gr_all = _dot_nt_exact01(eye_l, gc_all)
        heads = []
        for qh in range(GDN_HALF // rep):
            q = q_ref[qh, rows, :]
            k = k_ref[qh, rows, :]
            kb = k.astype(BF16)
            kq = _dot_nt(jnp.concatenate([kb, q.astype(BF16)], axis=0), kb)
            kk, qk = kq[:c], kq[c:]
            for j in range(rep):
                hh = qh * rep + j
                beta = beta_all[:, hh:hh + 1]
                g_col = gc_all[:, GDN_HALF + hh:GDN_HALF + hh + 1]
                g_row = gr_all[GDN_HALF + hh:GDN_HALF + hh + 1, :]
                decay = tri * jnp.exp(jnp.minimum(g_col - g_row, 0.0))
                m = strict * (kk * decay * beta)
                heads.append(dict(hh=hh, q=q, k=k, v=v_ref[hh, rows, :], s=s_ref[b, hh], beta=beta,
                                  g_col=g_col, mb=m.astype(BF16), qkd=(qk * decay).astype(BF16),
                                  t=eye_c - mask_ref[1] * m))
        for lv in range(2, nlv):
            for hd in heads:
                tb = hd["t"].astype(BF16)
                hd["t"] = hd["t"] - _dot(_dot(tb, maskb_ref[lv] * hd["mb"]).astype(BF16), tb)
        for hd in heads:
            e_g = jnp.exp(hd["g_col"])
            rhs = jnp.concatenate([hd["v"] * hd["beta"], hd["k"] * (hd["beta"] * e_g)], axis=1).astype(BF16)
            hd["uw"] = _dot(hd["t"].astype(BF16), rhs)
            hd["q_dec"] = hd["q"] * e_g
        for hd in heads:
            uw = hd["uw"]
            ws = _dot(jnp.concatenate([uw[:, GDN_DV:], hd["q_dec"]], axis=0).astype(BF16),
                      hd["s"].astype(BF16))
            hd["vnb"] = (uw[:, :GDN_DV] - ws[:c]).astype(BF16)
            hd["o_inter"] = ws[c:]
        for hd in heads:
            g_col = hd["g_col"]
            g_last = g_col[c - 1:c, :]
            k_dec = (hd["k"] * jnp.exp(g_last - g_col)).astype(BF16)
            hd["o"] = hd["o_inter"] + _dot(hd["qkd"], hd["vnb"])
            hd["s_new"] = jnp.exp(g_last) * hd["s"] + _dot_tn(k_dec, hd["vnb"])
        for hd in heads:
            o_ref[hd["hh"], rows, :] = hd["o"]
            s_ref[b, hd["hh"]] = hd["s_new"]
        return carry

    lax.fori_loop(0, SUBLANES, per_sequence, 0)


def gdn_chunks(qk_tiles, v_tiles, ba_tiles, a_log, dt_bias, s0, nb, seq, c):
    rows = qk_tiles.shape[1]
    nqk = GDN_HALF // (GDN_V_HEADS // GDN_QK_HEADS)
    nc = seq // c
    pad = lambda x: jnp.pad(x.reshape(2, 1, GDN_HALF), ((0, 0), (0, 0), (GDN_HALF, LANES - 2 * GDN_HALF)))
    masks = _level_masks(c)
    tri = jnp.asarray(np.tril(np.ones((c, c), np.float32)))
    nl = masks.shape[0]
    tiles = lambda nt, blk: pl.BlockSpec((nt, c * SUBLANES, LANES), lambda b, hf, n: (blk(hf), b * nc + n, 0))
    const = lambda shape: pl.BlockSpec(shape, lambda b, hf, n: (0,) * len(shape))
    state = pl.BlockSpec((SUBLANES, GDN_HALF, GDN_DK, GDN_DV), lambda b, hf, n: (b, hf, 0, 0))
    o, s_out = pl.pallas_call(
        functools.partial(_gdn_chunk_kernel, c=c), grid=(nb // SUBLANES, 2, seq // c),
        in_specs=[tiles(nqk, lambda hf: hf), tiles(nqk, lambda hf: 2 + hf),
                  tiles(GDN_HALF, lambda hf: hf), tiles(1, lambda hf: hf),
                  pl.BlockSpec((1, 1, LANES), lambda b, hf, n: (hf, 0, 0)),
                  pl.BlockSpec((1, 1, LANES), lambda b, hf, n: (hf, 0, 0)),
                  const((c, c)), const((nl, c, c)), const((nl, c, c)), state],
        out_specs=[tiles(GDN_HALF, lambda hf: hf), state],
        out_shape=(SDS((GDN_V_HEADS, rows, LANES), F32), SDS((nb, GDN_V_HEADS, GDN_DK, GDN_DV), F32)),
        compiler_params=_cparams(3), name="gdn_chunks",
    )(qk_tiles, qk_tiles, v_tiles, ba_tiles, pad(a_log), pad(dt_bias), tri, masks, masks.astype(BF16), s0)
    return o, s_out


def gdn_layer(h, nb, seq, gain, s0, conv0, w_in, conv_w, a_log, dt_bias, norm_o, w_out):
    c = CHUNK if seq % CHUNK == 0 else seq
    n_main = GDN_CONV_DIM + GDN_VAL
    p_tiles = norm_matmul(h, gain, w_in[:, :n_main].astype(BF16), GDN_TN)
    w_b = w_in[:, n_main:n_main + GDN_V_HEADS].reshape(D_MODEL, 2, GDN_HALF)
    w_a = w_in[:, n_main + GDN_V_HEADS:].reshape(D_MODEL, 2, GDN_HALF)
    w_ba = jnp.pad(jnp.concatenate([w_b, w_a], axis=2), ((0, 0), (0, 0), (0, LANES - 2 * GDN_HALF)))
    ba_tiles = norm_matmul(h, gain, w_ba.reshape(D_MODEL, 2 * LANES).astype(BF16), LANES)
    ngrp = nb // SUBLANES
    conv0_tm = conv0.reshape(ngrp, SUBLANES, GDN_CONV - 1, GDN_CONV_DIM).transpose(0, 2, 1, 3)
    conv0_tm = conv0_tm.reshape(ngrp * (GDN_CONV - 1) * SUBLANES, GDN_CONV_DIM)
    qk_tiles, nc_qk = gdn_conv(p_tiles, 0, conv_w, conv0_tm, nb, True)
    v_tiles, nc_v = gdn_conv(p_tiles, 2, conv_w, conv0_tm, nb, False)
    o_tiles, s_out = gdn_chunks(qk_tiles, v_tiles, ba_tiles, a_log, dt_bias, s0, nb, seq, c)
    h = gated_out(o_tiles, p_tiles, 2, norm_o, w_out.astype(BF16), h, GDN_V_HEADS, GDN_DV, TM // 2)
    new_conv = jnp.concatenate([nc_qk, nc_v], axis=1).reshape(ngrp, GDN_CONV - 1, SUBLANES, GDN_CONV_DIM)
    return h, s_out, new_conv.transpose(0, 2, 1, 3).reshape(nb, GDN_CONV - 1, GDN_CONV_DIM)


def _trunk(x, s5_re, s5_im, gla_s, gdn_s, gdn_conv_s, w):
    nb, seq, d = x.shape
    h, hn = norm_in(x, w["norm_mix"][0])
    h, s5r0, s5i0 = s5_layer(h, hn, nb, s5_re[0], s5_im[0], w["s5_a_re"][0], w["s5_a_im"][0],
                             w["s5_log_dt"][0], w["s5_b_re"][0], w["s5_b_im"][0], w["s5_c_re"][0],
                             w["s5_c_im"][0], w["s5_d"][0], w["s5_w_glu"][0])
    h = ffn(h, w["norm_ffn"][0], w["w_up"][0], w["w_down"][0])
    h, gla_o = gla_layer(h, nb, seq, w["norm_mix"][1], gla_s[0], w["gla_w_in"][0], w["gla_w_gate_up"][0],
                         w["gla_b_gate"][0], w["gla_norm"][0], w["gla_w_out"][0])
    h = ffn(h, w["norm_ffn"][1], w["w_up"][1], w["w_down"][1])
    h, gdn_o, conv_o = gdn_layer(h, nb, seq, w["norm_mix"][2], gdn_s[0], gdn_conv_s[0], w["gdn_w_in"][0],
                                 w["gdn_conv_w"][0], w["gdn_a_log"][0], w["gdn_dt_bias"][0],
                                 w["gdn_norm"][0], w["gdn_w_out"][0])
    h = ffn(h, w["norm_ffn"][2], w["w_up"][2], w["w_down"][2])
    hn = rmsnorm_rows(h, w["norm_mix"][3])
    h, s5r1, s5i1 = s5_layer(h, hn, nb, s5_re[1], s5_im[1], w["s5_a_re"][1], w["s5_a_im"][1],
                             w["s5_log_dt"][1], w["s5_b_re"][1], w["s5_b_im"][1], w["s5_c_re"][1],
                             w["s5_c_im"][1], w["s5_d"][1], w["s5_w_glu"][1])
    h = ffn(h, w["norm_ffn"][3], w["w_up"][3], w["w_down"][3])
    y = norm_out(h, w["norm_final"], nb, seq)
    return (y, jnp.stack([s5r0, s5r1]), jnp.stack([s5i0, s5i1]), gla_o[None], gdn_o[None], conv_o[None])


def kernel(x_prompt, x_sample, state_s5_re, state_s5_im, state_gla, state_gdn, state_gdn_conv, norm_mix, norm_ffn, norm_final, w_up, w_down, s5_a_re, s5_a_im, s5_log_dt, s5_b_re, s5_b_im, s5_c_re, s5_c_im, s5_d, s5_w_glu, gla_w_in, gla_w_gate_up, gla_b_gate, gla_norm, gla_w_out, gdn_w_in, gdn_conv_w, gdn_a_log, gdn_dt_bias, gdn_norm, gdn_w_out):
    w = dict(norm_mix=norm_mix, norm_ffn=norm_ffn, norm_final=norm_final,
             w_up=w_up.astype(BF16), w_down=w_down.astype(BF16),
             s5_a_re=s5_a_re, s5_a_im=s5_a_im, s5_log_dt=s5_log_dt, s5_b_re=s5_b_re, s5_b_im=s5_b_im,
             s5_c_re=s5_c_re, s5_c_im=s5_c_im, s5_d=s5_d, s5_w_glu=s5_w_glu,
             gla_w_in=gla_w_in, gla_w_gate_up=gla_w_gate_up, gla_b_gate=gla_b_gate, gla_norm=gla_norm,
             gla_w_out=gla_w_out, gdn_w_in=gdn_w_in, gdn_conv_w=gdn_conv_w, gdn_a_log=gdn_a_log,
             gdn_dt_bias=gdn_dt_bias, gdn_norm=gdn_norm, gdn_w_out=gdn_w_out)
    bp = x_prompt.shape[0]
    dt = x_prompt.dtype
    z_s5 = jnp.zeros((state_s5_re.shape[0], bp) + state_s5_re.shape[2:], dt)
    z_gla = jnp.zeros((state_gla.shape[0], bp) + state_gla.shape[2:], dt)
    z_gdn = jnp.zeros((state_gdn.shape[0], bp) + state_gdn.shape[2:], dt)
    z_conv = jnp.zeros((state_gdn_conv.shape[0], bp) + state_gdn_conv.shape[2:], dt)
    out_p = _trunk(x_prompt, z_s5, z_s5, z_gla, z_gdn, z_conv, w)
    out_s = _trunk(x_sample, state_s5_re, state_s5_im, state_gla, state_gdn, state_gdn_conv, w)
    return (out_p[0], out_s[0]) + out_p[1:] + out_s[1:]
```

```python
import functools
import math

import numpy as np
import jax
import jax.numpy as jnp
from jax import lax
from jax.experimental import pallas as pl
from jax.experimental.pallas import tpu as pltpu

F32 = jnp.float32
BF16 = jnp.bfloat16
SDS = jax.ShapeDtypeStruct

D_MODEL = 1024
NORM_EPS = 1e-6

S5_GROUP = 16
S5_STATE = 64
S5_GROUPS = D_MODEL // S5_GROUP
S5_GB = 16
S5_NGB = S5_GROUPS // S5_GB
S5_BC = S5_GB * S5_GROUP
S5_BS = S5_GB * S5_STATE

GLA_HEADS = 4
GLA_DK = 128
GLA_DV = 256
GLA_KEY = GLA_HEADS * GLA_DK
GLA_VAL = GLA_HEADS * GLA_DV
GLA_RANK = 16
GLA_TAU = 16.0
GLA_TN = 512

GDN_DK = 128
GDN_DV = 128
GDN_QK_HEADS = 8
GDN_V_HEADS = 16
GDN_KEY = GDN_QK_HEADS * GDN_DK
GDN_VAL = GDN_V_HEADS * GDN_DV
GDN_HALF = GDN_V_HEADS // 2
GDN_CONV = 4
GDN_CONV_DIM = 2 * GDN_KEY + GDN_VAL
GDN_TN = 1024
GDN_PROJ_TN = 1280

CHUNK = 128
LANES = 128
SUBLANES = 8
TM = 1024
FFN_TF = 1024
MIB = 1024 * 1024


def _cparams(n_axes, vmem_mib=48):
    return pltpu.CompilerParams(dimension_semantics=("arbitrary",) * n_axes,
                                vmem_limit_bytes=vmem_mib * MIB)


def _rms(x, gain):
    ms = jnp.mean(x * x, axis=-1, keepdims=True)
    return x * lax.rsqrt(ms + NORM_EPS) * gain


def _sigmoid(x):
    return 1.0 / (1.0 + jnp.exp(-x))


def _softplus(x):
    return jnp.maximum(x, 0.0) + jnp.log1p(jnp.exp(-jnp.abs(x)))


def _gelu_tanh(x):
    c = math.sqrt(2.0 / math.pi)
    return x * (0.5 * (1.0 + jnp.tanh(c * (x + 0.044715 * (x * x * x)))))


def _dot(a, b):
    return jnp.dot(a, b, preferred_element_type=F32)


def _dot_nt(a, b):
    return lax.dot_general(a, b, (((1,), (1,)), ((), ())), preferred_element_type=F32)


def _dot_tn(a, b):
    return lax.dot_general(a, b, (((0,), (0,)), ((), ())), preferred_element_type=F32)


def _eye(n):
    return (lax.broadcasted_iota(jnp.int32, (n, n), 0)
            == lax.broadcasted_iota(jnp.int32, (n, n), 1)).astype(F32)


def _split3(x):
    x1 = x.astype(BF16)
    r1 = x - x1.astype(F32)
    x2 = r1.astype(BF16)
    x3 = (r1 - x2.astype(F32)).astype(BF16)
    return x1, x2, x3


def _dot_exact01(m01, x):
    mb = m01.astype(BF16)
    x1, x2, x3 = _split3(x)
    return _dot(mb, x1) + _dot(mb, x2) + _dot(mb, x3)


def _dot_nt_exact01(m01, x):
    mb = m01.astype(BF16)
    x1, x2, x3 = _split3(x)
    return _dot_nt(mb, x1) + _dot_nt(mb, x2) + _dot_nt(mb, x3)


def _rmsnorm_kernel(h_ref, g_ref, o_ref):
    o_ref[...] = _rms(h_ref[...], g_ref[...])


def rmsnorm_rows(h, gain):
    rows, d = h.shape
    return pl.pallas_call(
        _rmsnorm_kernel, grid=(rows // TM,),
        in_specs=[pl.BlockSpec((TM, d), lambda i: (i, 0)), pl.BlockSpec((1, d), lambda i: (0, 0))],
        out_specs=pl.BlockSpec((TM, d), lambda i: (i, 0)),
        out_shape=SDS((rows, d), F32), compiler_params=_cparams(1), name="rmsnorm",
    )(h, gain.reshape(1, d))


def _norm_in_kernel(x_ref, g_ref, h_ref, hn_ref):
    for j in range(SUBLANES):
        x = x_ref[j]
        h_ref[:, j, :] = x
        hn_ref[:, j, :] = _rms(x, g_ref[...])


def norm_in(x, gain):
    nb, seq, d = x.shape
    tt = min(seq, TM // SUBLANES)
    nblk = seq // tt
    out = SDS((nb // SUBLANES * seq, SUBLANES, d), F32)
    h, hn = pl.pallas_call(
        _norm_in_kernel, grid=(nb // SUBLANES, nblk),
        in_specs=[pl.BlockSpec((SUBLANES, tt, d), lambda g, i: (g, i, 0)),
                  pl.BlockSpec((1, d), lambda g, i: (0, 0))],
        out_specs=[pl.BlockSpec((tt, SUBLANES, d), lambda g, i: (g * nblk + i, 0, 0))] * 2,
        out_shape=(out, out), compiler_params=_cparams(2), name="norm_in",
    )(x, gain.reshape(1, d))
    return h.reshape(seq * nb, d), hn.reshape(seq * nb, d)


def _norm_out_kernel(h_ref, g_ref, y_ref):
    for j in range(SUBLANES):
        y_ref[j] = _rms(h_ref[:, j, :], g_ref[...])


def norm_out(h, gain, nb, seq):
    d = h.shape[1]
    tt = min(seq, TM // SUBLANES)
    nblk = seq // tt
    return pl.pallas_call(
        _norm_out_kernel, grid=(nb // SUBLANES, nblk),
        in_specs=[pl.BlockSpec((tt, SUBLANES, d), lambda g, i: (g * nblk + i, 0, 0)),
                  pl.BlockSpec((1, d), lambda g, i: (0, 0))],
        out_specs=pl.BlockSpec((SUBLANES, tt, d), lambda g, i: (g, i, 0)),
        out_shape=SDS((nb, seq, d), F32), compiler_params=_cparams(2), name="norm_out",
    )(h.reshape(nb // SUBLANES * seq, SUBLANES, d), gain.reshape(1, d))


def _norm_matmul_kernel(h_ref, g_ref, w_ref, o_ref, hn_ref):
    @pl.when(pl.program_id(1) == 0)
    def _():
        hn_ref[...] = _rms(h_ref[...], g_ref[...]).astype(BF16)

    res = _dot(hn_ref[...], w_ref[...])
    for t in range(o_ref.shape[0]):
        o_ref[t] = res[:, t * LANES:(t + 1) * LANES]


def norm_matmul(h, gain, w, tn):
    rows, d = h.shape
    n = w.shape[1]
    return pl.pallas_call(
        _norm_matmul_kernel, grid=(rows // TM, n // tn),
        in_specs=[pl.BlockSpec((TM, d), lambda i, j: (i, 0)),
                  pl.BlockSpec((1, d), lambda i, j: (0, 0)),
                  pl.BlockSpec((d, tn), lambda i, j: (0, j))],
        out_specs=pl.BlockSpec((tn // LANES, TM, LANES), lambda i, j: (j, i, 0)),
        out_shape=SDS((n // LANES, rows, LANES), F32),
        scratch_shapes=[pltpu.VMEM((TM, d), BF16)],
        compiler_params=_cparams(2), name="norm_matmul",
    )(h, gain.reshape(1, d), w)


def _ffn_kernel(h_ref, g_ref, wu_ref, wd_ref, o_ref, hn_ref, acc_ref):
    j = pl.program_id(1)

    @pl.when(j == 0)
    def _():
        hn_ref[...] = _rms(h_ref[...], g_ref[...]).astype(BF16)
        acc_ref[...] = jnp.zeros_like(acc_ref)

    a = jnp.square(jnp.maximum(_dot(hn_ref[...], wu_ref[...]), 0.0)).astype(BF16)
    acc_ref[...] += _dot(a, wd_ref[...])

    @pl.when(j == pl.num_programs(1) - 1)
    def _():
        o_ref[...] = h_ref[...] + acc_ref[...]


def ffn(h, gain, w_up, w_down):
    rows, d = h.shape
    f = w_up.shape[1]
    return pl.pallas_call(
        _ffn_kernel, grid=(rows // TM, f // FFN_TF),
        in_specs=[pl.BlockSpec((TM, d), lambda i, j: (i, 0)),
                  pl.BlockSpec((1, d), lambda i, j: (0, 0)),
                  pl.BlockSpec((d, FFN_TF), lambda i, j: (0, j)),
                  pl.BlockSpec((FFN_TF, d), lambda i, j: (j, 0))],
        out_specs=pl.BlockSpec((TM, d), lambda i, j: (i, 0)),
        out_shape=SDS((rows, d), F32),
        scratch_shapes=[pltpu.VMEM((TM, d), BF16), pltpu.VMEM((TM, d), F32)],
        compiler_params=_cparams(2), name="ffn",
    )(h, gain.reshape(1, d), w_up, w_down)


def _gated_out_kernel(o_ref, z_ref, gn_ref, w_ref, h_ref, out_ref, *, nheads, hd):
    tph = hd // LANES
    sub = 2 * LANES
    for rb in range(out_ref.shape[0] // sub):
        rows = slice(rb * sub, (rb + 1) * sub)
        parts = []
        for hh in range(nheads):
            tiles = range(hh * tph, (hh + 1) * tph)
            o = jnp.concatenate([o_ref[t, rows, :] for t in tiles], axis=1)
            z = jnp.concatenate([z_ref[t, rows, :] for t in tiles], axis=1)
            parts.append((_rms(o, gn_ref[...]) * (z * _sigmoid(z))).astype(BF16))
        out_ref[rows, :] = h_ref[rows, :] + _dot(jnp.concatenate(parts, axis=1), w_ref[...])


def gated_out(o_tiles, p_tiles, z_block, gain, w_out, h, nheads, hd, tm):
    rows, d = h.shape
    kdim = nheads * hd
    nt = kdim // LANES
    return pl.pallas_call(
        functools.partial(_gated_out_kernel, nheads=nheads, hd=hd), grid=(rows // tm,),
        in_specs=[pl.BlockSpec((nt, tm, LANES), lambda i: (0, i, 0)),
                  pl.BlockSpec((nt, tm, LANES), lambda i: (z_block, i, 0)),
                  pl.BlockSpec((1, hd), lambda i: (0, 0)),
                  pl.BlockSpec((kdim, d), lambda i: (0, 0)),
                  pl.BlockSpec((tm, d), lambda i: (i, 0))],
        out_specs=pl.BlockSpec((tm, d), lambda i: (i, 0)),
        out_shape=SDS((rows, d), F32),
        compiler_params=_cparams(1), name="gated_out",
    )(o_tiles, p_tiles, gain.reshape(1, hd), w_out, h)


def _s5_discretize_kernel(are_ref, aim_ref, ldt_ref, bre_ref, bim_ref,
                          abr_ref, abi_ref, bbr_ref, bbi_ref):
    a_re, a_im = are_ref[...], aim_ref[...]
    dt = jnp.exp(ldt_ref[...])
    mag = jnp.exp(a_re * dt)
    ab_re = mag * jnp.cos(a_im * dt)
    ab_im = mag * jnp.sin(a_im * dt)
    den = a_re * a_re + a_im * a_im
    c_re = ((ab_re - 1.0) * a_re + ab_im * a_im) / den
    c_im = (ab_im * a_re - (ab_re - 1.0) * a_im) / den
    abr_ref[...] = ab_re
    abi_ref[...] = ab_im
    bbr_ref[...] = c_re * bre_ref[...] - c_im * bim_ref[...]
    bbi_ref[...] = c_re * bim_ref[...] + c_im * bre_ref[...]


def s5_discretize(a_re, a_im, log_dt, b_re, b_im):
    g, p, c = S5_GROUPS, S5_STATE, S5_GROUP
    expand = lambda v: jnp.broadcast_to(v[..., None], (g, p, c)).reshape(g, p * c)
    ldt = jnp.broadcast_to(log_dt[:, None], (g, p * c))
    shp = SDS((g, p * c), F32)
    ab_re, ab_im, bb_re, bb_im = pl.pallas_call(
        _s5_discretize_kernel, out_shape=(shp, shp, shp, shp), name="s5_discretize",
    )(expand(a_re), expand(a_im), ldt, b_re.reshape(g, p * c), b_im.reshape(g, p * c))
    ab_re = ab_re.reshape(g, p, c)[:, :, 0].reshape(S5_NGB, 1, S5_BS)
    ab_im = ab_im.reshape(g, p, c)[:, :, 0].reshape(S5_NGB, 1, S5_BS)
    eye = jnp.eye(S5_GB, dtype=F32)

    def block_diag_in(bb):
        bb = bb.reshape(S5_NGB, S5_GB, p, c)
        return jnp.einsum("bgpc,gh->bgchp", bb, eye).reshape(S5_NGB, S5_BC, S5_BS)

    b_blk = jnp.concatenate([block_diag_in(bb_re.reshape(g, p, c)),
                             block_diag_in(bb_im.reshape(g, p, c))], axis=-1).astype(BF16)
    return ab_re, ab_im, b_blk


def s5_block_diag_out(c_par):
    eye = jnp.eye(S5_GB, dtype=F32)
    cc = c_par.reshape(S5_NGB, S5_GB, S5_GROUP, S5_STATE)
    return jnp.einsum("bgcp,gh->bgphc", cc, eye).reshape(S5_NGB, S5_BS, S5_BC).astype(BF16)


def _s5_scan_kernel(u_ref, b_ref, cre_ref, cim_ref, are_ref, aim_ref, d_ref, s0r_ref, s0i_ref,
                    g_ref, slr_ref, sli_ref, xr_ref, xi_ref, str_ref, sti_ref, *, ngrp, tc):
    n = pl.program_id(1)

    @pl.when(n == 0)
    def _():
        str_ref[...] = s0r_ref[...]
        sti_ref[...] = s0i_ref[...]

    u = u_ref[...]
    ub = u.astype(BF16)
    xr_ref[...] = _dot(ub, b_ref[0, :, :S5_BS])
    xi_ref[...] = _dot(ub, b_ref[0, :, S5_BS:])

    sub = SUBLANES
    a_re = jnp.broadcast_to(are_ref[0], (sub, S5_BS))
    a_im = jnp.broadcast_to(aim_ref[0], (sub, S5_BS))
    for rb in range(ngrp):
        def step(t, carry, rb=rb):
            x_re, x_im = carry
            off = pl.multiple_of((rb * tc + t) * sub, sub)
            n_re = a_re * x_re - a_im * x_im + xr_ref[pl.ds(off, sub), :]
            n_im = a_re * x_im + a_im * x_re + xi_ref[pl.ds(off, sub), :]
            xr_ref[pl.ds(off, sub), :] = n_re
            xi_ref[pl.ds(off, sub), :] = n_im
            return n_re, n_im

        rows = slice(rb * sub, (rb + 1) * sub)
        x_re, x_im = lax.fori_loop(0, tc, step, (str_ref[rows, :], sti_ref[rows, :]))
        str_ref[rows, :] = x_re
        sti_ref[rows, :] = x_im

    y = _dot(xr_ref[...].astype(BF16), cre_ref[0]) - _dot(xi_ref[...].astype(BF16), cim_ref[0])
    y = y + d_ref[...] * u
    g_ref[...] = _gelu_tanh(y).astype(BF16)

    @pl.when(n == pl.num_programs(1) - 1)
    def _():
        slr_ref[...] = str_ref[...]
        sli_ref[...] = sti_ref[...]


def s5_scan(hn, nb, b_blk, c_re_blk, c_im_blk, ab_re, ab_im, d_skip, s0_re, s0_im):
    rows, d = hn.shape
    tc = min(rows // nb, TM // SUBLANES)
    ngrp = TM // (tc * SUBLANES)
    assert ngrp == 1 or ngrp * SUBLANES == nb
    st = SDS((nb, S5_GROUPS * S5_STATE), F32)
    return pl.pallas_call(
        functools.partial(_s5_scan_kernel, ngrp=ngrp, tc=tc), grid=(S5_NGB, rows // TM),
        in_specs=[pl.BlockSpec((TM, S5_BC), lambda gb, n: (n, gb)),
                  pl.BlockSpec((1, S5_BC, 2 * S5_BS), lambda gb, n: (gb, 0, 0)),
                  pl.BlockSpec((1, S5_BS, S5_BC), lambda gb, n: (gb, 0, 0)),
                  pl.BlockSpec((1, S5_BS, S5_BC), lambda gb, n: (gb, 0, 0)),
                  pl.BlockSpec((1, 1, S5_BS), lambda gb, n: (gb, 0, 0)),
                  pl.BlockSpec((1, 1, S5_BS), lambda gb, n: (gb, 0, 0)),
                  pl.BlockSpec((1, S5_BC), lambda gb, n: (0, gb)),
                  pl.BlockSpec((nb, S5_BS), lambda gb, n: (0, gb)),
                  pl.BlockSpec((nb, S5_BS), lambda gb, n: (0, gb))],
        out_specs=[pl.BlockSpec((TM, S5_BC), lambda gb, n: (n, gb)),
                   pl.BlockSpec((nb, S5_BS), lambda gb, n: (0, gb)),
                   pl.BlockSpec((nb, S5_BS), lambda gb, n: (0, gb))],
        out_shape=(SDS((rows, d), BF16), st, st),
        scratch_shapes=[pltpu.VMEM((TM, S5_BS), F32), pltpu.VMEM((TM, S5_BS), F32),
                        pltpu.VMEM((nb, S5_BS), F32), pltpu.VMEM((nb, S5_BS), F32)],
        compiler_params=_cparams(2), name="s5_scan",
    )(hn, b_blk, c_re_blk, c_im_blk, ab_re, ab_im, d_skip.reshape(1, d), s0_re, s0_im)


def _glu_out_kernel(g_ref, w_ref, h_ref, o_ref):
    gv = _dot(g_ref[...], w_ref[...])
    o_ref[...] = h_ref[...] + gv[:, :D_MODEL] * _sigmoid(gv[:, D_MODEL:])


def glu_out(g, w_glu, h):
    rows, d = h.shape
    tm = TM // 2
    return pl.pallas_call(
        _glu_out_kernel, grid=(rows // tm,),
        in_specs=[pl.BlockSpec((tm, d), lambda i: (i, 0)),
                  pl.BlockSpec((d, 2 * d), lambda i: (0, 0)),
                  pl.BlockSpec((tm, d), lambda i: (i, 0))],
        out_specs=pl.BlockSpec((tm, d), lambda i: (i, 0)),
        out_shape=SDS((rows, d), F32), compiler_params=_cparams(1), name="glu_out",
    )(g, w_glu, h)


def s5_layer(h, hn, nb, s0_re, s0_im, a_re, a_im, log_dt, b_re, b_im, c_re, c_im, d_skip, w_glu):
    ab_re, ab_im, b_blk = s5_discretize(a_re, a_im, log_dt, b_re, b_im)
    g, sl_re, sl_im = s5_scan(hn, nb, b_blk, s5_block_diag_out(c_re), s5_block_diag_out(c_im),
                              ab_re, ab_im, d_skip, s0_re.reshape(nb, -1), s0_im.reshape(nb, -1))
    shape = (nb, S5_GROUPS, S5_STATE)
    return glu_out(g, w_glu.astype(BF16), h), sl_re.reshape(shape), sl_im.reshape(shape)


class _Problem:
    def __init__(self, c, pack):
        p = c * pack
        r = np.arange(p)
        seq, time = r % pack, r // pack
        same = seq[:, None] == seq[None, :]
        self.p, self.pack = p, pack
        self.tri = same & (time[None, :] <= time[:, None])
        self.whole = same
        masks, pivots = [r[:, None] == r[None, :]], []
        for sz in [2 ** i for i in range(1, int(math.log2(p)) + 1)]:
            blk, off = r // sz, r % sz
            m = same & (blk[:, None] == blk[None, :]) & (off[:, None] >= sz // 2) & (off[None, :] < sz // 2)
            if not m.any():
                continue
            lower = same & (blk[:, None] == blk[None, :]) & (off[None, :] < sz // 2)
            piv = np.where(lower.any(1), (lower * r[None, :]).max(1), r)
            masks.append(m)
            pivots.append(self.tri[piv])
        self.masks = np.stack(masks)
        self.pivots = pivots
        self.seq_lanes = np.repeat(seq[:, None] == np.arange(pack)[None, :], LANES, axis=1)

    def f32(self, x):
        return jnp.asarray(np.asarray(x, np.float32))


def _gla_gate_kernel(gl_ref, w_ref, b_ref, o_ref):
    x = _dot(gl_ref[0].astype(BF16), w_ref[...]) + b_ref[...]
    g = -_softplus(-x) * (1.0 / GLA_TAU)
    for t in range(o_ref.shape[0]):
        o_ref[t] = g[:, t * LANES:(t + 1) * LANES]


def gla_gate(gl_tiles, w_gate_pad, b_gate):
    rows = gl_tiles.shape[1]
    nt = GLA_KEY // LANES
    return pl.pallas_call(
        _gla_gate_kernel, grid=(rows // TM,),
        in_specs=[pl.BlockSpec((1, TM, LANES), lambda i: (0, i, 0)),
                  pl.BlockSpec((LANES, GLA_KEY), lambda i: (0, 0)),
                  pl.BlockSpec((1, GLA_KEY), lambda i: (0, 0))],
        out_specs=pl.BlockSpec((nt, TM, LANES), lambda i: (0, i, 0)),
        out_shape=SDS((nt, rows, LANES), F32), compiler_params=_cparams(1), name="gla_gate",
    )(gl_tiles, w_gate_pad, b_gate.reshape(1, GLA_KEY))


def _gla_chunk_kernel(q_ref, k_ref, v_ref, g_ref, wst_ref, mask_ref, seqm_ref, s0_ref, o_ref, s_ref, *, p, pack):
    @pl.when(pl.program_id(1) == 0)
    def _():
        s_ref[...] = s0_ref[...]

    wst = wst_ref[...]
    eye = _eye(GLA_DK)
    nlev = mask_ref.shape[0] - 1
    tpv = GLA_DV // LANES

    def problem(rows, seqs):
        loaded = [(q_ref[hh, rows, :], k_ref[hh, rows, :],
                   [v_ref[hh * tpv + t, rows, :] for t in range(tpv)],
                   g_ref[hh, rows, :], [s_ref[j, hh] for j in seqs]) for hh in range(GLA_HEADS)]
        results = []
        work = []
        for q, k, v, g, ss in loaded:
            q = q * (GLA_DK ** -0.5)
            bp = _dot_exact01(wst, g)
            work.append(dict(q=q, k=k, v=v, ss=ss, bp=bp, b=bp[0:p],
                             att=mask_ref[0] * _dot_nt(q.astype(BF16), k.astype(BF16))))
        for lv in range(1, nlev + 1):
            for wk in work:
                e = jnp.exp(-jnp.abs(wk["b"] - wk["bp"][lv * p:(lv + 1) * p]))
                wk["att"] = wk["att"] + mask_ref[lv] * _dot_nt((wk["q"] * e).astype(BF16),
                                                               (wk["k"] * e).astype(BF16))
        for wk in work:
            q, k, v, ss, bp, b, att = (wk[n] for n in ("q", "k", "v", "ss", "bp", "b", "att"))
            vb = jnp.concatenate(v, axis=1).astype(BF16)
            b_end = bp[(nlev + 1) * p:(nlev + 2) * p]
            s_all = jnp.concatenate(ss, axis=0)
            q_dec = q * jnp.exp(b)
            k_dec = k * jnp.exp(b_end - b)
            if pack > 1:
                q_dec = jnp.concatenate([q_dec] * pack, axis=1) * seqm_ref[...]
                k_dec = jnp.concatenate([k_dec] * pack, axis=1) * seqm_ref[...]
            o = _dot(q_dec.astype(BF16), s_all.astype(BF16)) + _dot(att.astype(BF16), vb)
            last = jnp.concatenate([b_end[0:pack]] * (GLA_DK // pack), axis=0) if pack > 1 else (
                jnp.broadcast_to(b_end[0:1], (GLA_DK, GLA_DK)))
            col = jnp.exp(_dot_nt_exact01(eye, last))
            dec = jnp.concatenate([jnp.broadcast_to(col[:, i:i + 1], (GLA_DK, GLA_DV)) for i in range(pack)], axis=0)
            results.append((o, dec * s_all + _dot_tn(k_dec.astype(BF16), vb)))
        for hh, (o, s_new) in enumerate(results):
            for t in range(tpv):
                o_ref[hh * tpv + t, rows, :] = o[:, t * LANES:(t + 1) * LANES]
            for i, j in enumerate(seqs):
                s_ref[j, hh] = s_new[i * GLA_DK:(i + 1) * GLA_DK]

    if pack == 1:
        def per_sequence(j, carry):
            problem(pl.ds(j, p, stride=SUBLANES), [j])
            return carry

        lax.fori_loop(0, SUBLANES, per_sequence, 0)
    else:
        problem(slice(None), list(range(SUBLANES)))


def gla_chunks(p_tiles, g_tiles, s0, nb, seq, c, pack):
    rows = p_tiles.shape[1]
    pr = _Problem(c, pack)
    wst = pr.f32(np.concatenate([pr.tri] + pr.pivots + [pr.whole], axis=0))
    nkt = GLA_KEY // LANES
    nvt = GLA_VAL // LANES
    nc = seq // c
    tiles = lambda nt, blk: pl.BlockSpec((nt, c * SUBLANES, LANES), lambda b, n: (blk, b * nc + n, 0))
    const = lambda x: pl.BlockSpec(x.shape, lambda b, n: (0,) * x.ndim)
    state = pl.BlockSpec((SUBLANES, GLA_HEADS, GLA_DK, GLA_DV), lambda b, n: (b, 0, 0, 0))
    masks, seqm = pr.f32(pr.masks), pr.f32(pr.seq_lanes)
    o, s_out = pl.pallas_call(
        functools.partial(_gla_chunk_kernel, p=pr.p, pack=pack), grid=(nb // SUBLANES, nc),
        in_specs=[tiles(nkt, 0), tiles(nkt, 1), tiles(nvt, 1), tiles(nkt, 0),
                  const(wst), const(masks), const(seqm), state],
        out_specs=[tiles(nvt, 0), state],
        out_shape=(SDS((nvt, rows, LANES), F32), SDS((nb, GLA_HEADS, GLA_DK, GLA_DV), F32)),
        compiler_params=_cparams(2), name="gla_chunks",
    )(p_tiles, p_tiles, p_tiles, g_tiles, wst, masks, seqm, s0)
    return o, s_out


def _chunking(seq):
    return (CHUNK, 1) if seq % CHUNK == 0 else (seq, SUBLANES)


def gla_layer(h, nb, seq, gain, s0, w_in, w_gate_up, b_gate, norm_o, w_out):
    c, pack = _chunking(seq)
    n_main = 2 * GLA_KEY + 2 * GLA_VAL
    p_tiles = norm_matmul(h, gain, w_in[:, :n_main].astype(BF16), GLA_TN)
    w_gl = jnp.pad(w_in[:, n_main:], ((0, 0), (0, LANES - GLA_RANK))).astype(BF16)
    gl_tiles = norm_matmul(h, gain, w_gl, LANES)
    w_gate_pad = jnp.pad(w_gate_up, ((0, LANES - GLA_RANK), (0, 0))).astype(BF16)
    g_tiles = gla_gate(gl_tiles, w_gate_pad, b_gate)
    o_tiles, s_out = gla_chunks(p_tiles, g_tiles, s0, nb, seq, c, pack)
    h = gated_out(o_tiles, p_tiles, 2, norm_o, w_out.astype(BF16), h, GLA_HEADS, GLA_DV, TM)
    return h, s_out


def _gdn_conv_kernel(x_ref, cw_ref, c0_ref, o_ref, nc_ref, xp_ref, *, tm, normalize):
    i = pl.program_id(2)
    nb = SUBLANES
    halo = (GDN_CONV - 1) * nb
    scale = jnp.where(pl.program_id(0) == 0, GDN_DK ** -0.5, 1.0)
    for t in range(x_ref.shape[0]):
        lanes = slice(t * LANES, (t + 1) * LANES)

        @pl.when(i == 0)
        def _():
            xp_ref[t, 0:halo, :] = c0_ref[:, lanes]

        xp_ref[t, halo:halo + tm, :] = x_ref[t]
        acc = cw_ref[0:1, lanes] * xp_ref[t, 0:tm, :]
        for j in range(1, GDN_CONV):
            acc = acc + cw_ref[j:j + 1, lanes] * xp_ref[t, j * nb:j * nb + tm, :]
        y = acc * _sigmoid(acc)
        if normalize:
            y = y * (lax.rsqrt(jnp.sum(y * y, axis=-1, keepdims=True) + NORM_EPS) * scale)
        o_ref[t] = y
        tail = xp_ref[t, tm:tm + halo, :]
        xp_ref[t, 0:halo, :] = tail

        @pl.when(i == pl.num_programs(2) - 1)
        def _():
            nc_ref[:, lanes] = tail


def gdn_conv(p_tiles, group0, conv_w, conv0, nb, normalize):
    rows = p_tiles.shape[1]
    ngrp = nb // SUBLANES
    halo = (GDN_CONV - 1) * SUBLANES
    tm = min(TM, rows // ngrp)
    nblk = rows // ngrp // tm
    gt = GDN_TN // LANES
    assert tm >= halo
    return pl.pallas_call(
        functools.partial(_gdn_conv_kernel, tm=tm, normalize=normalize), grid=(2, ngrp, nblk),
        in_specs=[pl.BlockSpec((gt, tm, LANES), lambda j, g, i: (group0 + j, g * nblk + i, 0)),
                  pl.BlockSpec((GDN_CONV, GDN_TN), lambda j, g, i: (0, group0 + j)),
                  pl.BlockSpec((halo, GDN_TN), lambda j, g, i: (g, group0 + j))],
        out_specs=[pl.BlockSpec((gt, tm, LANES), lambda j, g, i: (j, g * nblk + i, 0)),
                   pl.BlockSpec((halo, GDN_TN), lambda j, g, i: (g, j))],
        out_shape=(SDS((2 * gt, rows, LANES), F32), SDS((ngrp * halo, 2 * GDN_TN), F32)),
        scratch_shapes=[pltpu.VMEM((gt, halo + tm, LANES), F32)],
        compiler_params=_cparams(3), name="gdn_conv",
    )(p_tiles, conv_w, conv0)


def _gdn_chunk_kernel(q_ref, k_ref, v_ref, ba_ref, alog_ref, dtb_ref, tril_ref, mask_ref, maskb_ref, seqm_ref,
                      s0_ref, o_ref, s_ref, *, p, pack):
    @pl.when(pl.program_id(2) == 0)
    def _():
        s_ref[...] = s0_ref[...]

    tri = tril_ref[0:p, :]
    eye_l = _eye(LANES)
    eye_c = mask_ref[0]
    strict = tri - eye_c
    nlv = mask_ref.shape[0]
    rep = GDN_V_HEADS // GDN_QK_HEADS

    def problem(rows, seqs):
        ba = ba_ref[0, rows, :]
        beta_all = _sigmoid(ba)
        g_all = -jnp.exp(alog_ref[0]) * _softplus(ba + dtb_ref[0])
        gcl = _dot_exact01(tril_ref[...], g_all)
        gc_all, ge_all = gcl[0:p], gcl[p:2 * p]
        gr_all = _dot_nt_exact01(eye_l, gc_all)
        heads = []
        for qh in range(GDN_HALF // rep):
            q = q_ref[qh, rows, :]
            k = k_ref[qh, rows, :]
            kb = k.astype(BF16)
            kq = _dot_nt(jnp.concatenate([kb, q.astype(BF16)], axis=0), kb)
            kk, qk = kq[:p], kq[p:]
            for j in range(rep):
                hh = qh * rep + j
                beta = beta_all[:, hh:hh + 1]
                g_col = gc_all[:, GDN_HALF + hh:GDN_HALF + hh + 1]
                g_end = ge_all[:, GDN_HALF + hh:GDN_HALF + hh + 1]
                g_row = gr_all[GDN_HALF + hh:GDN_HALF + hh + 1, :]
                decay = tri * jnp.exp(jnp.minimum(g_col - g_row, 0.0))
                m = strict * (kk * decay * beta)
                heads.append(dict(hh=hh, q=q, k=k, v=v_ref[hh, rows, :], beta=beta, g_col=g_col, g_end=g_end,
                                  s=jnp.concatenate([s_ref[i, hh] for i in seqs], axis=0),
                                  mb=m.astype(BF16), qkd=(qk * decay).astype(BF16),
                                  t=eye_c - mask_ref[1] * m))
        for lv in range(2, nlv):
            for hd in heads:
                tb = hd["t"].astype(BF16)
                hd["t"] = hd["t"] - _dot(_dot(tb, maskb_ref[lv] * hd["mb"]).astype(BF16), tb)
        for hd in heads:
            e_g = jnp.exp(hd["g_col"])
            rhs = jnp.concatenate([hd["v"] * hd["beta"], hd["k"] * (hd["beta"] * e_g)], axis=1).astype(BF16)
            hd["uw"] = _dot(hd["t"].astype(BF16), rhs)
            hd["q_dec"] = hd["q"] * e_g
        for hd in heads:
            uw = hd["uw"]
            lhs = jnp.concatenate([uw[:, GDN_DV:], hd["q_dec"]], axis=0)
            if pack > 1:
                lhs = jnp.concatenate([lhs] * pack, axis=1) * jnp.concatenate([seqm_ref[...]] * 2, axis=0)
            ws = _dot(lhs.astype(BF16), hd["s"].astype(BF16))
            hd["vnb"] = (uw[:, :GDN_DV] - ws[:p]).astype(BF16)
            hd["o_inter"] = ws[p:]
        for hd in heads:
            g_end = hd["g_end"]
            k_dec = hd["k"] * jnp.exp(g_end - hd["g_col"])
            if pack > 1:
                k_dec = jnp.concatenate([k_dec] * pack, axis=1) * seqm_ref[...]
            hd["o"] = hd["o_inter"] + _dot(hd["qkd"], hd["vnb"])
            dec = jnp.concatenate([jnp.broadcast_to(jnp.exp(g_end[i:i + 1, :]), (GDN_DK, GDN_DV))
                                   for i in range(pack)], axis=0)
            hd["s_new"] = dec * hd["s"] + _dot_tn(k_dec.astype(BF16), hd["vnb"])
        for hd in heads:
            o_ref[hd["hh"], rows, :] = hd["o"]
            for i, j in enumerate(seqs):
                s_ref[j, hd["hh"]] = hd["s_new"][i * GDN_DK:(i + 1) * GDN_DK]

    if pack == 1:
        def per_sequence(b, carry):
            problem(pl.ds(b, p, stride=SUBLANES), [b])
            return carry

        lax.fori_loop(0, SUBLANES, per_sequence, 0)
    else:
        problem(slice(None), list(range(SUBLANES)))


def gdn_chunks(qk_tiles, v_tiles, p_tiles, a_log, dt_bias, s0, nb, seq, c, pack):
    ba0 = (GDN_CONV_DIM + GDN_VAL) // LANES
    rows = qk_tiles.shape[1]
    nqk = GDN_HALF // (GDN_V_HEADS // GDN_QK_HEADS)
    nc = seq // c
    pr = _Problem(c, pack)
    pad = lambda x: jnp.pad(x.reshape(2, 1, GDN_HALF), ((0, 0), (0, 0), (GDN_HALF, LANES - 2 * GDN_HALF)))
    masks, seqm = pr.f32(pr.masks), pr.f32(pr.seq_lanes)
    tril = pr.f32(np.concatenate([pr.tri, pr.whole], axis=0))
    tiles = lambda nt, blk: pl.BlockSpec((nt, c * SUBLANES, LANES), lambda b, hf, n: (blk(hf), b * nc + n, 0))
    const = lambda x: pl.BlockSpec(x.shape, lambda b, hf, n: (0,) * x.ndim)
    state = pl.BlockSpec((SUBLANES, GDN_HALF, GDN_DK, GDN_DV), lambda b, hf, n: (b, hf, 0, 0))
    o, s_out = pl.pallas_call(
        functools.partial(_gdn_chunk_kernel, p=pr.p, pack=pack), grid=(nb // SUBLANES, 2, nc),
        in_specs=[tiles(nqk, lambda hf: hf), tiles(nqk, lambda hf: 2 + hf),
                  tiles(GDN_HALF, lambda hf: hf), tiles(1, lambda hf: ba0 + hf),
                  pl.BlockSpec((1, 1, LANES), lambda b, hf, n: (hf, 0, 0)),
                  pl.BlockSpec((1, 1, LANES), lambda b, hf, n: (hf, 0, 0)),
                  const(tril), const(masks), const(masks), const(seqm), state],
        out_specs=[tiles(GDN_HALF, lambda hf: hf), state],
        out_shape=(SDS((GDN_V_HEADS, rows, LANES), F32), SDS((nb, GDN_V_HEADS, GDN_DK, GDN_DV), F32)),
        compiler_params=_cparams(3), name="gdn_chunks",
    )(qk_tiles, qk_tiles, v_tiles, p_tiles, pad(a_log), pad(dt_bias), tril, masks, masks.astype(BF16), seqm, s0)
    return o, s_out


def gdn_layer(h, nb, seq, gain, s0, conv0, w_in, conv_w, a_log, dt_bias, norm_o, w_out):
    c, pack = _chunking(seq)
    n_main = GDN_CONV_DIM + GDN_VAL
    w_b = w_in[:, n_main:n_main + GDN_V_HEADS].reshape(D_MODEL, 2, GDN_HALF)
    w_a = w_in[:, n_main + GDN_V_HEADS:].reshape(D_MODEL, 2, GDN_HALF)
    w_ba = jnp.pad(jnp.concatenate([w_b, w_a], axis=2), ((0, 0), (0, 0), (0, LANES - 2 * GDN_HALF)))
    w_all = jnp.concatenate([w_in[:, :n_main], w_ba.reshape(D_MODEL, 2 * LANES)], axis=1).astype(BF16)
    p_tiles = norm_matmul(h, gain, w_all, GDN_PROJ_TN)
    ngrp = nb // SUBLANES
    conv0_tm = conv0.reshape(ngrp, SUBLANES, GDN_CONV - 1, GDN_CONV_DIM).transpose(0, 2, 1, 3)
    conv0_tm = conv0_tm.reshape(ngrp * (GDN_CONV - 1) * SUBLANES, GDN_CONV_DIM)
    qk_tiles, nc_qk = gdn_conv(p_tiles, 0, conv_w, conv0_tm, nb, True)
    v_tiles, nc_v = gdn_conv(p_tiles, 2, conv_w, conv0_tm, nb, False)
    o_tiles, s_out = gdn_chunks(qk_tiles, v_tiles, p_tiles, a_log, dt_bias, s0, nb, seq, c, pack)
    h = gated_out(o_tiles, p_tiles, 2, norm_o, w_out.astype(BF16), h, GDN_V_HEADS, GDN_DV, TM // 2)
    new_conv = jnp.concatenate([nc_qk, nc_v], axis=1).reshape(ngrp, GDN_CONV - 1, SUBLANES, GDN_CONV_DIM)
    return h, s_out, new_conv.transpose(0, 2, 1, 3).reshape(nb, GDN_CONV - 1, GDN_CONV_DIM)


def _trunk(x, s5_re, s5_im, gla_s, gdn_s, gdn_conv_s, w):
    nb, seq, d = x.shape
    h, hn = norm_in(x, w["norm_mix"][0])
    h, s5r0, s5i0 = s5_layer(h, hn, nb, s5_re[0], s5_im[0], w["s5_a_re"][0], w["s5_a_im"][0],
                             w["s5_log_dt"][0], w["s5_b_re"][0], w["s5_b_im"][0], w["s5_c_re"][0],
                             w["s5_c_im"][0], w["s5_d"][0], w["s5_w_glu"][0])
    h = ffn(h, w["norm_ffn"][0], w["w_up"][0], w["w_down"][0])
    h, gla_o = gla_layer(h, nb, seq, w["norm_mix"][1], gla_s[0], w["gla_w_in"][0], w["gla_w_gate_up"][0],
                         w["gla_b_gate"][0], w["gla_norm"][0], w["gla_w_out"][0])
    h = ffn(h, w["norm_ffn"][1], w["w_up"][1], w["w_down"][1])
    h, gdn_o, conv_o = gdn_layer(h, nb, seq, w["norm_mix"][2], gdn_s[0], gdn_conv_s[0], w["gdn_w_in"][0],
                                 w["gdn_conv_w"][0], w["gdn_a_log"][0], w["gdn_dt_bias"][0],
                                 w["gdn_norm"][0], w["gdn_w_out"][0])
    h = ffn(h, w["norm_ffn"][2], w["w_up"][2], w["w_down"][2])
    hn = rmsnorm_rows(h, w["norm_mix"][3])
    h, s5r1, s5i1 = s5_layer(h, hn, nb, s5_re[1], s5_im[1], w["s5_a_re"][1], w["s5_a_im"][1],
                             w["s5_log_dt"][1], w["s5_b_re"][1], w["s5_b_im"][1], w["s5_c_re"][1],
                             w["s5_c_im"][1], w["s5_d"][1], w["s5_w_glu"][1])
    h = ffn(h, w["norm_ffn"][3], w["w_up"][3], w["w_down"][3])
    y = norm_out(h, w["norm_final"], nb, seq)
    return (y, jnp.stack([s5r0, s5r1]), jnp.stack([s5i0, s5i1]), gla_o[None], gdn_o[None], conv_o[None])


def kernel(x_prompt, x_sample, state_s5_re, state_s5_im, state_gla, state_gdn, state_gdn_conv, norm_mix, norm_ffn, norm_final, w_up, w_down, s5_a_re, s5_a_im, s5_log_dt, s5_b_re, s5_b_im, s5_c_re, s5_c_im, s5_d, s5_w_glu, gla_w_in, gla_w_gate_up, gla_b_gate, gla_norm, gla_w_out, gdn_w_in, gdn_conv_w, gdn_a_log, gdn_dt_bias, gdn_norm, gdn_w_out):
    w = dict(norm_mix=norm_mix, norm_ffn=norm_ffn, norm_final=norm_final,
             w_up=w_up.astype(BF16), w_down=w_down.astype(BF16),
             s5_a_re=s5_a_re, s5_a_im=s5_a_im, s5_log_dt=s5_log_dt, s5_b_re=s5_b_re, s5_b_im=s5_b_im,
             s5_c_re=s5_c_re, s5_c_im=s5_c_im, s5_d=s5_d, s5_w_glu=s5_w_glu,
             gla_w_in=gla_w_in, gla_w_gate_up=gla_w_gate_up, gla_b_gate=gla_b_gate, gla_norm=gla_norm,
             gla_w_out=gla_w_out, gdn_w_in=gdn_w_in, gdn_conv_w=gdn_conv_w, gdn_a_log=gdn_a_log,
             gdn_dt_bias=gdn_dt_bias, gdn_norm=gdn_norm, gdn_w_out=gdn_w_out)
    bp = x_prompt.shape[0]
    dt = x_prompt.dtype
    z_s5 = jnp.zeros((state_s5_re.shape[0], bp) + state_s5_re.shape[2:], dt)
    z_gla = jnp.zeros((state_gla.shape[0], bp) + state_gla.shape[2:], dt)
    z_gdn = jnp.zeros((state_gdn.shape[0], bp) + state_gdn.shape[2:], dt)
    z_conv = jnp.zeros((state_gdn_conv.shape[0], bp) + state_gdn_conv.shape[2:], dt)
    out_p = _trunk(x_prompt, z_s5, z_s5, z_gla, z_gdn, z_conv, w)
    out_s = _trunk(x_sample, state_s5_re, state_s5_im, state_gla, state_gdn, state_gdn_conv, w)
    return (out_p[0], out_s[0]) + out_p[1:] + out_s[1:]
```

```python
import functools
import math

import numpy as np
import jax
import jax.numpy as jnp
from jax import lax
from jax.experimental import pallas as pl
from jax.experimental.pallas import tpu as pltpu

F32 = jnp.float32
BF16 = jnp.bfloat16
SDS = jax.ShapeDtypeStruct

D_MODEL = 1024
NORM_EPS = 1e-6

S5_GROUP = 16
S5_STATE = 64
S5_GROUPS = D_MODEL // S5_GROUP
S5_GB = 16
S5_NGB = S5_GROUPS // S5_GB
S5_BC = S5_GB * S5_GROUP
S5_BS = S5_GB * S5_STATE

GLA_HEADS = 4
GLA_DK = 128
GLA_DV = 256
GLA_KEY = GLA_HEADS * GLA_DK
GLA_VAL = GLA_HEADS * GLA_DV
GLA_RANK = 16
GLA_TAU = 16.0
GLA_TN = 512

GDN_DK = 128
GDN_DV = 128
GDN_QK_HEADS = 8
GDN_V_HEADS = 16
GDN_KEY = GDN_QK_HEADS * GDN_DK
GDN_VAL = GDN_V_HEADS * GDN_DV
GDN_HALF = GDN_V_HEADS // 2
GDN_CONV = 4
GDN_CONV_DIM = 2 * GDN_KEY + GDN_VAL
GDN_TN = 1024
GDN_PROJ_TN = 1280

CHUNK = 128
LANES = 128
SUBLANES = 8
TM = 1024
FFN_TF = 1024
MIB = 1024 * 1024


def _cparams(n_axes, vmem_mib=48):
    return pltpu.CompilerParams(dimension_semantics=("arbitrary",) * n_axes,
                                vmem_limit_bytes=vmem_mib * MIB)


def _rms(x, gain):
    ms = jnp.mean(x * x, axis=-1, keepdims=True)
    return x * lax.rsqrt(ms + NORM_EPS) * gain


def _sigmoid(x):
    return 1.0 / (1.0 + jnp.exp(-x))


def _softplus(x):
    return jnp.maximum(x, 0.0) + jnp.log1p(jnp.exp(-jnp.abs(x)))


def _gelu_tanh(x):
    c = math.sqrt(2.0 / math.pi)
    return x * (0.5 * (1.0 + jnp.tanh(c * (x + 0.044715 * (x * x * x)))))


def _dot(a, b):
    return jnp.dot(a, b, preferred_element_type=F32)


def _dot_nt(a, b):
    return lax.dot_general(a, b, (((1,), (1,)), ((), ())), preferred_element_type=F32)


def _dot_tn(a, b):
    return lax.dot_general(a, b, (((0,), (0,)), ((), ())), preferred_element_type=F32)


def _eye(n):
    return (lax.broadcasted_iota(jnp.int32, (n, n), 0)
            == lax.broadcasted_iota(jnp.int32, (n, n), 1)).astype(F32)


def _split3(x):
    x1 = x.astype(BF16)
    r1 = x - x1.astype(F32)
    x2 = r1.astype(BF16)
    x3 = (r1 - x2.astype(F32)).astype(BF16)
    return x1, x2, x3


def _dot_exact01(m01, x):
    mb = m01.astype(BF16)
    x1, x2, x3 = _split3(x)
    return _dot(mb, x1) + _dot(mb, x2) + _dot(mb, x3)


def _dot_nt_exact01(m01, x):
    mb = m01.astype(BF16)
    x1, x2, x3 = _split3(x)
    return _dot_nt(mb, x1) + _dot_nt(mb, x2) + _dot_nt(mb, x3)


def _rmsnorm_kernel(h_ref, g_ref, o_ref):
    o_ref[...] = _rms(h_ref[...], g_ref[...])


def rmsnorm_rows(h, gain):
    rows, d = h.shape
    return pl.pallas_call(
        _rmsnorm_kernel, grid=(rows // TM,),
        in_specs=[pl.BlockSpec((TM, d), lambda i: (i, 0)), pl.BlockSpec((1, d), lambda i: (0, 0))],
        out_specs=pl.BlockSpec((TM, d), lambda i: (i, 0)),
        out_shape=SDS((rows, d), F32), compiler_params=_cparams(1), name="rmsnorm",
    )(h, gain.reshape(1, d))


def _norm_in_kernel(x_ref, g_ref, h_ref, hn_ref):
    for j in range(SUBLANES):
        x = x_ref[j]
        h_ref[:, j, :] = x
        hn_ref[:, j, :] = _rms(x, g_ref[...])


def norm_in(x, gain):
    nb, seq, d = x.shape
    tt = min(seq, TM // SUBLANES)
    nblk = seq // tt
    out = SDS((nb // SUBLANES * seq, SUBLANES, d), F32)
    h, hn = pl.pallas_call(
        _norm_in_kernel, grid=(nb // SUBLANES, nblk),
        in_specs=[pl.BlockSpec((SUBLANES, tt, d), lambda g, i: (g, i, 0)),
                  pl.BlockSpec((1, d), lambda g, i: (0, 0))],
        out_specs=[pl.BlockSpec((tt, SUBLANES, d), lambda g, i: (g * nblk + i, 0, 0))] * 2,
        out_shape=(out, out), compiler_params=_cparams(2), name="norm_in",
    )(x, gain.reshape(1, d))
    return h.reshape(seq * nb, d), hn.reshape(seq * nb, d)


def _norm_out_kernel(h_ref, g_ref, y_ref):
    for j in range(SUBLANES):
        y_ref[j] = _rms(h_ref[:, j, :], g_ref[...])


def norm_out(h, gain, nb, seq):
    d = h.shape[1]
    tt = min(seq, TM // SUBLANES)
    nblk = seq // tt
    return pl.pallas_call(
        _norm_out_kernel, grid=(nb // SUBLANES, nblk),
        in_specs=[pl.BlockSpec((tt, SUBLANES, d), lambda g, i: (g * nblk + i, 0, 0)),
                  pl.BlockSpec((1, d), lambda g, i: (0, 0))],
        out_specs=pl.BlockSpec((SUBLANES, tt, d), lambda g, i: (g, i, 0)),
        out_shape=SDS((nb, seq, d), F32), compiler_params=_cparams(2), name="norm_out",
    )(h.reshape(nb // SUBLANES * seq, SUBLANES, d), gain.reshape(1, d))


def _norm_matmul_kernel(h_ref, hnext_ref, g_ref, w_ref, o_ref, hn_ref):
    i = pl.program_id(0)
    j = pl.program_id(1)
    last = pl.num_programs(1) - 1
    slot = i % 2

    @pl.when((i == 0) & (j == 0))
    def _():
        hn_ref[0] = _rms(h_ref[...], g_ref[...]).astype(BF16)

    def tile():
        res = _dot(hn_ref[slot], w_ref[...])
        for t in range(o_ref.shape[0]):
            o_ref[t] = res[:, t * LANES:(t + 1) * LANES]

    @pl.when(j < last)
    def _():
        tile()

    @pl.when(j == last)
    def _():
        hn_ref[1 - slot] = _rms(hnext_ref[...], g_ref[...]).astype(BF16)
        tile()


def norm_matmul(h, gain, w, tn):
    rows, d = h.shape
    n = w.shape[1]
    nblk = rows // TM
    return pl.pallas_call(
        _norm_matmul_kernel, grid=(nblk, n // tn),
        in_specs=[pl.BlockSpec((TM, d), lambda i, j: (i, 0)),
                  pl.BlockSpec((TM, d), lambda i, j: (jnp.minimum(i + 1, nblk - 1), 0)),
                  pl.BlockSpec((1, d), lambda i, j: (0, 0)),
                  pl.BlockSpec((d, tn), lambda i, j: (0, j))],
        out_specs=pl.BlockSpec((tn // LANES, TM, LANES), lambda i, j: (j, i, 0)),
        out_shape=SDS((n // LANES, rows, LANES), F32),
        scratch_shapes=[pltpu.VMEM((2, TM, d), BF16)],
        compiler_params=_cparams(2), name="norm_matmul",
    )(h, h, gain.reshape(1, d), w)


def _ffn_kernel(h_ref, hnext_ref, g_ref, wu_ref, wd_ref, o_ref, hn_ref, acc_ref):
    i = pl.program_id(0)
    j = pl.program_id(1)
    last = pl.num_programs(1) - 1
    slot = i % 2

    @pl.when((i == 0) & (j == 0))
    def _():
        hn_ref[0] = _rms(h_ref[...], g_ref[...]).astype(BF16)

    def partial():
        a = jnp.square(jnp.maximum(_dot(hn_ref[slot], wu_ref[...]), 0.0)).astype(BF16)
        return _dot(a, wd_ref[...])

    @pl.when(j == 0)
    def _():
        acc_ref[...] = partial()

    @pl.when((j > 0) & (j < last))
    def _():
        acc_ref[...] += partial()

    @pl.when(j == last)
    def _():
        hn_ref[1 - slot] = _rms(hnext_ref[...], g_ref[...]).astype(BF16)
        o_ref[...] = h_ref[...] + acc_ref[...] + partial()


def ffn(h, gain, w_up, w_down):
    rows, d = h.shape
    f = w_up.shape[1]
    nblk = rows // TM
    assert f // FFN_TF >= 2
    return pl.pallas_call(
        _ffn_kernel, grid=(nblk, f // FFN_TF),
        in_specs=[pl.BlockSpec((TM, d), lambda i, j: (i, 0)),
                  pl.BlockSpec((TM, d), lambda i, j: (jnp.minimum(i + 1, nblk - 1), 0)),
                  pl.BlockSpec((1, d), lambda i, j: (0, 0)),
                  pl.BlockSpec((d, FFN_TF), lambda i, j: (0, j)),
                  pl.BlockSpec((FFN_TF, d), lambda i, j: (j, 0))],
        out_specs=pl.BlockSpec((TM, d), lambda i, j: (i, 0)),
        out_shape=SDS((rows, d), F32),
        scratch_shapes=[pltpu.VMEM((2, TM, d), BF16), pltpu.VMEM((TM, d), F32)],
        compiler_params=_cparams(2), name="ffn",
    )(h, h, gain.reshape(1, d), w_up, w_down)


def _gated_out_kernel(o_ref, z_ref, gn_ref, w_ref, h_ref, out_ref, *, nheads, hd):
    tph = hd // LANES
    kstep = 2 * LANES
    acc = h_ref[...]
    for k0 in range(0, nheads * hd, kstep):
        tiles = range(k0 // LANES, (k0 + kstep) // LANES)
        o = jnp.concatenate([o_ref[t] for t in tiles], axis=1)
        z = jnp.concatenate([z_ref[t] for t in tiles], axis=1)
        on = jnp.concatenate([_rms(o[:, i * hd:(i + 1) * hd], gn_ref[...]) for i in range(kstep // hd)], axis=1)
        a = (on * (z * _sigmoid(z))).astype(BF16)
        acc = acc + _dot(a, w_ref[k0:k0 + kstep, :])
    out_ref[...] = acc


def gated_out(o_tiles, p_tiles, z_block, gain, w_out, h, nheads, hd, tm):
    rows, d = h.shape
    kdim = nheads * hd
    nt = kdim // LANES
    return pl.pallas_call(
        functools.partial(_gated_out_kernel, nheads=nheads, hd=hd), grid=(rows // tm,),
        in_specs=[pl.BlockSpec((nt, tm, LANES), lambda i: (0, i, 0)),
                  pl.BlockSpec((nt, tm, LANES), lambda i: (z_block, i, 0)),
                  pl.BlockSpec((1, hd), lambda i: (0, 0)),
                  pl.BlockSpec((kdim, d), lambda i: (0, 0)),
                  pl.BlockSpec((tm, d), lambda i: (i, 0))],
        out_specs=pl.BlockSpec((tm, d), lambda i: (i, 0)),
        out_shape=SDS((rows, d), F32),
        compiler_params=_cparams(1), name="gated_out",
    )(o_tiles, p_tiles, gain.reshape(1, hd), w_out, h)


def _s5_discretize_kernel(are_ref, aim_ref, ldt_ref, bre_ref, bim_ref,
                          abr_ref, abi_ref, bbr_ref, bbi_ref):
    a_re, a_im = are_ref[...], aim_ref[...]
    dt = jnp.exp(ldt_ref[...])
    mag = jnp.exp(a_re * dt)
    ab_re = mag * jnp.cos(a_im * dt)
    ab_im = mag * jnp.sin(a_im * dt)
    den = a_re * a_re + a_im * a_im
    c_re = ((ab_re - 1.0) * a_re + ab_im * a_im) / den
    c_im = (ab_im * a_re - (ab_re - 1.0) * a_im) / den
    abr_ref[...] = ab_re
    abi_ref[...] = ab_im
    bbr_ref[...] = c_re * bre_ref[...] - c_im * bim_ref[...]
    bbi_ref[...] = c_re * bim_ref[...] + c_im * bre_ref[...]


def s5_discretize(a_re, a_im, log_dt, b_re, b_im):
    g, p, c = S5_GROUPS, S5_STATE, S5_GROUP
    expand = lambda v: jnp.broadcast_to(v[..., None], (g, p, c)).reshape(g, p * c)
    ldt = jnp.broadcast_to(log_dt[:, None], (g, p * c))
    shp = SDS((g, p * c), F32)
    ab_re, ab_im, bb_re, bb_im = pl.pallas_call(
        _s5_discretize_kernel, out_shape=(shp, shp, shp, shp), name="s5_discretize",
    )(expand(a_re), expand(a_im), ldt, b_re.reshape(g, p * c), b_im.reshape(g, p * c))
    ab_re = ab_re.reshape(g, p, c)[:, :, 0].reshape(S5_NGB, 1, S5_BS)
    ab_im = ab_im.reshape(g, p, c)[:, :, 0].reshape(S5_NGB, 1, S5_BS)
    eye = jnp.eye(S5_GB, dtype=F32)

    def block_diag_in(bb):
        bb = bb.reshape(S5_NGB, S5_GB, p, c)
        return jnp.einsum("bgpc,gh->bgchp", bb, eye).reshape(S5_NGB, S5_BC, S5_BS)

    b_blk = jnp.concatenate([block_diag_in(bb_re.reshape(g, p, c)),
                             block_diag_in(bb_im.reshape(g, p, c))], axis=-1).astype(BF16)
    return ab_re, ab_im, b_blk


def s5_block_diag_out(c_par):
    eye = jnp.eye(S5_GB, dtype=F32)
    cc = c_par.reshape(S5_NGB, S5_GB, S5_GROUP, S5_STATE)
    return jnp.einsum("bgcp,gh->bgphc", cc, eye).reshape(S5_NGB, S5_BS, S5_BC).astype(BF16)


def _s5_scan_kernel(u_ref, b_ref, cre_ref, cim_ref, are_ref, aim_ref, d_ref, s0r_ref, s0i_ref,
                    g_ref, slr_ref, sli_ref, xr_ref, xi_ref, str_ref, sti_ref, *, ngrp, tc):
    n = pl.program_id(1)

    @pl.when(n == 0)
    def _():
        str_ref[...] = s0r_ref[...]
        sti_ref[...] = s0i_ref[...]

    sub = SUBLANES
    a_re = jnp.broadcast_to(are_ref[0], (sub, S5_BS))
    a_im = jnp.broadcast_to(aim_ref[0], (sub, S5_BS))
    sb = 2 * LANES
    x_re = x_im = None
    for k in range(u_ref.shape[0] // sb):
        rows = slice(k * sb, (k + 1) * sb)
        u = u_ref[rows, :]
        ub = u.astype(BF16)
        xr_ref[rows, :] = _dot(ub, b_ref[0, :, :S5_BS])
        xi_ref[rows, :] = _dot(ub, b_ref[0, :, S5_BS:])
        for slab in range(k * sb // sub, (k + 1) * sb // sub):
            grp, t = divmod(slab, tc)
            srows = slice(grp * sub, (grp + 1) * sub)
            if t == 0:
                x_re, x_im = str_ref[srows, :], sti_ref[srows, :]
            r8 = slice(slab * sub, (slab + 1) * sub)
            x_re, x_im = (a_re * x_re - a_im * x_im + xr_ref[r8, :],
                          a_re * x_im + a_im * x_re + xi_ref[r8, :])
            xr_ref[r8, :] = x_re
            xi_ref[r8, :] = x_im
            if t == tc - 1:
                str_ref[srows, :] = x_re
                sti_ref[srows, :] = x_im
        y = _dot(xr_ref[rows, :].astype(BF16), cre_ref[0]) - _dot(xi_ref[rows, :].astype(BF16), cim_ref[0])
        g_ref[rows, :] = _gelu_tanh(y + d_ref[...] * u).astype(BF16)

    @pl.when(n == pl.num_programs(1) - 1)
    def _():
        slr_ref[...] = str_ref[...]
        sli_ref[...] = sti_ref[...]


def s5_scan(hn, nb, b_blk, c_re_blk, c_im_blk, ab_re, ab_im, d_skip, s0_re, s0_im):
    rows, d = hn.shape
    tc = min(rows // nb, TM // SUBLANES)
    ngrp = TM // (tc * SUBLANES)
    assert ngrp == 1 or ngrp * SUBLANES == nb
    st = SDS((nb, S5_GROUPS * S5_STATE), F32)
    return pl.pallas_call(
        functools.partial(_s5_scan_kernel, ngrp=ngrp, tc=tc), grid=(S5_NGB, rows // TM),
        in_specs=[pl.BlockSpec((TM, S5_BC), lambda gb, n: (n, gb)),
                  pl.BlockSpec((1, S5_BC, 2 * S5_BS), lambda gb, n: (gb, 0, 0)),
                  pl.BlockSpec((1, S5_BS, S5_BC), lambda gb, n: (gb, 0, 0)),
                  pl.BlockSpec((1, S5_BS, S5_BC), lambda gb, n: (gb, 0, 0)),
                  pl.BlockSpec((1, 1, S5_BS), lambda gb, n: (gb, 0, 0)),
                  pl.BlockSpec((1, 1, S5_BS), lambda gb, n: (gb, 0, 0)),
                  pl.BlockSpec((1, S5_BC), lambda gb, n: (0, gb)),
                  pl.BlockSpec((nb, S5_BS), lambda gb, n: (0, gb)),
                  pl.BlockSpec((nb, S5_BS), lambda gb, n: (0, gb))],
        out_specs=[pl.BlockSpec((TM, S5_BC), lambda gb, n: (n, gb)),
                   pl.BlockSpec((nb, S5_BS), lambda gb, n: (0, gb)),
                   pl.BlockSpec((nb, S5_BS), lambda gb, n: (0, gb))],
        out_shape=(SDS((rows, d), BF16), st, st),
        scratch_shapes=[pltpu.VMEM((TM, S5_BS), F32), pltpu.VMEM((TM, S5_BS), F32),
                        pltpu.VMEM((nb, S5_BS), F32), pltpu.VMEM((nb, S5_BS), F32)],
        compiler_params=_cparams(2), name="s5_scan",
    )(hn, b_blk, c_re_blk, c_im_blk, ab_re, ab_im, d_skip.reshape(1, d), s0_re, s0_im)


def _glu_out_kernel(g_ref, w_ref, h_ref, o_ref):
    gv = _dot(g_ref[...], w_ref[...])
    o_ref[...] = h_ref[...] + gv[:, :D_MODEL] * _sigmoid(gv[:, D_MODEL:])


def glu_out(g, w_glu, h):
    rows, d = h.shape
    tm = TM // 2
    return pl.pallas_call(
        _glu_out_kernel, grid=(rows // tm,),
        in_specs=[pl.BlockSpec((tm, d), lambda i: (i, 0)),
                  pl.BlockSpec((d, 2 * d), lambda i: (0, 0)),
                  pl.BlockSpec((tm, d), lambda i: (i, 0))],
        out_specs=pl.BlockSpec((tm, d), lambda i: (i, 0)),
        out_shape=SDS((rows, d), F32), compiler_params=_cparams(1), name="glu_out",
    )(g, w_glu, h)


def s5_layer(h, hn, nb, s0_re, s0_im, a_re, a_im, log_dt, b_re, b_im, c_re, c_im, d_skip, w_glu):
    ab_re, ab_im, b_blk = s5_discretize(a_re, a_im, log_dt, b_re, b_im)
    g, sl_re, sl_im = s5_scan(hn, nb, b_blk, s5_block_diag_out(c_re), s5_block_diag_out(c_im),
                              ab_re, ab_im, d_skip, s0_re.reshape(nb, -1), s0_im.reshape(nb, -1))
    shape = (nb, S5_GROUPS, S5_STATE)
    return glu_out(g, w_glu.astype(BF16), h), sl_re.reshape(shape), sl_im.reshape(shape)


class _Problem:
    def __init__(self, c, pack):
        p = c * pack
        r = np.arange(p)
        seq, time = r % pack, r // pack
        same = seq[:, None] == seq[None, :]
        self.p, self.pack = p, pack
        self.tri = same & (time[None, :] <= time[:, None])
        self.whole = same
        masks, pivots = [r[:, None] == r[None, :]], []
        for sz in [2 ** i for i in range(1, int(math.log2(p)) + 1)]:
            blk, off = r // sz, r % sz
            m = same & (blk[:, None] == blk[None, :]) & (off[:, None] >= sz // 2) & (off[None, :] < sz // 2)
            if not m.any():
                continue
            lower = same & (blk[:, None] == blk[None, :]) & (off[None, :] < sz // 2)
            piv = np.where(lower.any(1), (lower * r[None, :]).max(1), r)
            masks.append(m)
            pivots.append(self.tri[piv])
        self.masks = np.stack(masks)
        self.pivots = pivots
        self.seq_lanes = np.repeat(seq[:, None] == np.arange(pack)[None, :], LANES, axis=1)

    def f32(self, x):
        return jnp.asarray(np.asarray(x, np.float32))


def _gla_gate_kernel(gl_ref, w_ref, b_ref, o_ref):
    x = _dot(gl_ref[0].astype(BF16), w_ref[...]) + b_ref[...]
    g = -_softplus(-x) * (1.0 / GLA_TAU)
    for t in range(o_ref.shape[0]):
        o_ref[t] = g[:, t * LANES:(t + 1) * LANES]


def gla_gate(gl_tiles, w_gate_pad, b_gate):
    rows = gl_tiles.shape[1]
    nt = GLA_KEY // LANES
    return pl.pallas_call(
        _gla_gate_kernel, grid=(rows // TM,),
        in_specs=[pl.BlockSpec((1, TM, LANES), lambda i: (0, i, 0)),
                  pl.BlockSpec((LANES, GLA_KEY), lambda i: (0, 0)),
                  pl.BlockSpec((1, GLA_KEY), lambda i: (0, 0))],
        out_specs=pl.BlockSpec((nt, TM, LANES), lambda i: (0, i, 0)),
        out_shape=SDS((nt, rows, LANES), F32), compiler_params=_cparams(1), name="gla_gate",
    )(gl_tiles, w_gate_pad, b_gate.reshape(1, GLA_KEY))


def _gla_chunk_kernel(q_ref, k_ref, v_ref, g_ref, wst_ref, mask_ref, seqm_ref, s0_ref, o_ref, s_ref, *, p, pack):
    @pl.when(pl.program_id(1) == 0)
    def _():
        s_ref[...] = s0_ref[...]

    wst = wst_ref[...]
    eye = _eye(GLA_DK)
    nlev = mask_ref.shape[0] - 1
    tpv = GLA_DV // LANES

    def problem(rows, seqs):
        loaded = [(q_ref[hh, rows, :], k_ref[hh, rows, :],
                   [v_ref[hh * tpv + t, rows, :] for t in range(tpv)],
                   g_ref[hh, rows, :], [s_ref[j, hh] for j in seqs]) for hh in range(GLA_HEADS)]
        results = []
        work = []
        for q, k, v, g, ss in loaded:
            q = q * (GLA_DK ** -0.5)
            bp = _dot_exact01(wst, g)
            work.append(dict(q=q, k=k, v=v, ss=ss, bp=bp, b=bp[0:p],
                             att=mask_ref[0] * _dot_nt(q.astype(BF16), k.astype(BF16))))
        for lv in range(1, nlev + 1):
            for wk in work:
                e = jnp.exp(-jnp.abs(wk["b"] - wk["bp"][lv * p:(lv + 1) * p]))
                wk["att"] = wk["att"] + mask_ref[lv] * _dot_nt((wk["q"] * e).astype(BF16),
                                                               (wk["k"] * e).astype(BF16))
        for wk in work:
            q, k, v, ss, bp, b, att = (wk[n] for n in ("q", "k", "v", "ss", "bp", "b", "att"))
            vb = jnp.concatenate(v, axis=1).astype(BF16)
            b_end = bp[(nlev + 1) * p:(nlev + 2) * p]
            s_all = jnp.concatenate(ss, axis=0)
            q_dec = q * jnp.exp(b)
            k_dec = k * jnp.exp(b_end - b)
            if pack > 1:
                q_dec = jnp.concatenate([q_dec] * pack, axis=1) * seqm_ref[...]
                k_dec = jnp.concatenate([k_dec] * pack, axis=1) * seqm_ref[...]
            o = _dot(q_dec.astype(BF16), s_all.astype(BF16)) + _dot(att.astype(BF16), vb)
            last = jnp.concatenate([b_end[0:pack]] * (GLA_DK // pack), axis=0) if pack > 1 else (
                jnp.broadcast_to(b_end[0:1], (GLA_DK, GLA_DK)))
            col = jnp.exp(_dot_nt_exact01(eye, last))
            dec = jnp.concatenate([jnp.broadcast_to(col[:, i:i + 1], (GLA_DK, GLA_DV)) for i in range(pack)], axis=0)
            results.append((o, dec * s_all + _dot_tn(k_dec.astype(BF16), vb)))
        for hh, (o, s_new) in enumerate(results):
            for t in range(tpv):
                o_ref[hh * tpv + t, rows, :] = o[:, t * LANES:(t + 1) * LANES]
            for i, j in enumerate(seqs):
                s_ref[j, hh] = s_new[i * GLA_DK:(i + 1) * GLA_DK]

    if pack == 1:
        def per_sequence(j, carry):
            problem(pl.ds(j, p, stride=SUBLANES), [j])
            return carry

        lax.fori_loop(0, SUBLANES, per_sequence, 0)
    else:
        problem(slice(None), list(range(SUBLANES)))


def gla_chunks(p_tiles, g_tiles, s0, nb, seq, c, pack):
    rows = p_tiles.shape[1]
    pr = _Problem(c, pack)
    wst = pr.f32(np.concatenate([pr.tri] + pr.pivots + [pr.whole], axis=0))
    nkt = GLA_KEY // LANES
    nvt = GLA_VAL // LANES
    nc = seq // c
    tiles = lambda nt, blk: pl.BlockSpec((nt, c * SUBLANES, LANES), lambda b, n: (blk, b * nc + n, 0))
    const = lambda x: pl.BlockSpec(x.shape, lambda b, n: (0,) * x.ndim)
    state = pl.BlockSpec((SUBLANES, GLA_HEADS, GLA_DK, GLA_DV), lambda b, n: (b, 0, 0, 0))
    masks, seqm = pr.f32(pr.masks), pr.f32(pr.seq_lanes)
    o, s_out = pl.pallas_call(
        functools.partial(_gla_chunk_kernel, p=pr.p, pack=pack), grid=(nb // SUBLANES, nc),
        in_specs=[tiles(nkt, 0), tiles(nkt, 1), tiles(nvt, 1), tiles(nkt, 0),
                  const(wst), const(masks), const(seqm), state],
        out_specs=[tiles(nvt, 0), state],
        out_shape=(SDS((nvt, rows, LANES), F32), SDS((nb, GLA_HEADS, GLA_DK, GLA_DV), F32)),
        compiler_params=_cparams(2), name="gla_chunks",
    )(p_tiles, p_tiles, p_tiles, g_tiles, wst, masks, seqm, s0)
    return o, s_out


def _chunking(seq):
    return (CHUNK, 1) if seq % CHUNK == 0 else (seq, SUBLANES)


def gla_layer(h, nb, seq, gain, s0, w_in, w_gate_up, b_gate, norm_o, w_out):
    c, pack = _chunking(seq)
    n_main = 2 * GLA_KEY + 2 * GLA_VAL
    p_tiles = norm_matmul(h, gain, w_in[:, :n_main].astype(BF16), GLA_TN)
    w_gl = jnp.pad(w_in[:, n_main:], ((0, 0), (0, LANES - GLA_RANK))).astype(BF16)
    gl_tiles = norm_matmul(h, gain, w_gl, LANES)
    w_gate_pad = jnp.pad(w_gate_up, ((0, LANES - GLA_RANK), (0, 0))).astype(BF16)
    g_tiles = gla_gate(gl_tiles, w_gate_pad, b_gate)
    o_tiles, s_out = gla_chunks(p_tiles, g_tiles, s0, nb, seq, c, pack)
    h = gated_out(o_tiles, p_tiles, 2, norm_o, w_out.astype(BF16), h, GLA_HEADS, GLA_DV, TM)
    return h, s_out


def _gdn_conv_kernel(x_ref, cw_ref, c0_ref, o_ref, nc_ref, xp_ref, *, tm, normalize):
    i = pl.program_id(2)
    nb = SUBLANES
    halo = (GDN_CONV - 1) * nb
    scale = jnp.where(pl.program_id(0) == 0, GDN_DK ** -0.5, 1.0)
    for t in range(x_ref.shape[0]):
        lanes = slice(t * LANES, (t + 1) * LANES)

        @pl.when(i == 0)
        def _():
            xp_ref[t, 0:halo, :] = c0_ref[:, lanes]

        xp_ref[t, halo:halo + tm, :] = x_ref[t]
        acc = cw_ref[0:1, lanes] * xp_ref[t, 0:tm, :]
        for j in range(1, GDN_CONV):
            acc = acc + cw_ref[j:j + 1, lanes] * xp_ref[t, j * nb:j * nb + tm, :]
        y = acc * _sigmoid(acc)
        if normalize:
            y = y * (lax.rsqrt(jnp.sum(y * y, axis=-1, keepdims=True) + NORM_EPS) * scale)
        o_ref[t] = y
        tail = xp_ref[t, tm:tm + halo, :]
        xp_ref[t, 0:halo, :] = tail

        @pl.when(i == pl.num_programs(2) - 1)
        def _():
            nc_ref[:, lanes] = tail


def gdn_conv(p_tiles, group0, conv_w, conv0, nb, normalize):
    rows = p_tiles.shape[1]
    ngrp = nb // SUBLANES
    halo = (GDN_CONV - 1) * SUBLANES
    tm = min(TM, rows // ngrp)
    nblk = rows // ngrp // tm
    gt = GDN_TN // LANES
    assert tm >= halo
    return pl.pallas_call(
        functools.partial(_gdn_conv_kernel, tm=tm, normalize=normalize), grid=(2, ngrp, nblk),
        in_specs=[pl.BlockSpec((gt, tm, LANES), lambda j, g, i: (group0 + j, g * nblk + i, 0)),
                  pl.BlockSpec((GDN_CONV, GDN_TN), lambda j, g, i: (0, group0 + j)),
                  pl.BlockSpec((halo, GDN_TN), lambda j, g, i: (g, group0 + j))],
        out_specs=[pl.BlockSpec((gt, tm, LANES), lambda j, g, i: (j, g * nblk + i, 0)),
                   pl.BlockSpec((halo, GDN_TN), lambda j, g, i: (g, j))],
        out_shape=(SDS((2 * gt, rows, LANES), F32), SDS((ngrp * halo, 2 * GDN_TN), F32)),
        scratch_shapes=[pltpu.VMEM((gt, halo + tm, LANES), F32)],
        compiler_params=_cparams(3), name="gdn_conv",
    )(p_tiles, conv_w, conv0)


def _gdn_chunk_kernel(q_ref, k_ref, v_ref, ba_ref, alog_ref, dtb_ref, tril_ref, mask_ref, maskb_ref, seqm_ref,
                      s0_ref, o_ref, s_ref, *, p, pack):
    @pl.when(pl.program_id(2) == 0)
    def _():
        s_ref[...] = s0_ref[...]

    tri = tril_ref[0:p, :]
    eye_l = _eye(LANES)
    eye_c = mask_ref[0]
    strict = tri - eye_c
    nlv = mask_ref.shape[0]
    rep = GDN_V_HEADS // GDN_QK_HEADS

    def problem(rows, seqs):
        ba = ba_ref[0, rows, :]
        beta_all = _sigmoid(ba)
        g_all = -jnp.exp(alog_ref[0]) * _softplus(ba + dtb_ref[0])
        gcl = _dot_exact01(tril_ref[...], g_all)
        gc_all, ge_all = gcl[0:p], gcl[p:2 * p]
        gr_all = _dot_nt_exact01(eye_l, gc_all)
        heads = []
        for qh in range(GDN_HALF // rep):
            q = q_ref[qh, rows, :]
            k = k_ref[qh, rows, :]
            kb = k.astype(BF16)
            kq = _dot_nt(jnp.concatenate([kb, q.astype(BF16)], axis=0), kb)
            kk, qk = kq[:p], kq[p:]
            for j in range(rep):
                hh = qh * rep + j
                beta = beta_all[:, hh:hh + 1]
                g_col = gc_all[:, GDN_HALF + hh:GDN_HALF + hh + 1]
                g_end = ge_all[:, GDN_HALF + hh:GDN_HALF + hh + 1]
                g_row = gr_all[GDN_HALF + hh:GDN_HALF + hh + 1, :]
                decay = tri * jnp.exp(jnp.minimum(g_col - g_row, 0.0))
                m = strict * (kk * decay * beta)
                heads.append(dict(hh=hh, q=q, k=k, v=v_ref[hh, rows, :], beta=beta, g_col=g_col, g_end=g_end,
                                  s=jnp.concatenate([s_ref[i, hh] for i in seqs], axis=0),
                                  mb=m.astype(BF16), qkd=(qk * decay).astype(BF16),
                                  t=eye_c - mask_ref[1] * m))
        for lv in range(2, nlv):
            for hd in heads:
                tb = hd["t"].astype(BF16)
                hd["t"] = hd["t"] - _dot(_dot(tb, maskb_ref[lv] * hd["mb"]).astype(BF16), tb)
        for hd in heads:
            e_g = jnp.exp(hd["g_col"])
            rhs = jnp.concatenate([hd["v"] * hd["beta"], hd["k"] * (hd["beta"] * e_g)], axis=1).astype(BF16)
            hd["uw"] = _dot(hd["t"].astype(BF16), rhs)
            hd["q_dec"] = hd["q"] * e_g
        for hd in heads:
            uw = hd["uw"]
            lhs = jnp.concatenate([uw[:, GDN_DV:], hd["q_dec"]], axis=0)
            if pack > 1:
                lhs = jnp.concatenate([lhs] * pack, axis=1) * jnp.concatenate([seqm_ref[...]] * 2, axis=0)
            ws = _dot(lhs.astype(BF16), hd["s"].astype(BF16))
            hd["vnb"] = (uw[:, :GDN_DV] - ws[:p]).astype(BF16)
            hd["o_inter"] = ws[p:]
        for hd in heads:
            g_end = hd["g_end"]
            k_dec = hd["k"] * jnp.exp(g_end - hd["g_col"])
            if pack > 1:
                k_dec = jnp.concatenate([k_dec] * pack, axis=1) * seqm_ref[...]
            hd["o"] = hd["o_inter"] + _dot(hd["qkd"], hd["vnb"])
            dec = jnp.concatenate([jnp.broadcast_to(jnp.exp(g_end[i:i + 1, :]), (GDN_DK, GDN_DV))
                                   for i in range(pack)], axis=0)
            hd["s_new"] = dec * hd["s"] + _dot_tn(k_dec.astype(BF16), hd["vnb"])
        for hd in heads:
            o_ref[hd["hh"], rows, :] = hd["o"]
            for i, j in enumerate(seqs):
                s_ref[j, hd["hh"]] = hd["s_new"][i * GDN_DK:(i + 1) * GDN_DK]

    if pack == 1:
        def per_sequence(b, carry):
            problem(pl.ds(b, p, stride=SUBLANES), [b])
            return carry

        lax.fori_loop(0, SUBLANES, per_sequence, 0)
    else:
        problem(slice(None), list(range(SUBLANES)))


def gdn_chunks(qk_tiles, v_tiles, p_tiles, a_log, dt_bias, s0, nb, seq, c, pack):
    ba0 = (GDN_CONV_DIM + GDN_VAL) // LANES
    rows = qk_tiles.shape[1]
    nqk = GDN_HALF // (GDN_V_HEADS // GDN_QK_HEADS)
    nc = seq // c
    pr = _Problem(c, pack)
    pad = lambda x: jnp.pad(x.reshape(2, 1, GDN_HALF), ((0, 0), (0, 0), (GDN_HALF, LANES - 2 * GDN_HALF)))
    masks, seqm = pr.f32(pr.masks), pr.f32(pr.seq_lanes)
    tril = pr.f32(np.concatenate([pr.tri, pr.whole], axis=0))
    tiles = lambda nt, blk: pl.BlockSpec((nt, c * SUBLANES, LANES), lambda b, hf, n: (blk(hf), b * nc + n, 0))
    const = lambda x: pl.BlockSpec(x.shape, lambda b, hf, n: (0,) * x.ndim)
    state = pl.BlockSpec((SUBLANES, GDN_HALF, GDN_DK, GDN_DV), lambda b, hf, n: (b, hf, 0, 0))
    o, s_out = pl.pallas_call(
        functools.partial(_gdn_chunk_kernel, p=pr.p, pack=pack), grid=(nb // SUBLANES, 2, nc),
        in_specs=[tiles(nqk, lambda hf: hf), tiles(nqk, lambda hf: 2 + hf),
                  tiles(GDN_HALF, lambda hf: hf), tiles(1, lambda hf: ba0 + hf),
                  pl.BlockSpec((1, 1, LANES), lambda b, hf, n: (hf, 0, 0)),
                  pl.BlockSpec((1, 1, LANES), lambda b, hf, n: (hf, 0, 0)),
                  const(tril), const(masks), const(masks), const(seqm), state],
        out_specs=[tiles(GDN_HALF, lambda hf: hf), state],
        out_shape=(SDS((GDN_V_HEADS, rows, LANES), F32), SDS((nb, GDN_V_HEADS, GDN_DK, GDN_DV), F32)),
        compiler_params=_cparams(3), name="gdn_chunks",
    )(qk_tiles, qk_tiles, v_tiles, p_tiles, pad(a_log), pad(dt_bias), tril, masks, masks.astype(BF16), seqm, s0)
    return o, s_out


def gdn_layer(h, nb, seq, gain, s0, conv0, w_in, conv_w, a_log, dt_bias, norm_o, w_out):
    c, pack = _chunking(seq)
    n_main = GDN_CONV_DIM + GDN_VAL
    w_b = w_in[:, n_main:n_main + GDN_V_HEADS].reshape(D_MODEL, 2, GDN_HALF)
    w_a = w_in[:, n_main + GDN_V_HEADS:].reshape(D_MODEL, 2, GDN_HALF)
    w_ba = jnp.pad(jnp.concatenate([w_b, w_a], axis=2), ((0, 0), (0, 0), (0, LANES - 2 * GDN_HALF)))
    w_all = jnp.concatenate([w_in[:, :n_main], w_ba.reshape(D_MODEL, 2 * LANES)], axis=1).astype(BF16)
    p_tiles = norm_matmul(h, gain, w_all, GDN_PROJ_TN)
    ngrp = nb // SUBLANES
    conv0_tm = conv0.reshape(ngrp, SUBLANES, GDN_CONV - 1, GDN_CONV_DIM).transpose(0, 2, 1, 3)
    conv0_tm = conv0_tm.reshape(ngrp * (GDN_CONV - 1) * SUBLANES, GDN_CONV_DIM)
    qk_tiles, nc_qk = gdn_conv(p_tiles, 0, conv_w, conv0_tm, nb, True)
    v_tiles, nc_v = gdn_conv(p_tiles, 2, conv_w, conv0_tm, nb, False)
    o_tiles, s_out = gdn_chunks(qk_tiles, v_tiles, p_tiles, a_log, dt_bias, s0, nb, seq, c, pack)
    h = gated_out(o_tiles, p_tiles, 2, norm_o, w_out.astype(BF16), h, GDN_V_HEADS, GDN_DV, TM // 2)
    new_conv = jnp.concatenate([nc_qk, nc_v], axis=1).reshape(ngrp, GDN_CONV - 1, SUBLANES, GDN_CONV_DIM)
    return h, s_out, new_conv.transpose(0, 2, 1, 3).reshape(nb, GDN_CONV - 1, GDN_CONV_DIM)


def _trunk(x, s5_re, s5_im, gla_s, gdn_s, gdn_conv_s, w):
    nb, seq, d = x.shape
    h, hn = norm_in(x, w["norm_mix"][0])
    h, s5r0, s5i0 = s5_layer(h, hn, nb, s5_re[0], s5_im[0], w["s5_a_re"][0], w["s5_a_im"][0],
                             w["s5_log_dt"][0], w["s5_b_re"][0], w["s5_b_im"][0], w["s5_c_re"][0],
                             w["s5_c_im"][0], w["s5_d"][0], w["s5_w_glu"][0])
    h = ffn(h, w["norm_ffn"][0], w["w_up"][0], w["w_down"][0])
    h, gla_o = gla_layer(h, nb, seq, w["norm_mix"][1], gla_s[0], w["gla_w_in"][0], w["gla_w_gate_up"][0],
                         w["gla_b_gate"][0], w["gla_norm"][0], w["gla_w_out"][0])
    h = ffn(h, w["norm_ffn"][1], w["w_up"][1], w["w_down"][1])
    h, gdn_o, conv_o = gdn_layer(h, nb, seq, w["norm_mix"][2], gdn_s[0], gdn_conv_s[0], w["gdn_w_in"][0],
                                 w["gdn_conv_w"][0], w["gdn_a_log"][0], w["gdn_dt_bias"][0],
                                 w["gdn_norm"][0], w["gdn_w_out"][0])
    h = ffn(h, w["norm_ffn"][2], w["w_up"][2], w["w_down"][2])
    hn = rmsnorm_rows(h, w["norm_mix"][3])
    h, s5r1, s5i1 = s5_layer(h, hn, nb, s5_re[1], s5_im[1], w["s5_a_re"][1], w["s5_a_im"][1],
                             w["s5_log_dt"][1], w["s5_b_re"][1], w["s5_b_im"][1], w["s5_c_re"][1],
                             w["s5_c_im"][1], w["s5_d"][1], w["s5_w_glu"][1])
    h = ffn(h, w["norm_ffn"][3], w["w_up"][3], w["w_down"][3])
    y = norm_out(h, w["norm_final"], nb, seq)
    return (y, jnp.stack([s5r0, s5r1]), jnp.stack([s5i0, s5i1]), gla_o[None], gdn_o[None], conv_o[None])


def kernel(x_prompt, x_sample, state_s5_re, state_s5_im, state_gla, state_gdn, state_gdn_conv, norm_mix, norm_ffn, norm_final, w_up, w_down, s5_a_re, s5_a_im, s5_log_dt, s5_b_re, s5_b_im, s5_c_re, s5_c_im, s5_d, s5_w_glu, gla_w_in, gla_w_gate_up, gla_b_gate, gla_norm, gla_w_out, gdn_w_in, gdn_conv_w, gdn_a_log, gdn_dt_bias, gdn_norm, gdn_w_out):
    w = dict(norm_mix=norm_mix, norm_ffn=norm_ffn, norm_final=norm_final,
             w_up=w_up.astype(BF16), w_down=w_down.astype(BF16),
             s5_a_re=s5_a_re, s5_a_im=s5_a_im, s5_log_dt=s5_log_dt, s5_b_re=s5_b_re, s5_b_im=s5_b_im,
             s5_c_re=s5_c_re, s5_c_im=s5_c_im, s5_d=s5_d, s5_w_glu=s5_w_glu,
             gla_w_in=gla_w_in, gla_w_gate_up=gla_w_gate_up, gla_b_gate=gla_b_gate, gla_norm=gla_norm,
             gla_w_out=gla_w_out, gdn_w_in=gdn_w_in, gdn_conv_w=gdn_conv_w, gdn_a_log=gdn_a_log,
             gdn_dt_bias=gdn_dt_bias, gdn_norm=gdn_norm, gdn_w_out=gdn_w_out)
    bp = x_prompt.shape[0]
    dt = x_prompt.dtype
    z_s5 = jnp.zeros((state_s5_re.shape[0], bp) + state_s5_re.shape[2:], dt)
    z_gla = jnp.zeros((state_gla.shape[0], bp) + state_gla.shape[2:], dt)
    z_gdn = jnp.zeros((state_gdn.shape[0], bp) + state_gdn.shape[2:], dt)
    z_conv = jnp.zeros((state_gdn_conv.shape[0], bp) + state_gdn_conv.shape[2:], dt)
    out_p = _trunk(x_prompt, z_s5, z_s5, z_gla, z_gdn, z_conv, w)
    out_s = _trunk(x_sample, state_s5_re, state_s5_im, state_gla, state_gdn, state_gdn_conv, w)
    return (out_p[0], out_s[0]) + out_p[1:] + out_s[1:]
```

```python
import functools
import math

import numpy as np
import jax
import jax.numpy as jnp
from jax import lax
from jax.experimental import pallas as pl
from jax.experimental.pallas import tpu as pltpu

F32 = jnp.float32
BF16 = jnp.bfloat16
SDS = jax.ShapeDtypeStruct

D_MODEL = 1024
NORM_EPS = 1e-6

S5_GROUP = 16
S5_STATE = 64
S5_GROUPS = D_MODEL // S5_GROUP
S5_GB = 16
S5_NGB = S5_GROUPS // S5_GB
S5_BC = S5_GB * S5_GROUP
S5_BS = S5_GB * S5_STATE

GLA_HEADS = 4
GLA_DK = 128
GLA_DV = 256
GLA_KEY = GLA_HEADS * GLA_DK
GLA_VAL = GLA_HEADS * GLA_DV
GLA_RANK = 16
GLA_TAU = 16.0
GLA_TN = 1024

GDN_DK = 128
GDN_DV = 128
GDN_QK_HEADS = 8
GDN_V_HEADS = 16
GDN_KEY = GDN_QK_HEADS * GDN_DK
GDN_VAL = GDN_V_HEADS * GDN_DV
GDN_HALF = GDN_V_HEADS // 2
GDN_CONV = 4
GDN_CONV_DIM = 2 * GDN_KEY + GDN_VAL
GDN_TN = 1024
GDN_PROJ_TN = 1280

CHUNK = 128
LANES = 128
SUBLANES = 8
TM = 1024
FFN_TF = 1024
MIB = 1024 * 1024


def _cparams(n_axes, vmem_mib=48):
    return pltpu.CompilerParams(dimension_semantics=("arbitrary",) * n_axes,
                                vmem_limit_bytes=vmem_mib * MIB)


def _rms(x, gain):
    ms = jnp.mean(x * x, axis=-1, keepdims=True)
    return x * lax.rsqrt(ms + NORM_EPS) * gain


def _sigmoid(x):
    return 1.0 / (1.0 + jnp.exp(-x))


def _softplus(x):
    return jnp.maximum(x, 0.0) + jnp.log1p(jnp.exp(-jnp.abs(x)))


def _gelu_tanh(x):
    c = math.sqrt(2.0 / math.pi)
    return x * (0.5 * (1.0 + jnp.tanh(c * (x + 0.044715 * (x * x * x)))))


def _dot(a, b):
    return jnp.dot(a, b, preferred_element_type=F32)


def _dot_nt(a, b):
    return lax.dot_general(a, b, (((1,), (1,)), ((), ())), preferred_element_type=F32)


def _dot_tn(a, b):
    return lax.dot_general(a, b, (((0,), (0,)), ((), ())), preferred_element_type=F32)


def _eye(n):
    return (lax.broadcasted_iota(jnp.int32, (n, n), 0)
            == lax.broadcasted_iota(jnp.int32, (n, n), 1)).astype(F32)


def _split3(x):
    x1 = x.astype(BF16)
    r1 = x - x1.astype(F32)
    x2 = r1.astype(BF16)
    x3 = (r1 - x2.astype(F32)).astype(BF16)
    return x1, x2, x3


def _dot_exact01(m01, x):
    mb = m01.astype(BF16)
    x1, x2, x3 = _split3(x)
    return _dot(mb, x1) + _dot(mb, x2) + _dot(mb, x3)


def _dot_nt_exact01(m01, x):
    mb = m01.astype(BF16)
    x1, x2, x3 = _split3(x)
    return _dot_nt(mb, x1) + _dot_nt(mb, x2) + _dot_nt(mb, x3)


def _norm_in_kernel(x_ref, g_ref, h_ref, hn_ref):
    for j in range(SUBLANES):
        x = x_ref[j]
        h_ref[:, j, :] = x
        hn_ref[:, j, :] = _rms(x, g_ref[...])


def norm_in(x, gain):
    nb, seq, d = x.shape
    tt = min(seq, TM // SUBLANES)
    nblk = seq // tt
    out = SDS((nb // SUBLANES * seq, SUBLANES, d), F32)
    h, hn = pl.pallas_call(
        _norm_in_kernel, grid=(nb // SUBLANES, nblk),
        in_specs=[pl.BlockSpec((SUBLANES, tt, d), lambda g, i: (g, i, 0)),
                  pl.BlockSpec((1, d), lambda g, i: (0, 0))],
        out_specs=[pl.BlockSpec((tt, SUBLANES, d), lambda g, i: (g * nblk + i, 0, 0))] * 2,
        out_shape=(out, out), compiler_params=_cparams(2), name="norm_in",
    )(x, gain.reshape(1, d))
    return h.reshape(seq * nb, d), hn.reshape(seq * nb, d)


def _norm_out_kernel(h_ref, g_ref, y_ref):
    for j in range(SUBLANES):
        y_ref[j] = _rms(h_ref[:, j, :], g_ref[...])


def norm_out(h, gain, nb, seq):
    d = h.shape[1]
    tt = min(seq, TM // SUBLANES)
    nblk = seq // tt
    return pl.pallas_call(
        _norm_out_kernel, grid=(nb // SUBLANES, nblk),
        in_specs=[pl.BlockSpec((tt, SUBLANES, d), lambda g, i: (g * nblk + i, 0, 0)),
                  pl.BlockSpec((1, d), lambda g, i: (0, 0))],
        out_specs=pl.BlockSpec((SUBLANES, tt, d), lambda g, i: (g, i, 0)),
        out_shape=SDS((nb, seq, d), F32), compiler_params=_cparams(2), name="norm_out",
    )(h.reshape(nb // SUBLANES * seq, SUBLANES, d), gain.reshape(1, d))


def _norm_matmul_kernel(h_ref, g_ref, w_ref, o_ref, hn_ref):
    @pl.when(pl.program_id(1) == 0)
    def _():
        hn_ref[...] = _rms(h_ref[...], g_ref[...]).astype(BF16)

    res = _dot(hn_ref[...], w_ref[...])
    for t in range(o_ref.shape[0]):
        o_ref[t] = res[:, t * LANES:(t + 1) * LANES]


def norm_matmul(h, gain, w, tn):
    rows, d = h.shape
    n = w.shape[1]
    return pl.pallas_call(
        _norm_matmul_kernel, grid=(rows // TM, n // tn),
        in_specs=[pl.BlockSpec((TM, d), lambda i, j: (i, 0)),
                  pl.BlockSpec((1, d), lambda i, j: (0, 0)),
                  pl.BlockSpec((d, tn), lambda i, j: (0, j))],
        out_specs=pl.BlockSpec((tn // LANES, TM, LANES), lambda i, j: (j, i, 0)),
        out_shape=SDS((n // LANES, rows, LANES), F32),
        scratch_shapes=[pltpu.VMEM((TM, d), BF16)],
        compiler_params=_cparams(2), name="norm_matmul",
    )(h, gain.reshape(1, d), w)


def _ffn_kernel(h_ref, hnext_ref, g_ref, wu_ref, wd_ref, ng_ref, *rest, emit_norm):
    o_ref, hn_ref, acc_ref = rest[0], rest[-2], rest[-1]
    i = pl.program_id(0)
    j = pl.program_id(1)
    last = pl.num_programs(1) - 1
    slot = i % 2

    @pl.when((i == 0) & (j == 0))
    def _():
        hn_ref[0] = _rms(h_ref[...], g_ref[...]).astype(BF16)

    def partial():
        a = jnp.square(jnp.maximum(_dot(hn_ref[slot], wu_ref[...]), 0.0)).astype(BF16)
        return _dot(a, wd_ref[...])

    @pl.when(j == 0)
    def _():
        acc_ref[...] = partial()

    @pl.when((j > 0) & (j < last))
    def _():
        acc_ref[...] += partial()

    @pl.when(j == last)
    def _():
        hn_ref[1 - slot] = _rms(hnext_ref[...], g_ref[...]).astype(BF16)
        out = h_ref[...] + acc_ref[...] + partial()
        o_ref[...] = out
        if emit_norm:
            rest[1][...] = _rms(out, ng_ref[...])


def ffn(h, gain, w_up, w_down, next_gain=None):
    rows, d = h.shape
    f = w_up.shape[1]
    nblk = rows // TM
    assert f // FFN_TF >= 2
    emit_norm = next_gain is not None
    row_block = pl.BlockSpec((TM, d), lambda i, j: (i, 0))
    vec = pl.BlockSpec((1, d), lambda i, j: (0, 0))
    out = pl.pallas_call(
        functools.partial(_ffn_kernel, emit_norm=emit_norm), grid=(nblk, f // FFN_TF),
        in_specs=[row_block,
                  pl.BlockSpec((TM, d), lambda i, j: (jnp.minimum(i + 1, nblk - 1), 0)),
                  vec,
                  pl.BlockSpec((d, FFN_TF), lambda i, j: (0, j)),
                  pl.BlockSpec((FFN_TF, d), lambda i, j: (j, 0)),
                  vec],
        out_specs=[row_block] * (2 if emit_norm else 1),
        out_shape=[SDS((rows, d), F32)] * (2 if emit_norm else 1),
        scratch_shapes=[pltpu.VMEM((2, TM, d), BF16), pltpu.VMEM((TM, d), F32)],
        compiler_params=_cparams(2, vmem_mib=56 if emit_norm else 48), name="ffn",
    )(h, h, gain.reshape(1, d), w_up, w_down, (next_gain if emit_norm else gain).reshape(1, d))
    return (out[0], out[1]) if emit_norm else out[0]


def _gated_out_kernel(o_ref, z_ref, gn_ref, w_ref, h_ref, out_ref, *, nheads, hd):
    tph = hd // LANES
    kstep = 2 * LANES
    acc = h_ref[...]
    for k0 in range(0, nheads * hd, kstep):
        tiles = range(k0 // LANES, (k0 + kstep) // LANES)
        o = jnp.concatenate([o_ref[t] for t in tiles], axis=1)
        z = jnp.concatenate([z_ref[t] for t in tiles], axis=1)
        on = jnp.concatenate([_rms(o[:, i * hd:(i + 1) * hd], gn_ref[...]) for i in range(kstep // hd)], axis=1)
        a = (on * (z * _sigmoid(z))).astype(BF16)
        acc = acc + _dot(a, w_ref[k0:k0 + kstep, :])
    out_ref[...] = acc


def gated_out(o_tiles, p_tiles, z_block, gain, w_out, h, nheads, hd, tm):
    rows, d = h.shape
    kdim = nheads * hd
    nt = kdim // LANES
    return pl.pallas_call(
        functools.partial(_gated_out_kernel, nheads=nheads, hd=hd), grid=(rows // tm,),
        in_specs=[pl.BlockSpec((nt, tm, LANES), lambda i: (0, i, 0)),
                  pl.BlockSpec((nt, tm, LANES), lambda i: (z_block, i, 0)),
                  pl.BlockSpec((1, hd), lambda i: (0, 0)),
                  pl.BlockSpec((kdim, d), lambda i: (0, 0)),
                  pl.BlockSpec((tm, d), lambda i: (i, 0))],
        out_specs=pl.BlockSpec((tm, d), lambda i: (i, 0)),
        out_shape=SDS((rows, d), F32),
        compiler_params=_cparams(1), name="gated_out",
    )(o_tiles, p_tiles, gain.reshape(1, hd), w_out, h)


def _s5_discretize_kernel(are_ref, aim_ref, ldt_ref, bre_ref, bim_ref,
                          abr_ref, abi_ref, bbr_ref, bbi_ref):
    a_re, a_im = are_ref[...], aim_ref[...]
    dt = jnp.exp(ldt_ref[...])
    mag = jnp.exp(a_re * dt)
    ab_re = mag * jnp.cos(a_im * dt)
    ab_im = mag * jnp.sin(a_im * dt)
    den = a_re * a_re + a_im * a_im
    c_re = ((ab_re - 1.0) * a_re + ab_im * a_im) / den
    c_im = (ab_im * a_re - (ab_re - 1.0) * a_im) / den
    abr_ref[...] = ab_re
    abi_ref[...] = ab_im
    bbr_ref[...] = c_re * bre_ref[...] - c_im * bim_ref[...]
    bbi_ref[...] = c_re * bim_ref[...] + c_im * bre_ref[...]


def s5_discretize(a_re, a_im, log_dt, b_re, b_im):
    g, p, c = S5_GROUPS, S5_STATE, S5_GROUP
    expand = lambda v: jnp.broadcast_to(v[..., None], (g, p, c)).reshape(g, p * c)
    ldt = jnp.broadcast_to(log_dt[:, None], (g, p * c))
    shp = SDS((g, p * c), F32)
    ab_re, ab_im, bb_re, bb_im = pl.pallas_call(
        _s5_discretize_kernel, out_shape=(shp, shp, shp, shp), name="s5_discretize",
    )(expand(a_re), expand(a_im), ldt, b_re.reshape(g, p * c), b_im.reshape(g, p * c))
    ab_re = ab_re.reshape(g, p, c)[:, :, 0].reshape(S5_NGB, 1, S5_BS)
    ab_im = ab_im.reshape(g, p, c)[:, :, 0].reshape(S5_NGB, 1, S5_BS)
    eye = jnp.eye(S5_GB, dtype=F32)

    def block_diag_in(bb):
        bb = bb.reshape(S5_NGB, S5_GB, p, c)
        return jnp.einsum("bgpc,gh->bgchp", bb, eye).reshape(S5_NGB, S5_BC, S5_BS)

    b_blk = jnp.concatenate([block_diag_in(bb_re.reshape(g, p, c)),
                             block_diag_in(bb_im.reshape(g, p, c))], axis=-1).astype(BF16)
    return ab_re, ab_im, b_blk


def s5_block_diag_out(c_par):
    eye = jnp.eye(S5_GB, dtype=F32)
    cc = c_par.reshape(S5_NGB, S5_GB, S5_GROUP, S5_STATE)
    return jnp.einsum("bgcp,gh->bgphc", cc, eye).reshape(S5_NGB, S5_BS, S5_BC).astype(BF16)


def _s5_scan_kernel(u_ref, b_ref, cre_ref, cim_ref, are_ref, aim_ref, d_ref, s0r_ref, s0i_ref,
                    g_ref, slr_ref, sli_ref, xr_ref, xi_ref, str_ref, sti_ref, *, ngrp, tc):
    n = pl.program_id(1)

    @pl.when(n == 0)
    def _():
        str_ref[...] = s0r_ref[...]
        sti_ref[...] = s0i_ref[...]

    sub = SUBLANES
    a_re = jnp.broadcast_to(are_ref[0], (sub, S5_BS))
    a_im = jnp.broadcast_to(aim_ref[0], (sub, S5_BS))
    sb = 2 * LANES
    x_re = x_im = None
    for k in range(u_ref.shape[0] // sb):
        rows = slice(k * sb, (k + 1) * sb)
        u = u_ref[rows, :]
        ub = u.astype(BF16)
        xr_ref[rows, :] = _dot(ub, b_ref[0, :, :S5_BS])
        xi_ref[rows, :] = _dot(ub, b_ref[0, :, S5_BS:])
        for slab in range(k * sb // sub, (k + 1) * sb // sub):
            grp, t = divmod(slab, tc)
            srows = slice(grp * sub, (grp + 1) * sub)
            if t == 0:
                x_re, x_im = str_ref[srows, :], sti_ref[srows, :]
            r8 = slice(slab * sub, (slab + 1) * sub)
            x_re, x_im = (a_re * x_re - a_im * x_im + xr_ref[r8, :],
                          a_re * x_im + a_im * x_re + xi_ref[r8, :])
            xr_ref[r8, :] = x_re
            xi_ref[r8, :] = x_im
            if t == tc - 1:
                str_ref[srows, :] = x_re
                sti_ref[srows, :] = x_im
        y = _dot(xr_ref[rows, :].astype(BF16), cre_ref[0]) - _dot(xi_ref[rows, :].astype(BF16), cim_ref[0])
        g_ref[rows, :] = _gelu_tanh(y + d_ref[...] * u).astype(BF16)

    @pl.when(n == pl.num_programs(1) - 1)
    def _():
        slr_ref[...] = str_ref[...]
        sli_ref[...] = sti_ref[...]


def s5_scan(hn, nb, b_blk, c_re_blk, c_im_blk, ab_re, ab_im, d_skip, s0_re, s0_im):
    rows, d = hn.shape
    tc = min(rows // nb, TM // SUBLANES)
    ngrp = TM // (tc * SUBLANES)
    assert ngrp == 1 or ngrp * SUBLANES == nb
    st = SDS((nb, S5_GROUPS * S5_STATE), F32)
    return pl.pallas_call(
        functools.partial(_s5_scan_kernel, ngrp=ngrp, tc=tc), grid=(S5_NGB, rows // TM),
        in_specs=[pl.BlockSpec((TM, S5_BC), lambda gb, n: (n, gb)),
                  pl.BlockSpec((1, S5_BC, 2 * S5_BS), lambda gb, n: (gb, 0, 0)),
                  pl.BlockSpec((1, S5_BS, S5_BC), lambda gb, n: (gb, 0, 0)),
                  pl.BlockSpec((1, S5_BS, S5_BC), lambda gb, n: (gb, 0, 0)),
                  pl.BlockSpec((1, 1, S5_BS), lambda gb, n: (gb, 0, 0)),
                  pl.BlockSpec((1, 1, S5_BS), lambda gb, n: (gb, 0, 0)),
                  pl.BlockSpec((1, S5_BC), lambda gb, n: (0, gb)),
                  pl.BlockSpec((nb, S5_BS), lambda gb, n: (0, gb)),
                  pl.BlockSpec((nb, S5_BS), lambda gb, n: (0, gb))],
        out_specs=[pl.BlockSpec((TM, S5_BC), lambda gb, n: (n, gb)),
                   pl.BlockSpec((nb, S5_BS), lambda gb, n: (0, gb)),
                   pl.BlockSpec((nb, S5_BS), lambda gb, n: (0, gb))],
        out_shape=(SDS((rows, d), BF16), st, st),
        scratch_shapes=[pltpu.VMEM((TM, S5_BS), F32), pltpu.VMEM((TM, S5_BS), F32),
                        pltpu.VMEM((nb, S5_BS), F32), pltpu.VMEM((nb, S5_BS), F32)],
        compiler_params=_cparams(2), name="s5_scan",
    )(hn, b_blk, c_re_blk, c_im_blk, ab_re, ab_im, d_skip.reshape(1, d), s0_re, s0_im)


def _glu_out_kernel(g_ref, w_ref, h_ref, o_ref):
    gv = _dot(g_ref[...], w_ref[...])
    o_ref[...] = h_ref[...] + gv[:, :D_MODEL] * _sigmoid(gv[:, D_MODEL:])


def glu_out(g, w_glu, h):
    rows, d = h.shape
    tm = TM // 2
    return pl.pallas_call(
        _glu_out_kernel, grid=(rows // tm,),
        in_specs=[pl.BlockSpec((tm, d), lambda i: (i, 0)),
                  pl.BlockSpec((d, 2 * d), lambda i: (0, 0)),
                  pl.BlockSpec((tm, d), lambda i: (i, 0))],
        out_specs=pl.BlockSpec((tm, d), lambda i: (i, 0)),
        out_shape=SDS((rows, d), F32), compiler_params=_cparams(1), name="glu_out",
    )(g, w_glu, h)


def s5_layer(h, hn, nb, s0_re, s0_im, a_re, a_im, log_dt, b_re, b_im, c_re, c_im, d_skip, w_glu):
    ab_re, ab_im, b_blk = s5_discretize(a_re, a_im, log_dt, b_re, b_im)
    g, sl_re, sl_im = s5_scan(hn, nb, b_blk, s5_block_diag_out(c_re), s5_block_diag_out(c_im),
                              ab_re, ab_im, d_skip, s0_re.reshape(nb, -1), s0_im.reshape(nb, -1))
    shape = (nb, S5_GROUPS, S5_STATE)
    return glu_out(g, w_glu.astype(BF16), h), sl_re.reshape(shape), sl_im.reshape(shape)


class _Problem:
    def __init__(self, c, pack):
        p = c * pack
        r = np.arange(p)
        seq, time = r % pack, r // pack
        same = seq[:, None] == seq[None, :]
        self.p, self.pack = p, pack
        self.tri = same & (time[None, :] <= time[:, None])
        self.whole = same
        masks, pivots = [r[:, None] == r[None, :]], []
        for sz in [2 ** i for i in range(1, int(math.log2(p)) + 1)]:
            blk, off = r // sz, r % sz
            m = same & (blk[:, None] == blk[None, :]) & (off[:, None] >= sz // 2) & (off[None, :] < sz // 2)
            if not m.any():
                continue
            lower = same & (blk[:, None] == blk[None, :]) & (off[None, :] < sz // 2)
            piv = np.where(lower.any(1), (lower * r[None, :]).max(1), r)
            masks.append(m)
            pivots.append(self.tri[piv])
        self.masks = np.stack(masks)
        self.pivots = pivots
        self.seq_lanes = np.repeat(seq[:, None] == np.arange(pack)[None, :], LANES, axis=1)

    def f32(self, x):
        return jnp.asarray(np.asarray(x, np.float32))


def _gla_gate_kernel(h_ref, gn_ref, wgl_ref, w_ref, b_ref, o_ref):
    hn = _rms(h_ref[...], gn_ref[...]).astype(BF16)
    gl = _dot(hn, wgl_ref[...])
    x = _dot(gl.astype(BF16), w_ref[...]) + b_ref[...]
    g = -_softplus(-x) * (1.0 / GLA_TAU)
    for t in range(o_ref.shape[0]):
        o_ref[t] = g[:, t * LANES:(t + 1) * LANES]


def gla_gate(h, gain, w_gl_pad, w_gate_pad, b_gate):
    rows, d = h.shape
    nt = GLA_KEY // LANES
    return pl.pallas_call(
        _gla_gate_kernel, grid=(rows // TM,),
        in_specs=[pl.BlockSpec((TM, d), lambda i: (i, 0)),
                  pl.BlockSpec((1, d), lambda i: (0, 0)),
                  pl.BlockSpec((d, LANES), lambda i: (0, 0)),
                  pl.BlockSpec((LANES, GLA_KEY), lambda i: (0, 0)),
                  pl.BlockSpec((1, GLA_KEY), lambda i: (0, 0))],
        out_specs=pl.BlockSpec((nt, TM, LANES), lambda i: (0, i, 0)),
        out_shape=SDS((nt, rows, LANES), F32), compiler_params=_cparams(1), name="gla_gate",
    )(h, gain.reshape(1, d), w_gl_pad, w_gate_pad, b_gate.reshape(1, GLA_KEY))


def _gla_chunk_kernel(q_ref, k_ref, v_ref, g_ref, wst_ref, mask_ref, seqm_ref, s0_ref, o_ref, s_ref, *, p, pack):
    @pl.when(pl.program_id(1) == 0)
    def _():
        s_ref[...] = s0_ref[...]

    wst = wst_ref[...]
    eye = _eye(GLA_DK)
    nlev = mask_ref.shape[0] - 1
    tpv = GLA_DV // LANES

    def problem(rows, seqs):
        loaded = [(q_ref[hh, rows, :], k_ref[hh, rows, :],
                   [v_ref[hh * tpv + t, rows, :] for t in range(tpv)],
                   g_ref[hh, rows, :], [s_ref[j, hh] for j in seqs]) for hh in range(GLA_HEADS)]
        results = []
        work = []
        for q, k, v, g, ss in loaded:
            q = q * (GLA_DK ** -0.5)
            bp = _dot_exact01(wst, g)
            work.append(dict(q=q, k=k, v=v, ss=ss, bp=bp, b=bp[0:p],
                             att=mask_ref[0] * _dot_nt(q.astype(BF16), k.astype(BF16))))
        for lv in range(1, nlev + 1):
            for wk in work:
                e = jnp.exp(-jnp.abs(wk["b"] - wk["bp"][lv * p:(lv + 1) * p]))
                wk["att"] = wk["att"] + mask_ref[lv] * _dot_nt((wk["q"] * e).astype(BF16),
                                                               (wk["k"] * e).astype(BF16))
        for wk in work:
            q, k, v, ss, bp, b, att = (wk[n] for n in ("q", "k", "v", "ss", "bp", "b", "att"))
            vb = jnp.concatenate(v, axis=1).astype(BF16)
            b_end = bp[(nlev + 1) * p:(nlev + 2) * p]
            s_all = jnp.concatenate(ss, axis=0)
            q_dec = q * jnp.exp(b)
            k_dec = k * jnp.exp(b_end - b)
            if pack > 1:
                q_dec = jnp.concatenate([q_dec] * pack, axis=1) * seqm_ref[...]
                k_dec = jnp.concatenate([k_dec] * pack, axis=1) * seqm_ref[...]
            o = _dot(q_dec.astype(BF16), s_all.astype(BF16)) + _dot(att.astype(BF16), vb)
            last = jnp.concatenate([b_end[0:pack]] * (GLA_DK // pack), axis=0) if pack > 1 else (
                jnp.broadcast_to(b_end[0:1], (GLA_DK, GLA_DK)))
            col = jnp.exp(_dot_nt_exact01(eye, last))
            dec = jnp.concatenate([jnp.broadcast_to(col[:, i:i + 1], (GLA_DK, GLA_DV)) for i in range(pack)], axis=0)
            results.append((o, dec * s_all + _dot_tn(k_dec.astype(BF16), vb)))
        for hh, (o, s_new) in enumerate(results):
            for t in range(tpv):
                o_ref[hh * tpv + t, rows, :] = o[:, t * LANES:(t + 1) * LANES]
            for i, j in enumerate(seqs):
                s_ref[j, hh] = s_new[i * GLA_DK:(i + 1) * GLA_DK]

    if pack == 1:
        def per_sequence(j, carry):
            problem(pl.ds(j, p, stride=SUBLANES), [j])
            return carry

        lax.fori_loop(0, SUBLANES, per_sequence, 0)
    else:
        problem(slice(None), list(range(SUBLANES)))


def gla_chunks(p_tiles, g_tiles, s0, nb, seq, c, pack):
    rows = p_tiles.shape[1]
    pr = _Problem(c, pack)
    wst = pr.f32(np.concatenate([pr.tri] + pr.pivots + [pr.whole], axis=0))
    nkt = GLA_KEY // LANES
    nvt = GLA_VAL // LANES
    nc = seq // c
    tiles = lambda nt, blk: pl.BlockSpec((nt, c * SUBLANES, LANES), lambda b, n: (blk, b * nc + n, 0))
    const = lambda x: pl.BlockSpec(x.shape, lambda b, n: (0,) * x.ndim)
    state = pl.BlockSpec((SUBLANES, GLA_HEADS, GLA_DK, GLA_DV), lambda b, n: (b, 0, 0, 0))
    masks, seqm = pr.f32(pr.masks), pr.f32(pr.seq_lanes)
    o, s_out = pl.pallas_call(
        functools.partial(_gla_chunk_kernel, p=pr.p, pack=pack), grid=(nb // SUBLANES, nc),
        in_specs=[tiles(nkt, 0), tiles(nkt, 1), tiles(nvt, 1), tiles(nkt, 0),
                  const(wst), const(masks), const(seqm), state],
        out_specs=[tiles(nvt, 0), state],
        out_shape=(SDS((nvt, rows, LANES), F32), SDS((nb, GLA_HEADS, GLA_DK, GLA_DV), F32)),
        compiler_params=_cparams(2), name="gla_chunks",
    )(p_tiles, p_tiles, p_tiles, g_tiles, wst, masks, seqm, s0)
    return o, s_out


def _chunking(seq):
    return (CHUNK, 1) if seq % CHUNK == 0 else (seq, SUBLANES)


def gla_layer(h, nb, seq, gain, s0, w_in, w_gate_up, b_gate, norm_o, w_out):
    c, pack = _chunking(seq)
    n_main = 2 * GLA_KEY + 2 * GLA_VAL
    p_tiles = norm_matmul(h, gain, w_in[:, :n_main].astype(BF16), GLA_TN)
    w_gl = jnp.pad(w_in[:, n_main:], ((0, 0), (0, LANES - GLA_RANK))).astype(BF16)
    w_gate_pad = jnp.pad(w_gate_up, ((0, LANES - GLA_RANK), (0, 0))).astype(BF16)
    g_tiles = gla_gate(h, gain, w_gl, w_gate_pad, b_gate)
    o_tiles, s_out = gla_chunks(p_tiles, g_tiles, s0, nb, seq, c, pack)
    h = gated_out(o_tiles, p_tiles, 2, norm_o, w_out.astype(BF16), h, GLA_HEADS, GLA_DV, TM)
    return h, s_out


def _gdn_conv_kernel(x_ref, cw_ref, c0_ref, o_ref, nc_ref, xp_ref, *, tm):
    grp = pl.program_id(0)
    i = pl.program_id(2)
    nb = SUBLANES
    halo = (GDN_CONV - 1) * nb
    scale = jnp.where(grp == 0, GDN_DK ** -0.5, 1.0)
    for t in range(x_ref.shape[0]):
        lanes = slice(t * LANES, (t + 1) * LANES)

        @pl.when(i == 0)
        def _():
            xp_ref[t, 0:halo, :] = c0_ref[:, lanes]

        xp_ref[t, halo:halo + tm, :] = x_ref[t]
        acc = cw_ref[0:1, lanes] * xp_ref[t, 0:tm, :]
        for j in range(1, GDN_CONV):
            acc = acc + cw_ref[j:j + 1, lanes] * xp_ref[t, j * nb:j * nb + tm, :]
        y = acc * _sigmoid(acc)

        @pl.when(grp < 2)
        def _():
            o_ref[t] = y * (lax.rsqrt(jnp.sum(y * y, axis=-1, keepdims=True) + NORM_EPS) * scale)

        @pl.when(grp >= 2)
        def _():
            o_ref[t] = y

        tail = xp_ref[t, tm:tm + halo, :]
        xp_ref[t, 0:halo, :] = tail

        @pl.when(i == pl.num_programs(2) - 1)
        def _():
            nc_ref[:, lanes] = tail


def gdn_conv(p_tiles, conv_w, conv0, nb):
    rows = p_tiles.shape[1]
    ngrp = nb // SUBLANES
    halo = (GDN_CONV - 1) * SUBLANES
    tm = min(TM, rows // ngrp)
    nblk = rows // ngrp // tm
    gt = GDN_TN // LANES
    ng = GDN_CONV_DIM // GDN_TN
    assert tm >= halo
    return pl.pallas_call(
        functools.partial(_gdn_conv_kernel, tm=tm), grid=(ng, ngrp, nblk),
        in_specs=[pl.BlockSpec((gt, tm, LANES), lambda j, g, i: (j, g * nblk + i, 0)),
                  pl.BlockSpec((GDN_CONV, GDN_TN), lambda j, g, i: (0, j)),
                  pl.BlockSpec((halo, GDN_TN), lambda j, g, i: (g, j))],
        out_specs=[pl.BlockSpec((gt, tm, LANES), lambda j, g, i: (j, g * nblk + i, 0)),
                   pl.BlockSpec((halo, GDN_TN), lambda j, g, i: (g, j))],
        out_shape=(SDS((ng * gt, rows, LANES), F32), SDS((ngrp * halo, GDN_CONV_DIM), F32)),
        scratch_shapes=[pltpu.VMEM((gt, halo + tm, LANES), F32)],
        compiler_params=_cparams(3), name="gdn_conv",
    )(p_tiles, conv_w, conv0)


def _gdn_chunk_kernel(q_ref, k_ref, v_ref, ba_ref, alog_ref, dtb_ref, tril_ref, mask_ref, maskb_ref, seqm_ref,
                      s0_ref, o_ref, s_ref, *, p, pack):
    @pl.when(pl.program_id(2) == 0)
    def _():
        s_ref[...] = s0_ref[...]

    tri = tril_ref[0:p, :]
    eye_l = _eye(LANES)
    eye_c = mask_ref[0]
    strict = tri - eye_c
    nlv = mask_ref.shape[0]
    rep = GDN_V_HEADS // GDN_QK_HEADS

    def problem(rows, seqs):
        ba = ba_ref[0, rows, :]
        beta_all = _sigmoid(ba)
        g_all = -jnp.exp(alog_ref[0]) * _softplus(ba + dtb_ref[0])
        gcl = _dot_exact01(tril_ref[...], g_all)
        gc_all, ge_all = gcl[0:p], gcl[p:2 * p]
        gr_all = _dot_nt_exact01(eye_l, gc_all)
        heads = []
        for qh in range(GDN_HALF // rep):
            q = q_ref[qh, rows, :]
            k = k_ref[qh, rows, :]
            kb = k.astype(BF16)
            kq = _dot_nt(jnp.concatenate([kb, q.astype(BF16)], axis=0), kb)
            kk, qk = kq[:p], kq[p:]
            for j in range(rep):
                hh = qh * rep + j
                beta = beta_all[:, hh:hh + 1]
                g_col = gc_all[:, GDN_HALF + hh:GDN_HALF + hh + 1]
                g_end = ge_all[:, GDN_HALF + hh:GDN_HALF + hh + 1]
                g_row = gr_all[GDN_HALF + hh:GDN_HALF + hh + 1, :]
                decay = tri * jnp.exp(jnp.minimum(g_col - g_row, 0.0))
                m = strict * (kk * decay * beta)
                heads.append(dict(hh=hh, q=q, k=k, v=v_ref[hh, rows, :], beta=beta, g_col=g_col, g_end=g_end,
                                  s=jnp.concatenate([s_ref[i, hh] for i in seqs], axis=0),
                                  mb=m.astype(BF16), qkd=(qk * decay).astype(BF16),
                                  t=eye_c - mask_ref[1] * m))
        for lv in range(2, nlv):
            for hd in heads:
                tb = hd["t"].astype(BF16)
                hd["t"] = hd["t"] - _dot(_dot(tb, maskb_ref[lv] * hd["mb"]).astype(BF16), tb)
        for hd in heads:
            e_g = jnp.exp(hd["g_col"])
            rhs = jnp.concatenate([hd["v"] * hd["beta"], hd["k"] * (hd["beta"] * e_g)], axis=1).astype(BF16)
            hd["uw"] = _dot(hd["t"].astype(BF16), rhs)
            hd["q_dec"] = hd["q"] * e_g
        for hd in heads:
            uw = hd["uw"]
            lhs = jnp.concatenate([uw[:, GDN_DV:], hd["q_dec"]], axis=0)
            if pack > 1:
                lhs = jnp.concatenate([lhs] * pack, axis=1) * jnp.concatenate([seqm_ref[...]] * 2, axis=0)
            ws = _dot(lhs.astype(BF16), hd["s"].astype(BF16))
            hd["vnb"] = (uw[:, :GDN_DV] - ws[:p]).astype(BF16)
            hd["o_inter"] = ws[p:]
        for hd in heads:
            g_end = hd["g_end"]
            k_dec = hd["k"] * jnp.exp(g_end - hd["g_col"])
            if pack > 1:
                k_dec = jnp.concatenate([k_dec] * pack, axis=1) * seqm_ref[...]
            hd["o"] = hd["o_inter"] + _dot(hd["qkd"], hd["vnb"])
            dec = jnp.concatenate([jnp.broadcast_to(jnp.exp(g_end[i:i + 1, :]), (GDN_DK, GDN_DV))
                                   for i in range(pack)], axis=0)
            hd["s_new"] = dec * hd["s"] + _dot_tn(k_dec.astype(BF16), hd["vnb"])
        for hd in heads:
            o_ref[hd["hh"], rows, :] = hd["o"]
            for i, j in enumerate(seqs):
                s_ref[j, hd["hh"]] = hd["s_new"][i * GDN_DK:(i + 1) * GDN_DK]

    if pack == 1:
        def per_sequence(b, carry):
            problem(pl.ds(b, p, stride=SUBLANES), [b])
            return carry

        lax.fori_loop(0, SUBLANES, per_sequence, 0)
    else:
        problem(slice(None), list(range(SUBLANES)))


def gdn_chunks(qkv_tiles, p_tiles, a_log, dt_bias, s0, nb, seq, c, pack):
    ba0 = (GDN_CONV_DIM + GDN_VAL) // LANES
    rows = qkv_tiles.shape[1]
    nqk = GDN_HALF // (GDN_V_HEADS // GDN_QK_HEADS)
    nc = seq // c
    pr = _Problem(c, pack)
    pad = lambda x: jnp.pad(x.reshape(2, 1, GDN_HALF), ((0, 0), (0, 0), (GDN_HALF, LANES - 2 * GDN_HALF)))
    masks, seqm = pr.f32(pr.masks), pr.f32(pr.seq_lanes)
    tril = pr.f32(np.concatenate([pr.tri, pr.whole], axis=0))
    tiles = lambda nt, blk: pl.BlockSpec((nt, c * SUBLANES, LANES), lambda b, hf, n: (blk(hf), b * nc + n, 0))
    const = lambda x: pl.BlockSpec(x.shape, lambda b, hf, n: (0,) * x.ndim)
    state = pl.BlockSpec((SUBLANES, GDN_HALF, GDN_DK, GDN_DV), lambda b, hf, n: (b, hf, 0, 0))
    o, s_out = pl.pallas_call(
        functools.partial(_gdn_chunk_kernel, p=pr.p, pack=pack), grid=(nb // SUBLANES, 2, nc),
        in_specs=[tiles(nqk, lambda hf: hf), tiles(nqk, lambda hf: 2 + hf),
                  tiles(GDN_HALF, lambda hf: 2 + hf), tiles(1, lambda hf: ba0 + hf),
                  pl.BlockSpec((1, 1, LANES), lambda b, hf, n: (hf, 0, 0)),
                  pl.BlockSpec((1, 1, LANES), lambda b, hf, n: (hf, 0, 0)),
                  const(tril), const(masks), const(masks), const(seqm), state],
        out_specs=[tiles(GDN_HALF, lambda hf: hf), state],
        out_shape=(SDS((GDN_V_HEADS, rows, LANES), F32), SDS((nb, GDN_V_HEADS, GDN_DK, GDN_DV), F32)),
        compiler_params=_cparams(3), name="gdn_chunks",
    )(qkv_tiles, qkv_tiles, qkv_tiles, p_tiles, pad(a_log), pad(dt_bias), tril, masks, masks.astype(BF16), seqm, s0)
    return o, s_out


def gdn_layer(h, nb, seq, gain, s0, conv0, w_in, conv_w, a_log, dt_bias, norm_o, w_out):
    c, pack = _chunking(seq)
    n_main = GDN_CONV_DIM + GDN_VAL
    w_b = w_in[:, n_main:n_main + GDN_V_HEADS].reshape(D_MODEL, 2, GDN_HALF)
    w_a = w_in[:, n_main + GDN_V_HEADS:].reshape(D_MODEL, 2, GDN_HALF)
    w_ba = jnp.pad(jnp.concatenate([w_b, w_a], axis=2), ((0, 0), (0, 0), (0, LANES - 2 * GDN_HALF)))
    w_all = jnp.concatenate([w_in[:, :n_main], w_ba.reshape(D_MODEL, 2 * LANES)], axis=1).astype(BF16)
    p_tiles = norm_matmul(h, gain, w_all, GDN_PROJ_TN)
    ngrp = nb // SUBLANES
    conv0_tm = conv0.reshape(ngrp, SUBLANES, GDN_CONV - 1, GDN_CONV_DIM).transpose(0, 2, 1, 3)
    conv0_tm = conv0_tm.reshape(ngrp * (GDN_CONV - 1) * SUBLANES, GDN_CONV_DIM)
    qkv_tiles, new_conv = gdn_conv(p_tiles, conv_w, conv0_tm, nb)
    o_tiles, s_out = gdn_chunks(qkv_tiles, p_tiles, a_log, dt_bias, s0, nb, seq, c, pack)
    h = gated_out(o_tiles, p_tiles, 2, norm_o, w_out.astype(BF16), h, GDN_V_HEADS, GDN_DV, TM // 2)
    new_conv = new_conv.reshape(ngrp, GDN_CONV - 1, SUBLANES, GDN_CONV_DIM)
    return h, s_out, new_conv.transpose(0, 2, 1, 3).reshape(nb, GDN_CONV - 1, GDN_CONV_DIM)


def _trunk(x, s5_re, s5_im, gla_s, gdn_s, gdn_conv_s, w):
    nb, seq, d = x.shape
    h, hn = norm_in(x, w["norm_mix"][0])
    h, s5r0, s5i0 = s5_layer(h, hn, nb, s5_re[0], s5_im[0], w["s5_a_re"][0], w["s5_a_im"][0],
                             w["s5_log_dt"][0], w["s5_b_re"][0], w["s5_b_im"][0], w["s5_c_re"][0],
                             w["s5_c_im"][0], w["s5_d"][0], w["s5_w_glu"][0])
    h = ffn(h, w["norm_ffn"][0], w["w_up"][0], w["w_down"][0])
    h, gla_o = gla_layer(h, nb, seq, w["norm_mix"][1], gla_s[0], w["gla_w_in"][0], w["gla_w_gate_up"][0],
                         w["gla_b_gate"][0], w["gla_norm"][0], w["gla_w_out"][0])
    h = ffn(h, w["norm_ffn"][1], w["w_up"][1], w["w_down"][1])
    h, gdn_o, conv_o = gdn_layer(h, nb, seq, w["norm_mix"][2], gdn_s[0], gdn_conv_s[0], w["gdn_w_in"][0],
                                 w["gdn_conv_w"][0], w["gdn_a_log"][0], w["gdn_dt_bias"][0],
                                 w["gdn_norm"][0], w["gdn_w_out"][0])
    h, hn = ffn(h, w["norm_ffn"][2], w["w_up"][2], w["w_down"][2], next_gain=w["norm_mix"][3])
    h, s5r1, s5i1 = s5_layer(h, hn, nb, s5_re[1], s5_im[1], w["s5_a_re"][1], w["s5_a_im"][1],
                             w["s5_log_dt"][1], w["s5_b_re"][1], w["s5_b_im"][1], w["s5_c_re"][1],
                             w["s5_c_im"][1], w["s5_d"][1], w["s5_w_glu"][1])
    h = ffn(h, w["norm_ffn"][3], w["w_up"][3], w["w_down"][3])
    y = norm_out(h, w["norm_final"], nb, seq)
    return (y, jnp.stack([s5r0, s5r1]), jnp.stack([s5i0, s5i1]), gla_o[None], gdn_o[None], conv_o[None])


def kernel(x_prompt, x_sample, state_s5_re, state_s5_im, state_gla, state_gdn, state_gdn_conv, norm_mix, norm_ffn, norm_final, w_up, w_down, s5_a_re, s5_a_im, s5_log_dt, s5_b_re, s5_b_im, s5_c_re, s5_c_im, s5_d, s5_w_glu, gla_w_in, gla_w_gate_up, gla_b_gate, gla_norm, gla_w_out, gdn_w_in, gdn_conv_w, gdn_a_log, gdn_dt_bias, gdn_norm, gdn_w_out):
    w = dict(norm_mix=norm_mix, norm_ffn=norm_ffn, norm_final=norm_final,
             w_up=w_up.astype(BF16), w_down=w_down.astype(BF16),
             s5_a_re=s5_a_re, s5_a_im=s5_a_im, s5_log_dt=s5_log_dt, s5_b_re=s5_b_re, s5_b_im=s5_b_im,
             s5_c_re=s5_c_re, s5_c_im=s5_c_im, s5_d=s5_d, s5_w_glu=s5_w_glu,
             gla_w_in=gla_w_in, gla_w_gate_up=gla_w_gate_up, gla_b_gate=gla_b_gate, gla_norm=gla_norm,
             gla_w_out=gla_w_out, gdn_w_in=gdn_w_in, gdn_conv_w=gdn_conv_w, gdn_a_log=gdn_a_log,
             gdn_dt_bias=gdn_dt_bias, gdn_norm=gdn_norm, gdn_w_out=gdn_w_out)
    bp = x_prompt.shape[0]
    dt = x_prompt.dtype
    z_s5 = jnp.zeros((state_s5_re.shape[0], bp) + state_s5_re.shape[2:], dt)
    z_gla = jnp.zeros((state_gla.shape[0], bp) + state_gla.shape[2:], dt)
    z_gdn = jnp.zeros((state_gdn.shape[0], bp) + state_gdn.shape[2:], dt)
    z_conv = jnp.zeros((state_gdn_conv.shape[0], bp) + state_gdn_conv.shape[2:], dt)
    out_p = _trunk(x_prompt, z_s5, z_s5, z_gla, z_gdn, z_conv, w)
    out_s = _trunk(x_sample, state_s5_re, state_s5_im, state_gla, state_gdn, state_gdn_conv, w)
    return (out_p[0], out_s[0]) + out_p[1:] + out_s[1:]
```

```python
import functools
import math

import numpy as np
import jax
import jax.numpy as jnp
from jax import lax
from jax.experimental import pallas as pl
from jax.experimental.pallas import tpu as pltpu

F32 = jnp.float32
BF16 = jnp.bfloat16
SDS = jax.ShapeDtypeStruct

D_MODEL = 1024
NORM_EPS = 1e-6

S5_GROUP = 16
S5_STATE = 64
S5_GROUPS = D_MODEL // S5_GROUP
S5_GB = 16
S5_NGB = S5_GROUPS // S5_GB
S5_BC = S5_GB * S5_GROUP
S5_BS = S5_GB * S5_STATE

GLA_HEADS = 4
GLA_DK = 128
GLA_DV = 256
GLA_KEY = GLA_HEADS * GLA_DK
GLA_VAL = GLA_HEADS * GLA_DV
GLA_RANK = 16
GLA_TAU = 16.0
GLA_TN = 1024

GDN_DK = 128
GDN_DV = 128
GDN_QK_HEADS = 8
GDN_V_HEADS = 16
GDN_KEY = GDN_QK_HEADS * GDN_DK
GDN_VAL = GDN_V_HEADS * GDN_DV
GDN_HALF = GDN_V_HEADS // 2
GDN_CONV = 4
GDN_CONV_DIM = 2 * GDN_KEY + GDN_VAL
GDN_TN = 1024
GDN_PROJ_TN = 1280

CHUNK = 128
LANES = 128
SUBLANES = 8
TM = 1024
FFN_TF = 1024
MIB = 1024 * 1024


def _cparams(n_axes, vmem_mib=48):
    return pltpu.CompilerParams(dimension_semantics=("arbitrary",) * n_axes,
                                vmem_limit_bytes=vmem_mib * MIB)


def _rms(x, gain):
    ms = jnp.mean(x * x, axis=-1, keepdims=True)
    return x * lax.rsqrt(ms + NORM_EPS) * gain


def _sigmoid(x):
    return 1.0 / (1.0 + jnp.exp(-x))


def _softplus(x):
    return jnp.maximum(x, 0.0) + jnp.log1p(jnp.exp(-jnp.abs(x)))


def _gelu_tanh(x):
    c = math.sqrt(2.0 / math.pi)
    return x * (0.5 * (1.0 + jnp.tanh(c * (x + 0.044715 * (x * x * x)))))


def _dot(a, b):
    return jnp.dot(a, b, preferred_element_type=F32)


def _dot_nt(a, b):
    return lax.dot_general(a, b, (((1,), (1,)), ((), ())), preferred_element_type=F32)


def _dot_tn(a, b):
    return lax.dot_general(a, b, (((0,), (0,)), ((), ())), preferred_element_type=F32)


def _eye(n):
    return (lax.broadcasted_iota(jnp.int32, (n, n), 0)
            == lax.broadcasted_iota(jnp.int32, (n, n), 1)).astype(F32)


def _split3(x):
    x1 = x.astype(BF16)
    r1 = x - x1.astype(F32)
    x2 = r1.astype(BF16)
    x3 = (r1 - x2.astype(F32)).astype(BF16)
    return x1, x2, x3


def _dot_exact01(m01, x):
    mb = m01.astype(BF16)
    x1, x2, x3 = _split3(x)
    return _dot(mb, x1) + _dot(mb, x2) + _dot(mb, x3)


def _dot_nt_exact01(m01, x):
    mb = m01.astype(BF16)
    x1, x2, x3 = _split3(x)
    return _dot_nt(mb, x1) + _dot_nt(mb, x2) + _dot_nt(mb, x3)


def _norm_in_kernel(x_ref, g_ref, h_ref, hn_ref):
    for j in range(SUBLANES):
        x = x_ref[j]
        h_ref[:, j, :] = x
        hn_ref[:, j, :] = _rms(x, g_ref[...])


def norm_in(x, gain):
    nb, seq, d = x.shape
    tt = min(seq, TM // SUBLANES)
    nblk = seq // tt
    out = SDS((nb // SUBLANES * seq, SUBLANES, d), F32)
    h, hn = pl.pallas_call(
        _norm_in_kernel, grid=(nb // SUBLANES, nblk),
        in_specs=[pl.BlockSpec((SUBLANES, tt, d), lambda g, i: (g, i, 0)),
                  pl.BlockSpec((1, d), lambda g, i: (0, 0))],
        out_specs=[pl.BlockSpec((tt, SUBLANES, d), lambda g, i: (g * nblk + i, 0, 0))] * 2,
        out_shape=(out, out), compiler_params=_cparams(2), name="norm_in",
    )(x, gain.reshape(1, d))
    return h.reshape(seq * nb, d), hn.reshape(seq * nb, d)


def _norm_out_kernel(h_ref, g_ref, y_ref):
    for j in range(SUBLANES):
        y_ref[j] = _rms(h_ref[:, j, :], g_ref[...])


def norm_out(h, gain, nb, seq):
    d = h.shape[1]
    tt = min(seq, TM // SUBLANES)
    nblk = seq // tt
    return pl.pallas_call(
        _norm_out_kernel, grid=(nb // SUBLANES, nblk),
        in_specs=[pl.BlockSpec((tt, SUBLANES, d), lambda g, i: (g * nblk + i, 0, 0)),
                  pl.BlockSpec((1, d), lambda g, i: (0, 0))],
        out_specs=pl.BlockSpec((SUBLANES, tt, d), lambda g, i: (g, i, 0)),
        out_shape=SDS((nb, seq, d), F32), compiler_params=_cparams(2), name="norm_out",
    )(h.reshape(nb // SUBLANES * seq, SUBLANES, d), gain.reshape(1, d))


def _norm_matmul_kernel(h_ref, g_ref, w_ref, o_ref, hn_ref):
    @pl.when(pl.program_id(1) == 0)
    def _():
        hn_ref[...] = _rms(h_ref[...], g_ref[...]).astype(BF16)

    res = _dot(hn_ref[...], w_ref[...])
    for t in range(o_ref.shape[0]):
        o_ref[t] = res[:, t * LANES:(t + 1) * LANES]


def norm_matmul(h, gain, w, tn):
    rows, d = h.shape
    n = w.shape[1]
    return pl.pallas_call(
        _norm_matmul_kernel, grid=(rows // TM, n // tn),
        in_specs=[pl.BlockSpec((TM, d), lambda i, j: (i, 0)),
                  pl.BlockSpec((1, d), lambda i, j: (0, 0)),
                  pl.BlockSpec((d, tn), lambda i, j: (0, j))],
        out_specs=pl.BlockSpec((tn // LANES, TM, LANES), lambda i, j: (j, i, 0)),
        out_shape=SDS((n // LANES, rows, LANES), F32),
        scratch_shapes=[pltpu.VMEM((TM, d), BF16)],
        compiler_params=_cparams(2), name="norm_matmul",
    )(h, gain.reshape(1, d), w)


def _ffn_kernel(h_ref, hnext_ref, g_ref, wu_ref, wd_ref, ng_ref, *rest, emit_norm):
    o_ref, hn_ref, acc_ref = rest[0], rest[-2], rest[-1]
    i = pl.program_id(0)
    j = pl.program_id(1)
    last = pl.num_programs(1) - 1
    slot = i % 2

    @pl.when((i == 0) & (j == 0))
    def _():
        hn_ref[0] = _rms(h_ref[...], g_ref[...]).astype(BF16)

    def partial():
        a = jnp.square(jnp.maximum(_dot(hn_ref[slot], wu_ref[...]), 0.0)).astype(BF16)
        return _dot(a, wd_ref[...])

    @pl.when(j == 0)
    def _():
        acc_ref[...] = partial()

    @pl.when((j > 0) & (j < last))
    def _():
        acc_ref[...] += partial()

    @pl.when(j == last)
    def _():
        hn_ref[1 - slot] = _rms(hnext_ref[...], g_ref[...]).astype(BF16)
        out = h_ref[...] + acc_ref[...] + partial()
        o_ref[...] = out
        if emit_norm:
            rest[1][...] = _rms(out, ng_ref[...])


def ffn(h, gain, w_up, w_down, next_gain=None):
    rows, d = h.shape
    f = w_up.shape[1]
    nblk = rows // TM
    assert f // FFN_TF >= 2
    emit_norm = next_gain is not None
    row_block = pl.BlockSpec((TM, d), lambda i, j: (i, 0))
    vec = pl.BlockSpec((1, d), lambda i, j: (0, 0))
    out = pl.pallas_call(
        functools.partial(_ffn_kernel, emit_norm=emit_norm), grid=(nblk, f // FFN_TF),
        in_specs=[row_block,
                  pl.BlockSpec((TM, d), lambda i, j: (jnp.minimum(i + 1, nblk - 1), 0)),
                  vec,
                  pl.BlockSpec((d, FFN_TF), lambda i, j: (0, j)),
                  pl.BlockSpec((FFN_TF, d), lambda i, j: (j, 0)),
                  vec],
        out_specs=[row_block] * (2 if emit_norm else 1),
        out_shape=[SDS((rows, d), F32)] * (2 if emit_norm else 1),
        scratch_shapes=[pltpu.VMEM((2, TM, d), BF16), pltpu.VMEM((TM, d), F32)],
        compiler_params=_cparams(2, vmem_mib=56 if emit_norm else 48), name="ffn",
    )(h, h, gain.reshape(1, d), w_up, w_down, (next_gain if emit_norm else gain).reshape(1, d))
    return (out[0], out[1]) if emit_norm else out[0]


def _gated_out_kernel(o_ref, z_ref, gn_ref, w_ref, h_ref, out_ref, *, nheads, hd):
    tph = hd // LANES
    kstep = 2 * LANES
    acc = h_ref[...]
    for k0 in range(0, nheads * hd, kstep):
        tiles = range(k0 // LANES, (k0 + kstep) // LANES)
        o = jnp.concatenate([o_ref[t] for t in tiles], axis=1)
        z = jnp.concatenate([z_ref[t] for t in tiles], axis=1)
        on = jnp.concatenate([_rms(o[:, i * hd:(i + 1) * hd], gn_ref[...]) for i in range(kstep // hd)], axis=1)
        a = (on * (z * _sigmoid(z))).astype(BF16)
        acc = acc + _dot(a, w_ref[k0:k0 + kstep, :])
    out_ref[...] = acc


def gated_out(o_tiles, p_tiles, z_block, gain, w_out, h, nheads, hd, tm):
    rows, d = h.shape
    kdim = nheads * hd
    nt = kdim // LANES
    return pl.pallas_call(
        functools.partial(_gated_out_kernel, nheads=nheads, hd=hd), grid=(rows // tm,),
        in_specs=[pl.BlockSpec((nt, tm, LANES), lambda i: (0, i, 0)),
                  pl.BlockSpec((nt, tm, LANES), lambda i: (z_block, i, 0)),
                  pl.BlockSpec((1, hd), lambda i: (0, 0)),
                  pl.BlockSpec((kdim, d), lambda i: (0, 0)),
                  pl.BlockSpec((tm, d), lambda i: (i, 0))],
        out_specs=pl.BlockSpec((tm, d), lambda i: (i, 0)),
        out_shape=SDS((rows, d), F32),
        compiler_params=_cparams(1), name="gated_out",
    )(o_tiles, p_tiles, gain.reshape(1, hd), w_out, h)


def _s5_discretize_kernel(are_ref, aim_ref, ldt_ref, bre_ref, bim_ref,
                          abr_ref, abi_ref, bbr_ref, bbi_ref):
    a_re, a_im = are_ref[...], aim_ref[...]
    dt = jnp.exp(ldt_ref[...])
    mag = jnp.exp(a_re * dt)
    ab_re = mag * jnp.cos(a_im * dt)
    ab_im = mag * jnp.sin(a_im * dt)
    den = a_re * a_re + a_im * a_im
    c_re = ((ab_re - 1.0) * a_re + ab_im * a_im) / den
    c_im = (ab_im * a_re - (ab_re - 1.0) * a_im) / den
    abr_ref[...] = ab_re
    abi_ref[...] = ab_im
    bbr_ref[...] = c_re * bre_ref[...] - c_im * bim_ref[...]
    bbi_ref[...] = c_re * bim_ref[...] + c_im * bre_ref[...]


def s5_discretize(a_re, a_im, log_dt, b_re, b_im):
    g, p, c = S5_GROUPS, S5_STATE, S5_GROUP
    expand = lambda v: jnp.broadcast_to(v[..., None], (g, p, c)).reshape(g, p * c)
    ldt = jnp.broadcast_to(log_dt[:, None], (g, p * c))
    shp = SDS((g, p * c), F32)
    ab_re, ab_im, bb_re, bb_im = pl.pallas_call(
        _s5_discretize_kernel, out_shape=(shp, shp, shp, shp), name="s5_discretize",
    )(expand(a_re), expand(a_im), ldt, b_re.reshape(g, p * c), b_im.reshape(g, p * c))
    ab_re = ab_re.reshape(g, p, c)[:, :, 0].reshape(S5_NGB, 1, S5_BS)
    ab_im = ab_im.reshape(g, p, c)[:, :, 0].reshape(S5_NGB, 1, S5_BS)
    eye = jnp.eye(S5_GB, dtype=F32)

    def block_diag_in(bb):
        bb = bb.reshape(S5_NGB, S5_GB, p, c)
        return jnp.einsum("bgpc,gh->bgchp", bb, eye).reshape(S5_NGB, S5_BC, S5_BS)

    b_blk = jnp.concatenate([block_diag_in(bb_re.reshape(g, p, c)),
                             block_diag_in(bb_im.reshape(g, p, c))], axis=-1).astype(BF16)
    return ab_re, ab_im, b_blk


def s5_block_diag_out(c_par):
    eye = jnp.eye(S5_GB, dtype=F32)
    cc = c_par.reshape(S5_NGB, S5_GB, S5_GROUP, S5_STATE)
    return jnp.einsum("bgcp,gh->bgphc", cc, eye).reshape(S5_NGB, S5_BS, S5_BC).astype(BF16)


def _s5_scan_kernel(u_ref, b_ref, cre_ref, cim_ref, are_ref, aim_ref, d_ref, s0r_ref, s0i_ref,
                    g_ref, slr_ref, sli_ref, xr_ref, xi_ref, str_ref, sti_ref, *, ngrp, tc):
    n = pl.program_id(1)

    @pl.when(n == 0)
    def _():
        str_ref[...] = s0r_ref[...]
        sti_ref[...] = s0i_ref[...]

    sub = SUBLANES
    a_re = jnp.broadcast_to(are_ref[0], (sub, S5_BS))
    a_im = jnp.broadcast_to(aim_ref[0], (sub, S5_BS))
    sb = 2 * LANES
    x_re = x_im = None
    for k in range(u_ref.shape[0] // sb):
        rows = slice(k * sb, (k + 1) * sb)
        u = u_ref[rows, :]
        ub = u.astype(BF16)
        xr_ref[rows, :] = _dot(ub, b_ref[0, :, :S5_BS])
        xi_ref[rows, :] = _dot(ub, b_ref[0, :, S5_BS:])
        for slab in range(k * sb // sub, (k + 1) * sb // sub):
            grp, t = divmod(slab, tc)
            srows = slice(grp * sub, (grp + 1) * sub)
            if t == 0:
                x_re, x_im = str_ref[srows, :], sti_ref[srows, :]
            r8 = slice(slab * sub, (slab + 1) * sub)
            x_re, x_im = (a_re * x_re - a_im * x_im + xr_ref[r8, :],
                          a_re * x_im + a_im * x_re + xi_ref[r8, :])
            xr_ref[r8, :] = x_re
            xi_ref[r8, :] = x_im
            if t == tc - 1:
                str_ref[srows, :] = x_re
                sti_ref[srows, :] = x_im
        y = _dot(xr_ref[rows, :].astype(BF16), cre_ref[0]) - _dot(xi_ref[rows, :].astype(BF16), cim_ref[0])
        g_ref[rows, :] = _gelu_tanh(y + d_ref[...] * u).astype(BF16)

    @pl.when(n == pl.num_programs(1) - 1)
    def _():
        slr_ref[...] = str_ref[...]
        sli_ref[...] = sti_ref[...]


def s5_scan(hn, nb, b_blk, c_re_blk, c_im_blk, ab_re, ab_im, d_skip, s0_re, s0_im):
    rows, d = hn.shape
    tc = min(rows // nb, TM // SUBLANES)
    ngrp = TM // (tc * SUBLANES)
    assert ngrp == 1 or ngrp * SUBLANES == nb
    st = SDS((nb, S5_GROUPS * S5_STATE), F32)
    return pl.pallas_call(
        functools.partial(_s5_scan_kernel, ngrp=ngrp, tc=tc), grid=(S5_NGB, rows // TM),
        in_specs=[pl.BlockSpec((TM, S5_BC), lambda gb, n: (n, gb)),
                  pl.BlockSpec((1, S5_BC, 2 * S5_BS), lambda gb, n: (gb, 0, 0)),
                  pl.BlockSpec((1, S5_BS, S5_BC), lambda gb, n: (gb, 0, 0)),
                  pl.BlockSpec((1, S5_BS, S5_BC), lambda gb, n: (gb, 0, 0)),
                  pl.BlockSpec((1, 1, S5_BS), lambda gb, n: (gb, 0, 0)),
                  pl.BlockSpec((1, 1, S5_BS), lambda gb, n: (gb, 0, 0)),
                  pl.BlockSpec((1, S5_BC), lambda gb, n: (0, gb)),
                  pl.BlockSpec((nb, S5_BS), lambda gb, n: (0, gb)),
                  pl.BlockSpec((nb, S5_BS), lambda gb, n: (0, gb))],
        out_specs=[pl.BlockSpec((TM, S5_BC), lambda gb, n: (n, gb)),
                   pl.BlockSpec((nb, S5_BS), lambda gb, n: (0, gb)),
                   pl.BlockSpec((nb, S5_BS), lambda gb, n: (0, gb))],
        out_shape=(SDS((rows, d), BF16), st, st),
        scratch_shapes=[pltpu.VMEM((TM, S5_BS), F32), pltpu.VMEM((TM, S5_BS), F32),
                        pltpu.VMEM((nb, S5_BS), F32), pltpu.VMEM((nb, S5_BS), F32)],
        compiler_params=_cparams(2), name="s5_scan",
    )(hn, b_blk, c_re_blk, c_im_blk, ab_re, ab_im, d_skip.reshape(1, d), s0_re, s0_im)


def _glu_out_kernel(g_ref, w_ref, h_ref, o_ref):
    gv = _dot(g_ref[...], w_ref[...])
    o_ref[...] = h_ref[...] + gv[:, :D_MODEL] * _sigmoid(gv[:, D_MODEL:])


def glu_out(g, w_glu, h):
    rows, d = h.shape
    tm = TM // 2
    return pl.pallas_call(
        _glu_out_kernel, grid=(rows // tm,),
        in_specs=[pl.BlockSpec((tm, d), lambda i: (i, 0)),
                  pl.BlockSpec((d, 2 * d), lambda i: (0, 0)),
                  pl.BlockSpec((tm, d), lambda i: (i, 0))],
        out_specs=pl.BlockSpec((tm, d), lambda i: (i, 0)),
        out_shape=SDS((rows, d), F32), compiler_params=_cparams(1), name="glu_out",
    )(g, w_glu, h)


def s5_layer(h, hn, nb, s0_re, s0_im, a_re, a_im, log_dt, b_re, b_im, c_re, c_im, d_skip, w_glu):
    ab_re, ab_im, b_blk = s5_discretize(a_re, a_im, log_dt, b_re, b_im)
    g, sl_re, sl_im = s5_scan(hn, nb, b_blk, s5_block_diag_out(c_re), s5_block_diag_out(c_im),
                              ab_re, ab_im, d_skip, s0_re.reshape(nb, -1), s0_im.reshape(nb, -1))
    shape = (nb, S5_GROUPS, S5_STATE)
    return glu_out(g, w_glu.astype(BF16), h), sl_re.reshape(shape), sl_im.reshape(shape)


class _Problem:
    def __init__(self, c, pack):
        p = c * pack
        r = np.arange(p)
        seq, time = r % pack, r // pack
        same = seq[:, None] == seq[None, :]
        self.p, self.pack = p, pack
        self.tri = same & (time[None, :] <= time[:, None])
        self.whole = same
        masks, pivots = [r[:, None] == r[None, :]], []
        for sz in [2 ** i for i in range(1, int(math.log2(p)) + 1)]:
            blk, off = r // sz, r % sz
            m = same & (blk[:, None] == blk[None, :]) & (off[:, None] >= sz // 2) & (off[None, :] < sz // 2)
            if not m.any():
                continue
            lower = same & (blk[:, None] == blk[None, :]) & (off[None, :] < sz // 2)
            piv = np.where(lower.any(1), (lower * r[None, :]).max(1), r)
            masks.append(m)
            pivots.append(self.tri[piv])
        self.masks = np.stack(masks)
        self.pivots = pivots
        self.seq_lanes = np.repeat(seq[:, None] == np.arange(pack)[None, :], LANES, axis=1)

    def f32(self, x):
        return jnp.asarray(np.asarray(x, np.float32))


def _gla_gate_kernel(h_ref, gn_ref, wgl_ref, w_ref, b_ref, o_ref):
    hn = _rms(h_ref[...], gn_ref[...]).astype(BF16)
    gl = _dot(hn, wgl_ref[...])
    x = _dot(gl.astype(BF16), w_ref[...]) + b_ref[...]
    g = -_softplus(-x) * (1.0 / GLA_TAU)
    for t in range(o_ref.shape[0]):
        o_ref[t] = g[:, t * LANES:(t + 1) * LANES]


def gla_gate(h, gain, w_gl_pad, w_gate_pad, b_gate):
    rows, d = h.shape
    nt = GLA_KEY // LANES
    return pl.pallas_call(
        _gla_gate_kernel, grid=(rows // TM,),
        in_specs=[pl.BlockSpec((TM, d), lambda i: (i, 0)),
                  pl.BlockSpec((1, d), lambda i: (0, 0)),
                  pl.BlockSpec((d, LANES), lambda i: (0, 0)),
                  pl.BlockSpec((LANES, GLA_KEY), lambda i: (0, 0)),
                  pl.BlockSpec((1, GLA_KEY), lambda i: (0, 0))],
        out_specs=pl.BlockSpec((nt, TM, LANES), lambda i: (0, i, 0)),
        out_shape=SDS((nt, rows, LANES), F32), compiler_params=_cparams(1), name="gla_gate",
    )(h, gain.reshape(1, d), w_gl_pad, w_gate_pad, b_gate.reshape(1, GLA_KEY))


def _gla_chunk_kernel(q_ref, k_ref, v_ref, g_ref, wst_ref, mask_ref, seqm_ref, s0_ref, o_ref, s_ref, *, p, pack):
    @pl.when(pl.program_id(1) == 0)
    def _():
        s_ref[...] = s0_ref[...]

    wst = wst_ref[...]
    eye = _eye(GLA_DK)
    nlev = mask_ref.shape[0] - 1
    tpv = GLA_DV // LANES

    def problems(rows_seqs):
        loaded = [(hh, rows, seqs, q_ref[hh, rows, :], k_ref[hh, rows, :],
                   [v_ref[hh * tpv + t, rows, :] for t in range(tpv)],
                   g_ref[hh, rows, :], [s_ref[j, hh] for j in seqs])
                  for rows, seqs in rows_seqs for hh in range(GLA_HEADS)]
        results = []
        work = []
        for hh, rows, seqs, q, k, v, g, ss in loaded:
            q = q * (GLA_DK ** -0.5)
            bp = _dot_exact01(wst, g)
            work.append(dict(hh=hh, rows=rows, seqs=seqs, q=q, k=k, v=v, ss=ss, bp=bp, b=bp[0:p],
                             att=mask_ref[0] * _dot_nt(q.astype(BF16), k.astype(BF16))))
        for lv in range(1, nlev + 1):
            for wk in work:
                e = jnp.exp(-jnp.abs(wk["b"] - wk["bp"][lv * p:(lv + 1) * p]))
                wk["att"] = wk["att"] + mask_ref[lv] * _dot_nt((wk["q"] * e).astype(BF16),
                                                               (wk["k"] * e).astype(BF16))
        for wk in work:
            q, k, v, ss, bp, b, att = (wk[n] for n in ("q", "k", "v", "ss", "bp", "b", "att"))
            vb = jnp.concatenate(v, axis=1).astype(BF16)
            b_end = bp[(nlev + 1) * p:(nlev + 2) * p]
            s_all = jnp.concatenate(ss, axis=0)
            q_dec = q * jnp.exp(b)
            k_dec = k * jnp.exp(b_end - b)
            if pack > 1:
                q_dec = jnp.concatenate([q_dec] * pack, axis=1) * seqm_ref[...]
                k_dec = jnp.concatenate([k_dec] * pack, axis=1) * seqm_ref[...]
            o = _dot(q_dec.astype(BF16), s_all.astype(BF16)) + _dot(att.astype(BF16), vb)
            last = jnp.concatenate([b_end[0:pack]] * (GLA_DK // pack), axis=0) if pack > 1 else (
                jnp.broadcast_to(b_end[0:1], (GLA_DK, GLA_DK)))
            col = jnp.exp(_dot_nt_exact01(eye, last))
            dec = jnp.concatenate([jnp.broadcast_to(col[:, i:i + 1], (GLA_DK, GLA_DV)) for i in range(pack)], axis=0)
            results.append((wk, o, dec * s_all + _dot_tn(k_dec.astype(BF16), vb)))
        for wk, o, s_new in results:
            hh = wk["hh"]
            for t in range(tpv):
                o_ref[hh * tpv + t, wk["rows"], :] = o[:, t * LANES:(t + 1) * LANES]
            for i, j in enumerate(wk["seqs"]):
                s_ref[j, hh] = s_new[i * GLA_DK:(i + 1) * GLA_DK]

    if pack == 1:
        def per_pair(jp, carry):
            j = 2 * jp
            problems([(pl.ds(j, p, stride=SUBLANES), [j]), (pl.ds(j + 1, p, stride=SUBLANES), [j + 1])])
            return carry

        lax.fori_loop(0, SUBLANES // 2, per_pair, 0)
    else:
        problems([(slice(None), list(range(SUBLANES)))])


def gla_chunks(p_tiles, g_tiles, s0, nb, seq, c, pack):
    rows = p_tiles.shape[1]
    pr = _Problem(c, pack)
    wst = pr.f32(np.concatenate([pr.tri] + pr.pivots + [pr.whole], axis=0))
    nkt = GLA_KEY // LANES
    nvt = GLA_VAL // LANES
    nc = seq // c
    tiles = lambda nt, blk: pl.BlockSpec((nt, c * SUBLANES, LANES), lambda b, n: (blk, b * nc + n, 0))
    const = lambda x: pl.BlockSpec(x.shape, lambda b, n: (0,) * x.ndim)
    state = pl.BlockSpec((SUBLANES, GLA_HEADS, GLA_DK, GLA_DV), lambda b, n: (b, 0, 0, 0))
    masks, seqm = pr.f32(pr.masks), pr.f32(pr.seq_lanes)
    o, s_out = pl.pallas_call(
        functools.partial(_gla_chunk_kernel, p=pr.p, pack=pack), grid=(nb // SUBLANES, nc),
        in_specs=[tiles(nkt, 0), tiles(nkt, 1), tiles(nvt, 1), tiles(nkt, 0),
                  const(wst), const(masks), const(seqm), state],
        out_specs=[tiles(nvt, 0), state],
        out_shape=(SDS((nvt, rows, LANES), F32), SDS((nb, GLA_HEADS, GLA_DK, GLA_DV), F32)),
        compiler_params=_cparams(2), name="gla_chunks",
    )(p_tiles, p_tiles, p_tiles, g_tiles, wst, masks, seqm, s0)
    return o, s_out


def _chunking(seq):
    return (CHUNK, 1) if seq % CHUNK == 0 else (seq, SUBLANES)


def gla_layer(h, nb, seq, gain, s0, w_in, w_gate_up, b_gate, norm_o, w_out):
    c, pack = _chunking(seq)
    n_main = 2 * GLA_KEY + 2 * GLA_VAL
    p_tiles = norm_matmul(h, gain, w_in[:, :n_main].astype(BF16), GLA_TN)
    w_gl = jnp.pad(w_in[:, n_main:], ((0, 0), (0, LANES - GLA_RANK))).astype(BF16)
    w_gate_pad = jnp.pad(w_gate_up, ((0, LANES - GLA_RANK), (0, 0))).astype(BF16)
    g_tiles = gla_gate(h, gain, w_gl, w_gate_pad, b_gate)
    o_tiles, s_out = gla_chunks(p_tiles, g_tiles, s0, nb, seq, c, pack)
    h = gated_out(o_tiles, p_tiles, 2, norm_o, w_out.astype(BF16), h, GLA_HEADS, GLA_DV, TM)
    return h, s_out


def _gdn_conv_kernel(x_ref, cw_ref, c0_ref, o_ref, nc_ref, xp_ref, *, tm):
    grp = pl.program_id(0)
    i = pl.program_id(2)
    nb = SUBLANES
    halo = (GDN_CONV - 1) * nb
    scale = jnp.where(grp == 0, GDN_DK ** -0.5, 1.0)
    for t in range(x_ref.shape[0]):
        lanes = slice(t * LANES, (t + 1) * LANES)

        @pl.when(i == 0)
        def _():
            xp_ref[t, 0:halo, :] = c0_ref[:, lanes]

        xp_ref[t, halo:halo + tm, :] = x_ref[t]
        acc = cw_ref[0:1, lanes] * xp_ref[t, 0:tm, :]
        for j in range(1, GDN_CONV):
            acc = acc + cw_ref[j:j + 1, lanes] * xp_ref[t, j * nb:j * nb + tm, :]
        y = acc * _sigmoid(acc)

        inv = lax.rsqrt(jnp.sum(y * y, axis=-1, keepdims=True) + NORM_EPS) * scale
        o_ref[t] = y * jnp.where(grp < 2, inv, 1.0)
        tail = xp_ref[t, tm:tm + halo, :]
        xp_ref[t, 0:halo, :] = tail

        @pl.when(i == pl.num_programs(2) - 1)
        def _():
            nc_ref[:, lanes] = tail


def gdn_conv(p_tiles, conv_w, conv0, nb):
    rows = p_tiles.shape[1]
    ngrp = nb // SUBLANES
    halo = (GDN_CONV - 1) * SUBLANES
    tm = min(TM, rows // ngrp)
    nblk = rows // ngrp // tm
    gt = GDN_TN // LANES
    ng = GDN_CONV_DIM // GDN_TN
    assert tm >= halo
    return pl.pallas_call(
        functools.partial(_gdn_conv_kernel, tm=tm), grid=(ng, ngrp, nblk),
        in_specs=[pl.BlockSpec((gt, tm, LANES), lambda j, g, i: (j, g * nblk + i, 0)),
                  pl.BlockSpec((GDN_CONV, GDN_TN), lambda j, g, i: (0, j)),
                  pl.BlockSpec((halo, GDN_TN), lambda j, g, i: (g, j))],
        out_specs=[pl.BlockSpec((gt, tm, LANES), lambda j, g, i: (j, g * nblk + i, 0)),
                   pl.BlockSpec((halo, GDN_TN), lambda j, g, i: (g, j))],
        out_shape=(SDS((ng * gt, rows, LANES), F32), SDS((ngrp * halo, GDN_CONV_DIM), F32)),
        scratch_shapes=[pltpu.VMEM((gt, halo + tm, LANES), F32)],
        compiler_params=_cparams(3), name="gdn_conv",
    )(p_tiles, conv_w, conv0)


def _gdn_chunk_kernel(q_ref, k_ref, v_ref, ba_ref, alog_ref, dtb_ref, tril_ref, mask_ref, maskb_ref, seqm_ref,
                      s0_ref, o_ref, s_ref, *, p, pack):
    @pl.when(pl.program_id(2) == 0)
    def _():
        s_ref[...] = s0_ref[...]

    tri = tril_ref[0:p, :]
    eye_l = _eye(LANES)
    eye_c = mask_ref[0]
    strict = tri - eye_c
    nlv = mask_ref.shape[0]
    rep = GDN_V_HEADS // GDN_QK_HEADS

    def load_heads(rows, seqs):
        ba = ba_ref[0, rows, :]
        beta_all = _sigmoid(ba)
        g_all = -jnp.exp(alog_ref[0]) * _softplus(ba + dtb_ref[0])
        gcl = _dot_exact01(tril_ref[...], g_all)
        gc_all, ge_all = gcl[0:p], gcl[p:2 * p]
        gr_all = _dot_nt_exact01(eye_l, gc_all)
        heads = []
        for qh in range(GDN_HALF // rep):
            q = q_ref[qh, rows, :]
            k = k_ref[qh, rows, :]
            kb = k.astype(BF16)
            kq = _dot_nt(jnp.concatenate([kb, q.astype(BF16)], axis=0), kb)
            kk, qk = kq[:p], kq[p:]
            for j in range(rep):
                hh = qh * rep + j
                beta = beta_all[:, hh:hh + 1]
                g_col = gc_all[:, GDN_HALF + hh:GDN_HALF + hh + 1]
                g_end = ge_all[:, GDN_HALF + hh:GDN_HALF + hh + 1]
                g_row = gr_all[GDN_HALF + hh:GDN_HALF + hh + 1, :]
                decay = tri * jnp.exp(jnp.minimum(g_col - g_row, 0.0))
                m = strict * (kk * decay * beta)
                heads.append(dict(hh=hh, rows=rows, seqs=seqs, q=q, k=k, v=v_ref[hh, rows, :], beta=beta,
                                  g_col=g_col, g_end=g_end,
                                  s=jnp.concatenate([s_ref[i, hh] for i in seqs], axis=0),
                                  mb=m.astype(BF16), qkd=(qk * decay).astype(BF16),
                                  t=eye_c - mask_ref[1] * m))
        return heads

    def solve(heads):
        for lv in range(2, nlv):
            for hd in heads:
                tb = hd["t"].astype(BF16)
                hd["t"] = hd["t"] - _dot(_dot(tb, maskb_ref[lv] * hd["mb"]).astype(BF16), tb)
        for hd in heads:
            e_g = jnp.exp(hd["g_col"])
            rhs = jnp.concatenate([hd["v"] * hd["beta"], hd["k"] * (hd["beta"] * e_g)], axis=1).astype(BF16)
            hd["uw"] = _dot(hd["t"].astype(BF16), rhs)
            hd["q_dec"] = hd["q"] * e_g
        for hd in heads:
            uw = hd["uw"]
            lhs = jnp.concatenate([uw[:, GDN_DV:], hd["q_dec"]], axis=0)
            if pack > 1:
                lhs = jnp.concatenate([lhs] * pack, axis=1) * jnp.concatenate([seqm_ref[...]] * 2, axis=0)
            ws = _dot(lhs.astype(BF16), hd["s"].astype(BF16))
            hd["vnb"] = (uw[:, :GDN_DV] - ws[:p]).astype(BF16)
            hd["o_inter"] = ws[p:]
        for hd in heads:
            g_end = hd["g_end"]
            k_dec = hd["k"] * jnp.exp(g_end - hd["g_col"])
            if pack > 1:
                k_dec = jnp.concatenate([k_dec] * pack, axis=1) * seqm_ref[...]
            hd["o"] = hd["o_inter"] + _dot(hd["qkd"], hd["vnb"])
            dec = jnp.concatenate([jnp.broadcast_to(jnp.exp(g_end[i:i + 1, :]), (GDN_DK, GDN_DV))
                                   for i in range(pack)], axis=0)
            hd["s_new"] = dec * hd["s"] + _dot_tn(k_dec.astype(BF16), hd["vnb"])
        for hd in heads:
            o_ref[hd["hh"], hd["rows"], :] = hd["o"]
            for i, j in enumerate(hd["seqs"]):
                s_ref[j, hd["hh"]] = hd["s_new"][i * GDN_DK:(i + 1) * GDN_DK]

    if pack == 1:
        def per_pair(jp, carry):
            b = 2 * jp
            solve(load_heads(pl.ds(b, p, stride=SUBLANES), [b])
                  + load_heads(pl.ds(b + 1, p, stride=SUBLANES), [b + 1]))
            return carry

        lax.fori_loop(0, SUBLANES // 2, per_pair, 0)
    else:
        solve(load_heads(slice(None), list(range(SUBLANES))))


def gdn_chunks(qkv_tiles, p_tiles, a_log, dt_bias, s0, nb, seq, c, pack):
    ba0 = (GDN_CONV_DIM + GDN_VAL) // LANES
    rows = qkv_tiles.shape[1]
    nqk = GDN_HALF // (GDN_V_HEADS // GDN_QK_HEADS)
    nc = seq // c
    pr = _Problem(c, pack)
    pad = lambda x: jnp.pad(x.reshape(2, 1, GDN_HALF), ((0, 0), (0, 0), (GDN_HALF, LANES - 2 * GDN_HALF)))
    masks, seqm = pr.f32(pr.masks), pr.f32(pr.seq_lanes)
    tril = pr.f32(np.concatenate([pr.tri, pr.whole], axis=0))
    tiles = lambda nt, blk: pl.BlockSpec((nt, c * SUBLANES, LANES), lambda b, hf, n: (blk(hf), b * nc + n, 0))
    const = lambda x: pl.BlockSpec(x.shape, lambda b, hf, n: (0,) * x.ndim)
    state = pl.BlockSpec((SUBLANES, GDN_HALF, GDN_DK, GDN_DV), lambda b, hf, n: (b, hf, 0, 0))
    o, s_out = pl.pallas_call(
        functools.partial(_gdn_chunk_kernel, p=pr.p, pack=pack), grid=(nb // SUBLANES, 2, nc),
        in_specs=[tiles(nqk, lambda hf: hf), tiles(nqk, lambda hf: 2 + hf),
                  tiles(GDN_HALF, lambda hf: 2 + hf), tiles(1, lambda hf: ba0 + hf),
                  pl.BlockSpec((1, 1, LANES), lambda b, hf, n: (hf, 0, 0)),
                  pl.BlockSpec((1, 1, LANES), lambda b, hf, n: (hf, 0, 0)),
                  const(tril), const(masks), const(masks), const(seqm), state],
        out_specs=[tiles(GDN_HALF, lambda hf: hf), state],
        out_shape=(SDS((GDN_V_HEADS, rows, LANES), F32), SDS((nb, GDN_V_HEADS, GDN_DK, GDN_DV), F32)),
        compiler_params=_cparams(3, vmem_mib=56), name="gdn_chunks",
    )(qkv_tiles, qkv_tiles, qkv_tiles, p_tiles, pad(a_log), pad(dt_bias), tril, masks, masks.astype(BF16), seqm, s0)
    return o, s_out


def gdn_layer(h, nb, seq, gain, s0, conv0, w_in, conv_w, a_log, dt_bias, norm_o, w_out):
    c, pack = _chunking(seq)
    n_main = GDN_CONV_DIM + GDN_VAL
    w_b = w_in[:, n_main:n_main + GDN_V_HEADS].reshape(D_MODEL, 2, GDN_HALF)
    w_a = w_in[:, n_main + GDN_V_HEADS:].reshape(D_MODEL, 2, GDN_HALF)
    w_ba = jnp.pad(jnp.concatenate([w_b, w_a], axis=2), ((0, 0), (0, 0), (0, LANES - 2 * GDN_HALF)))
    w_all = jnp.concatenate([w_in[:, :n_main], w_ba.reshape(D_MODEL, 2 * LANES)], axis=1).astype(BF16)
    p_tiles = norm_matmul(h, gain, w_all, GDN_PROJ_TN)
    ngrp = nb // SUBLANES
    conv0_tm = conv0.reshape(ngrp, SUBLANES, GDN_CONV - 1, GDN_CONV_DIM).transpose(0, 2, 1, 3)
    conv0_tm = conv0_tm.reshape(ngrp * (GDN_CONV - 1) * SUBLANES, GDN_CONV_DIM)
    qkv_tiles, new_conv = gdn_conv(p_tiles, conv_w, conv0_tm, nb)
    o_tiles, s_out = gdn_chunks(qkv_tiles, p_tiles, a_log, dt_bias, s0, nb, seq, c, pack)
    h = gated_out(o_tiles, p_tiles, 2, norm_o, w_out.astype(BF16), h, GDN_V_HEADS, GDN_DV, TM // 2)
    new_conv = new_conv.reshape(ngrp, GDN_CONV - 1, SUBLANES, GDN_CONV_DIM)
    return h, s_out, new_conv.transpose(0, 2, 1, 3).reshape(nb, GDN_CONV - 1, GDN_CONV_DIM)


def _trunk(x, s5_re, s5_im, gla_s, gdn_s, gdn_conv_s, w):
    nb, seq, d = x.shape
    h, hn = norm_in(x, w["norm_mix"][0])
    h, s5r0, s5i0 = s5_layer(h, hn, nb, s5_re[0], s5_im[0], w["s5_a_re"][0], w["s5_a_im"][0],
                             w["s5_log_dt"][0], w["s5_b_re"][0], w["s5_b_im"][0], w["s5_c_re"][0],
                             w["s5_c_im"][0], w["s5_d"][0], w["s5_w_glu"][0])
    h = ffn(h, w["norm_ffn"][0], w["w_up"][0], w["w_down"][0])
    h, gla_o = gla_layer(h, nb, seq, w["norm_mix"][1], gla_s[0], w["gla_w_in"][0], w["gla_w_gate_up"][0],
                         w["gla_b_gate"][0], w["gla_norm"][0], w["gla_w_out"][0])
    h = ffn(h, w["norm_ffn"][1], w["w_up"][1], w["w_down"][1])
    h, gdn_o, conv_o = gdn_layer(h, nb, seq, w["norm_mix"][2], gdn_s[0], gdn_conv_s[0], w["gdn_w_in"][0],
                                 w["gdn_conv_w"][0], w["gdn_a_log"][0], w["gdn_dt_bias"][0],
                                 w["gdn_norm"][0], w["gdn_w_out"][0])
    h, hn = ffn(h, w["norm_ffn"][2], w["w_up"][2], w["w_down"][2], next_gain=w["norm_mix"][3])
    h, s5r1, s5i1 = s5_layer(h, hn, nb, s5_re[1], s5_im[1], w["s5_a_re"][1], w["s5_a_im"][1],
                             w["s5_log_dt"][1], w["s5_b_re"][1], w["s5_b_im"][1], w["s5_c_re"][1],
                             w["s5_c_im"][1], w["s5_d"][1], w["s5_w_glu"][1])
    h = ffn(h, w["norm_ffn"][3], w["w_up"][3], w["w_down"][3])
    y = norm_out(h, w["norm_final"], nb, seq)
    return (y, jnp.stack([s5r0, s5r1]), jnp.stack([s5i0, s5i1]), gla_o[None], gdn_o[None], conv_o[None])


def kernel(x_prompt, x_sample, state_s5_re, state_s5_im, state_gla, state_gdn, state_gdn_conv, norm_mix, norm_ffn, norm_final, w_up, w_down, s5_a_re, s5_a_im, s5_log_dt, s5_b_re, s5_b_im, s5_c_re, s5_c_im, s5_d, s5_w_glu, gla_w_in, gla_w_gate_up, gla_b_gate, gla_norm, gla_w_out, gdn_w_in, gdn_conv_w, gdn_a_log, gdn_dt_bias, gdn_norm, gdn_w_out):
    w = dict(norm_mix=norm_mix, norm_ffn=norm_ffn, norm_final=norm_final,
             w_up=w_up.astype(BF16), w_down=w_down.astype(BF16),
             s5_a_re=s5_a_re, s5_a_im=s5_a_im, s5_log_dt=s5_log_dt, s5_b_re=s5_b_re, s5_b_im=s5_b_im,
             s5_c_re=s5_c_re, s5_c_im=s5_c_im, s5_d=s5_d, s5_w_glu=s5_w_glu,
             gla_w_in=gla_w_in, gla_w_gate_up=gla_w_gate_up, gla_b_gate=gla_b_gate, gla_norm=gla_norm,
             gla_w_out=gla_w_out, gdn_w_in=gdn_w_in, gdn_conv_w=gdn_conv_w, gdn_a_log=gdn_a_log,
             gdn_dt_bias=gdn_dt_bias, gdn_norm=gdn_norm, gdn_w_out=gdn_w_out)
    bp = x_prompt.shape[0]
    dt = x_prompt.dtype
    z_s5 = jnp.zeros((state_s5_re.shape[0], bp) + state_s5_re.shape[2:], dt)
    z_gla = jnp.zeros((state_gla.shape[0], bp) + state_gla.shape[2:], dt)
    z_gdn = jnp.zeros((state_gdn.shape[0], bp) + state_gdn.shape[2:], dt)
    z_conv = jnp.zeros((state_gdn_conv.shape[0], bp) + state_gdn_conv.shape[2:], dt)
    out_p = _trunk(x_prompt, z_s5, z_s5, z_gla, z_gdn, z_conv, w)
    out_s = _trunk(x_sample, state_s5_re, state_s5_im, state_gla, state_gdn, state_gdn_conv, w)
    return (out_p[0], out_s[0]) + out_p[1:] + out_s[1:]
```

```python
import functools
import math

import numpy as np
import jax
import jax.numpy as jnp
from jax import lax
from jax.experimental import pallas as pl
from jax.experimental.pallas import tpu as pltpu

F32 = jnp.float32
BF16 = jnp.bfloat16
SDS = jax.ShapeDtypeStruct

D_MODEL = 1024
NORM_EPS = 1e-6

S5_GROUP = 16
S5_STATE = 64
S5_GROUPS = D_MODEL // S5_GROUP
S5_GB = 16
S5_NGB = S5_GROUPS // S5_GB
S5_BC = S5_GB * S5_GROUP
S5_BS = S5_GB * S5_STATE

GLA_HEADS = 4
GLA_DK = 128
GLA_DV = 256
GLA_KEY = GLA_HEADS * GLA_DK
GLA_VAL = GLA_HEADS * GLA_DV
GLA_RANK = 16
GLA_TAU = 16.0
GLA_TN = 1024

GDN_DK = 128
GDN_DV = 128
GDN_QK_HEADS = 8
GDN_V_HEADS = 16
GDN_KEY = GDN_QK_HEADS * GDN_DK
GDN_VAL = GDN_V_HEADS * GDN_DV
GDN_HALF = GDN_V_HEADS // 2
GDN_CONV = 4
GDN_CONV_DIM = 2 * GDN_KEY + GDN_VAL
GDN_TN = 1024
GDN_PROJ_TN = 1280

PACK_GROUPS = 2
CHUNK = 128
LANES = 128
SUBLANES = 8
TM = 1024
FFN_TF = 1024
MIB = 1024 * 1024


def _cparams(n_axes, vmem_mib=48):
    return pltpu.CompilerParams(dimension_semantics=("arbitrary",) * n_axes,
                                vmem_limit_bytes=vmem_mib * MIB)


def _rms(x, gain):
    ms = jnp.mean(x * x, axis=-1, keepdims=True)
    return x * lax.rsqrt(ms + NORM_EPS) * gain


def _sigmoid(x):
    return 1.0 / (1.0 + jnp.exp(-x))


def _softplus(x):
    return jnp.maximum(x, 0.0) + jnp.log1p(jnp.exp(-jnp.abs(x)))


def _gelu_tanh(x):
    c = math.sqrt(2.0 / math.pi)
    return x * (0.5 * (1.0 + jnp.tanh(c * (x + 0.044715 * (x * x * x)))))


def _dot(a, b):
    return jnp.dot(a, b, preferred_element_type=F32)


def _dot_nt(a, b):
    return lax.dot_general(a, b, (((1,), (1,)), ((), ())), preferred_element_type=F32)


def _dot_tn(a, b):
    return lax.dot_general(a, b, (((0,), (0,)), ((), ())), preferred_element_type=F32)


def _eye(n):
    return (lax.broadcasted_iota(jnp.int32, (n, n), 0)
            == lax.broadcasted_iota(jnp.int32, (n, n), 1)).astype(F32)


def _split3(x):
    x1 = x.astype(BF16)
    r1 = x - x1.astype(F32)
    x2 = r1.astype(BF16)
    x3 = (r1 - x2.astype(F32)).astype(BF16)
    return x1, x2, x3


def _dot_exact01(m01, x):
    mb = m01.astype(BF16)
    x1, x2, x3 = _split3(x)
    return _dot(mb, x1) + _dot(mb, x2) + _dot(mb, x3)


def _dot_nt_exact01(m01, x):
    mb = m01.astype(BF16)
    x1, x2, x3 = _split3(x)
    return _dot_nt(mb, x1) + _dot_nt(mb, x2) + _dot_nt(mb, x3)


def _norm_in_kernel(x_ref, g_ref, h_ref, hn_ref):
    for j in range(SUBLANES):
        x = x_ref[j]
        h_ref[:, j, :] = x
        hn_ref[:, j, :] = _rms(x, g_ref[...])


def norm_in(x, gain):
    nb, seq, d = x.shape
    tt = min(seq, TM // SUBLANES)
    nblk = seq // tt
    out = SDS((nb // SUBLANES * seq, SUBLANES, d), F32)
    h, hn = pl.pallas_call(
        _norm_in_kernel, grid=(nb // SUBLANES, nblk),
        in_specs=[pl.BlockSpec((SUBLANES, tt, d), lambda g, i: (g, i, 0)),
                  pl.BlockSpec((1, d), lambda g, i: (0, 0))],
        out_specs=[pl.BlockSpec((tt, SUBLANES, d), lambda g, i: (g * nblk + i, 0, 0))] * 2,
        out_shape=(out, out), compiler_params=_cparams(2), name="norm_in",
    )(x, gain.reshape(1, d))
    return h.reshape(seq * nb, d), hn.reshape(seq * nb, d)


def _norm_out_kernel(h_ref, g_ref, y_ref):
    for j in range(SUBLANES):
        y_ref[j] = _rms(h_ref[:, j, :], g_ref[...])


def norm_out(h, gain, nb, seq):
    d = h.shape[1]
    tt = min(seq, TM // SUBLANES)
    nblk = seq // tt
    return pl.pallas_call(
        _norm_out_kernel, grid=(nb // SUBLANES, nblk),
        in_specs=[pl.BlockSpec((tt, SUBLANES, d), lambda g, i: (g * nblk + i, 0, 0)),
                  pl.BlockSpec((1, d), lambda g, i: (0, 0))],
        out_specs=pl.BlockSpec((SUBLANES, tt, d), lambda g, i: (g, i, 0)),
        out_shape=SDS((nb, seq, d), F32), compiler_params=_cparams(2), name="norm_out",
    )(h.reshape(nb // SUBLANES * seq, SUBLANES, d), gain.reshape(1, d))


def _norm_matmul_kernel(h_ref, g_ref, w_ref, o_ref, hn_ref):
    @pl.when(pl.program_id(1) == 0)
    def _():
        hn_ref[...] = _rms(h_ref[...], g_ref[...]).astype(BF16)

    res = _dot(hn_ref[...], w_ref[...])
    for t in range(o_ref.shape[0]):
        o_ref[t] = res[:, t * LANES:(t + 1) * LANES]


def norm_matmul(h, gain, w, tn):
    rows, d = h.shape
    n = w.shape[1]
    return pl.pallas_call(
        _norm_matmul_kernel, grid=(rows // TM, n // tn),
        in_specs=[pl.BlockSpec((TM, d), lambda i, j: (i, 0)),
                  pl.BlockSpec((1, d), lambda i, j: (0, 0)),
                  pl.BlockSpec((d, tn), lambda i, j: (0, j))],
        out_specs=pl.BlockSpec((tn // LANES, TM, LANES), lambda i, j: (j, i, 0)),
        out_shape=SDS((n // LANES, rows, LANES), F32),
        scratch_shapes=[pltpu.VMEM((TM, d), BF16)],
        compiler_params=_cparams(2), name="norm_matmul",
    )(h, gain.reshape(1, d), w)


def _ffn_kernel(h_ref, hnext_ref, g_ref, wu_ref, wd_ref, ng_ref, *rest, emit_norm):
    o_ref, hn_ref, acc_ref = rest[0], rest[-2], rest[-1]
    i = pl.program_id(0)
    j = pl.program_id(1)
    last = pl.num_programs(1) - 1
    slot = i % 2

    @pl.when((i == 0) & (j == 0))
    def _():
        hn_ref[0] = _rms(h_ref[...], g_ref[...]).astype(BF16)

    def partial():
        a = jnp.square(jnp.maximum(_dot(hn_ref[slot], wu_ref[...]), 0.0)).astype(BF16)
        return _dot(a, wd_ref[...])

    @pl.when(j == 0)
    def _():
        acc_ref[...] = partial()

    @pl.when((j > 0) & (j < last))
    def _():
        acc_ref[...] += partial()

    @pl.when(j == last)
    def _():
        hn_ref[1 - slot] = _rms(hnext_ref[...], g_ref[...]).astype(BF16)
        out = h_ref[...] + acc_ref[...] + partial()
        o_ref[...] = out
        if emit_norm:
            rest[1][...] = _rms(out, ng_ref[...])


def ffn(h, gain, w_up, w_down, next_gain=None):
    rows, d = h.shape
    f = w_up.shape[1]
    nblk = rows // TM
    assert f // FFN_TF >= 2
    emit_norm = next_gain is not None
    row_block = pl.BlockSpec((TM, d), lambda i, j: (i, 0))
    vec = pl.BlockSpec((1, d), lambda i, j: (0, 0))
    out = pl.pallas_call(
        functools.partial(_ffn_kernel, emit_norm=emit_norm), grid=(nblk, f // FFN_TF),
        in_specs=[row_block,
                  pl.BlockSpec((TM, d), lambda i, j: (jnp.minimum(i + 1, nblk - 1), 0)),
                  vec,
                  pl.BlockSpec((d, FFN_TF), lambda i, j: (0, j)),
                  pl.BlockSpec((FFN_TF, d), lambda i, j: (j, 0)),
                  vec],
        out_specs=[row_block] * (2 if emit_norm else 1),
        out_shape=[SDS((rows, d), F32)] * (2 if emit_norm else 1),
        scratch_shapes=[pltpu.VMEM((2, TM, d), BF16), pltpu.VMEM((TM, d), F32)],
        compiler_params=_cparams(2, vmem_mib=56 if emit_norm else 48), name="ffn",
    )(h, h, gain.reshape(1, d), w_up, w_down, (next_gain if emit_norm else gain).reshape(1, d))
    return (out[0], out[1]) if emit_norm else out[0]


def _gated_out_kernel(o_ref, z_ref, gn_ref, w_ref, h_ref, out_ref, *, nheads, hd):
    tph = hd // LANES
    kstep = 2 * LANES
    acc = h_ref[...]
    for k0 in range(0, nheads * hd, kstep):
        tiles = range(k0 // LANES, (k0 + kstep) // LANES)
        o = jnp.concatenate([o_ref[t] for t in tiles], axis=1)
        z = jnp.concatenate([z_ref[t] for t in tiles], axis=1)
        on = jnp.concatenate([_rms(o[:, i * hd:(i + 1) * hd], gn_ref[...]) for i in range(kstep // hd)], axis=1)
        a = (on * (z * _sigmoid(z))).astype(BF16)
        acc = acc + _dot(a, w_ref[k0:k0 + kstep, :])
    out_ref[...] = acc


def gated_out(o_tiles, p_tiles, z_block, gain, w_out, h, nheads, hd, tm):
    rows, d = h.shape
    kdim = nheads * hd
    nt = kdim // LANES
    return pl.pallas_call(
        functools.partial(_gated_out_kernel, nheads=nheads, hd=hd), grid=(rows // tm,),
        in_specs=[pl.BlockSpec((nt, tm, LANES), lambda i: (0, i, 0)),
                  pl.BlockSpec((nt, tm, LANES), lambda i: (z_block, i, 0)),
                  pl.BlockSpec((1, hd), lambda i: (0, 0)),
                  pl.BlockSpec((kdim, d), lambda i: (0, 0)),
                  pl.BlockSpec((tm, d), lambda i: (i, 0))],
        out_specs=pl.BlockSpec((tm, d), lambda i: (i, 0)),
        out_shape=SDS((rows, d), F32),
        compiler_params=_cparams(1), name="gated_out",
    )(o_tiles, p_tiles, gain.reshape(1, hd), w_out, h)


def _s5_discretize_kernel(are_ref, aim_ref, ldt_ref, bre_ref, bim_ref,
                          abr_ref, abi_ref, bbr_ref, bbi_ref):
    a_re, a_im = are_ref[...], aim_ref[...]
    dt = jnp.exp(ldt_ref[...])
    mag = jnp.exp(a_re * dt)
    ab_re = mag * jnp.cos(a_im * dt)
    ab_im = mag * jnp.sin(a_im * dt)
    den = a_re * a_re + a_im * a_im
    c_re = ((ab_re - 1.0) * a_re + ab_im * a_im) / den
    c_im = (ab_im * a_re - (ab_re - 1.0) * a_im) / den
    abr_ref[...] = ab_re
    abi_ref[...] = ab_im
    bbr_ref[...] = c_re * bre_ref[...] - c_im * bim_ref[...]
    bbi_ref[...] = c_re * bim_ref[...] + c_im * bre_ref[...]


def s5_discretize(a_re, a_im, log_dt, b_re, b_im):
    g, p, c = S5_GROUPS, S5_STATE, S5_GROUP
    expand = lambda v: jnp.broadcast_to(v[..., None], (g, p, c)).reshape(g, p * c)
    ldt = jnp.broadcast_to(log_dt[:, None], (g, p * c))
    shp = SDS((g, p * c), F32)
    ab_re, ab_im, bb_re, bb_im = pl.pallas_call(
        _s5_discretize_kernel, out_shape=(shp, shp, shp, shp), name="s5_discretize",
    )(expand(a_re), expand(a_im), ldt, b_re.reshape(g, p * c), b_im.reshape(g, p * c))
    ab_re = ab_re.reshape(g, p, c)[:, :, 0].reshape(S5_NGB, 1, S5_BS)
    ab_im = ab_im.reshape(g, p, c)[:, :, 0].reshape(S5_NGB, 1, S5_BS)
    eye = jnp.eye(S5_GB, dtype=F32)

    def block_diag_in(bb):
        bb = bb.reshape(S5_NGB, S5_GB, p, c)
        return jnp.einsum("bgpc,gh->bgchp", bb, eye).reshape(S5_NGB, S5_BC, S5_BS)

    b_blk = jnp.concatenate([block_diag_in(bb_re.reshape(g, p, c)),
                             block_diag_in(bb_im.reshape(g, p, c))], axis=-1).astype(BF16)
    return ab_re, ab_im, b_blk


def s5_block_diag_out(c_par):
    eye = jnp.eye(S5_GB, dtype=F32)
    cc = c_par.reshape(S5_NGB, S5_GB, S5_GROUP, S5_STATE)
    return jnp.einsum("bgcp,gh->bgphc", cc, eye).reshape(S5_NGB, S5_BS, S5_BC).astype(BF16)


def _s5_scan_kernel(u_ref, b_ref, cre_ref, cim_ref, are_ref, aim_ref, d_ref, s0r_ref, s0i_ref,
                    g_ref, slr_ref, sli_ref, xr_ref, xi_ref, str_ref, sti_ref, *, ngrp, tc):
    n = pl.program_id(1)

    @pl.when(n == 0)
    def _():
        str_ref[...] = s0r_ref[...]
        sti_ref[...] = s0i_ref[...]

    sub = SUBLANES
    a_re = jnp.broadcast_to(are_ref[0], (sub, S5_BS))
    a_im = jnp.broadcast_to(aim_ref[0], (sub, S5_BS))
    sb = 2 * LANES
    x_re = x_im = None
    for k in range(u_ref.shape[0] // sb):
        rows = slice(k * sb, (k + 1) * sb)
        u = u_ref[rows, :]
        ub = u.astype(BF16)
        xr_ref[rows, :] = _dot(ub, b_ref[0, :, :S5_BS])
        xi_ref[rows, :] = _dot(ub, b_ref[0, :, S5_BS:])
        for slab in range(k * sb // sub, (k + 1) * sb // sub):
            grp, t = divmod(slab, tc)
            srows = slice(grp * sub, (grp + 1) * sub)
            if t == 0:
                x_re, x_im = str_ref[srows, :], sti_ref[srows, :]
            r8 = slice(slab * sub, (slab + 1) * sub)
            x_re, x_im = (a_re * x_re - a_im * x_im + xr_ref[r8, :],
                          a_re * x_im + a_im * x_re + xi_ref[r8, :])
            xr_ref[r8, :] = x_re
            xi_ref[r8, :] = x_im
            if t == tc - 1:
                str_ref[srows, :] = x_re
                sti_ref[srows, :] = x_im
        y = _dot(xr_ref[rows, :].astype(BF16), cre_ref[0]) - _dot(xi_ref[rows, :].astype(BF16), cim_ref[0])
        g_ref[rows, :] = _gelu_tanh(y + d_ref[...] * u).astype(BF16)

    @pl.when(n == pl.num_programs(1) - 1)
    def _():
        slr_ref[...] = str_ref[...]
        sli_ref[...] = sti_ref[...]


def s5_scan(hn, nb, b_blk, c_re_blk, c_im_blk, ab_re, ab_im, d_skip, s0_re, s0_im):
    rows, d = hn.shape
    tc = min(rows // nb, TM // SUBLANES)
    ngrp = TM // (tc * SUBLANES)
    assert ngrp == 1 or ngrp * SUBLANES == nb
    st = SDS((nb, S5_GROUPS * S5_STATE), F32)
    return pl.pallas_call(
        functools.partial(_s5_scan_kernel, ngrp=ngrp, tc=tc), grid=(S5_NGB, rows // TM),
        in_specs=[pl.BlockSpec((TM, S5_BC), lambda gb, n: (n, gb)),
                  pl.BlockSpec((1, S5_BC, 2 * S5_BS), lambda gb, n: (gb, 0, 0)),
                  pl.BlockSpec((1, S5_BS, S5_BC), lambda gb, n: (gb, 0, 0)),
                  pl.BlockSpec((1, S5_BS, S5_BC), lambda gb, n: (gb, 0, 0)),
                  pl.BlockSpec((1, 1, S5_BS), lambda gb, n: (gb, 0, 0)),
                  pl.BlockSpec((1, 1, S5_BS), lambda gb, n: (gb, 0, 0)),
                  pl.BlockSpec((1, S5_BC), lambda gb, n: (0, gb)),
                  pl.BlockSpec((nb, S5_BS), lambda gb, n: (0, gb)),
                  pl.BlockSpec((nb, S5_BS), lambda gb, n: (0, gb))],
        out_specs=[pl.BlockSpec((TM, S5_BC), lambda gb, n: (n, gb)),
                   pl.BlockSpec((nb, S5_BS), lambda gb, n: (0, gb)),
                   pl.BlockSpec((nb, S5_BS), lambda gb, n: (0, gb))],
        out_shape=(SDS((rows, d), BF16), st, st),
        scratch_shapes=[pltpu.VMEM((TM, S5_BS), F32), pltpu.VMEM((TM, S5_BS), F32),
                        pltpu.VMEM((nb, S5_BS), F32), pltpu.VMEM((nb, S5_BS), F32)],
        compiler_params=_cparams(2), name="s5_scan",
    )(hn, b_blk, c_re_blk, c_im_blk, ab_re, ab_im, d_skip.reshape(1, d), s0_re, s0_im)


def _glu_out_kernel(g_ref, w_ref, h_ref, o_ref):
    gv = _dot(g_ref[...], w_ref[...])
    o_ref[...] = h_ref[...] + gv[:, :D_MODEL] * _sigmoid(gv[:, D_MODEL:])


def glu_out(g, w_glu, h):
    rows, d = h.shape
    tm = TM // 2
    return pl.pallas_call(
        _glu_out_kernel, grid=(rows // tm,),
        in_specs=[pl.BlockSpec((tm, d), lambda i: (i, 0)),
                  pl.BlockSpec((d, 2 * d), lambda i: (0, 0)),
                  pl.BlockSpec((tm, d), lambda i: (i, 0))],
        out_specs=pl.BlockSpec((tm, d), lambda i: (i, 0)),
        out_shape=SDS((rows, d), F32), compiler_params=_cparams(1), name="glu_out",
    )(g, w_glu, h)


def s5_layer(h, hn, nb, s0_re, s0_im, a_re, a_im, log_dt, b_re, b_im, c_re, c_im, d_skip, w_glu):
    ab_re, ab_im, b_blk = s5_discretize(a_re, a_im, log_dt, b_re, b_im)
    g, sl_re, sl_im = s5_scan(hn, nb, b_blk, s5_block_diag_out(c_re), s5_block_diag_out(c_im),
                              ab_re, ab_im, d_skip, s0_re.reshape(nb, -1), s0_im.reshape(nb, -1))
    shape = (nb, S5_GROUPS, S5_STATE)
    return glu_out(g, w_glu.astype(BF16), h), sl_re.reshape(shape), sl_im.reshape(shape)


class _Problem:
    def __init__(self, c, pack):
        p = c * pack
        r = np.arange(p)
        seq, time = r % pack, r // pack
        same = seq[:, None] == seq[None, :]
        self.p, self.pack = p, pack
        self.tri = same & (time[None, :] <= time[:, None])
        self.whole = same
        masks, pivots = [r[:, None] == r[None, :]], []
        for sz in [2 ** i for i in range(1, int(math.log2(p)) + 1)]:
            blk, off = r // sz, r % sz
            m = same & (blk[:, None] == blk[None, :]) & (off[:, None] >= sz // 2) & (off[None, :] < sz // 2)
            if not m.any():
                continue
            lower = same & (blk[:, None] == blk[None, :]) & (off[None, :] < sz // 2)
            piv = np.where(lower.any(1), (lower * r[None, :]).max(1), r)
            masks.append(m)
            pivots.append(self.tri[piv])
        self.masks = np.stack(masks)
        self.pivots = pivots
        self.seq_lanes = np.repeat(seq[:, None] == np.arange(pack)[None, :], LANES, axis=1)

    def f32(self, x):
        return jnp.asarray(np.asarray(x, np.float32))


def _gla_gate_kernel(h_ref, gn_ref, wgl_ref, w_ref, b_ref, o_ref):
    hn = _rms(h_ref[...], gn_ref[...]).astype(BF16)
    gl = _dot(hn, wgl_ref[...])
    x = _dot(gl.astype(BF16), w_ref[...]) + b_ref[...]
    g = -_softplus(-x) * (1.0 / GLA_TAU)
    for t in range(o_ref.shape[0]):
        o_ref[t] = g[:, t * LANES:(t + 1) * LANES]


def gla_gate(h, gain, w_gl_pad, w_gate_pad, b_gate):
    rows, d = h.shape
    nt = GLA_KEY // LANES
    return pl.pallas_call(
        _gla_gate_kernel, grid=(rows // TM,),
        in_specs=[pl.BlockSpec((TM, d), lambda i: (i, 0)),
                  pl.BlockSpec((1, d), lambda i: (0, 0)),
                  pl.BlockSpec((d, LANES), lambda i: (0, 0)),
                  pl.BlockSpec((LANES, GLA_KEY), lambda i: (0, 0)),
                  pl.BlockSpec((1, GLA_KEY), lambda i: (0, 0))],
        out_specs=pl.BlockSpec((nt, TM, LANES), lambda i: (0, i, 0)),
        out_shape=SDS((nt, rows, LANES), F32), compiler_params=_cparams(1), name="gla_gate",
    )(h, gain.reshape(1, d), w_gl_pad, w_gate_pad, b_gate.reshape(1, GLA_KEY))


def _gla_chunk_kernel(q_ref, k_ref, v_ref, g_ref, wst_ref, mask_ref, seqm_ref, s0_ref, o_ref, s_ref, *, p, pack):
    @pl.when(pl.program_id(1) == 0)
    def _():
        s_ref[...] = s0_ref[...]

    wst = wst_ref[...]
    eye = _eye(GLA_DK)
    nlev = mask_ref.shape[0] - 1
    tpv = GLA_DV // LANES

    def problems(rows_seqs):
        loaded = [(hh, rows, seqs, q_ref[hh, rows, :], k_ref[hh, rows, :],
                   [v_ref[hh * tpv + t, rows, :] for t in range(tpv)],
                   g_ref[hh, rows, :], [s_ref[j, hh] for j in seqs])
                  for rows, seqs in rows_seqs for hh in range(GLA_HEADS)]
        results = []
        work = []
        for hh, rows, seqs, q, k, v, g, ss in loaded:
            q = q * (GLA_DK ** -0.5)
            bp = _dot_exact01(wst, g)
            work.append(dict(hh=hh, rows=rows, seqs=seqs, q=q, k=k, v=v, ss=ss, bp=bp, b=bp[0:p],
                             att=mask_ref[0] * _dot_nt(q.astype(BF16), k.astype(BF16))))
        for lv in range(1, nlev + 1):
            for wk in work:
                e = jnp.exp(-jnp.abs(wk["b"] - wk["bp"][lv * p:(lv + 1) * p]))
                wk["att"] = wk["att"] + mask_ref[lv] * _dot_nt((wk["q"] * e).astype(BF16),
                                                               (wk["k"] * e).astype(BF16))
        for wk in work:
            q, k, v, ss, bp, b, att = (wk[n] for n in ("q", "k", "v", "ss", "bp", "b", "att"))
            vb = jnp.concatenate(v, axis=1).astype(BF16)
            b_end = bp[(nlev + 1) * p:(nlev + 2) * p]
            s_all = jnp.concatenate(ss, axis=0)
            q_dec = q * jnp.exp(b)
            k_dec = k * jnp.exp(b_end - b)
            if pack > 1:
                q_dec = jnp.concatenate([q_dec] * pack, axis=1) * seqm_ref[...]
                k_dec = jnp.concatenate([k_dec] * pack, axis=1) * seqm_ref[...]
            o = _dot(q_dec.astype(BF16), s_all.astype(BF16)) + _dot(att.astype(BF16), vb)
            last = jnp.concatenate([b_end[0:pack]] * (GLA_DK // pack), axis=0) if pack > 1 else (
                jnp.broadcast_to(b_end[0:1], (GLA_DK, GLA_DK)))
            col = jnp.exp(_dot_nt_exact01(eye, last))
            dec = jnp.concatenate([jnp.broadcast_to(col[:, i:i + 1], (GLA_DK, GLA_DV)) for i in range(pack)], axis=0)
            results.append((wk, o, dec * s_all + _dot_tn(k_dec.astype(BF16), vb)))
        for wk, o, s_new in results:
            hh = wk["hh"]
            for t in range(tpv):
                o_ref[hh * tpv + t, wk["rows"], :] = o[:, t * LANES:(t + 1) * LANES]
            for i, j in enumerate(wk["seqs"]):
                s_ref[j, hh] = s_new[i * GLA_DK:(i + 1) * GLA_DK]

    if pack == 1:
        def per_pair(jp, carry):
            j = 2 * jp
            problems([(pl.ds(j, p, stride=SUBLANES), [j]), (pl.ds(j + 1, p, stride=SUBLANES), [j + 1])])
            return carry

        lax.fori_loop(0, SUBLANES // 2, per_pair, 0)
    else:
        problems([(slice(i * p, (i + 1) * p), list(range(i * SUBLANES, (i + 1) * SUBLANES)))
                  for i in range(s_ref.shape[0] // SUBLANES)])


def gla_chunks(p_tiles, g_tiles, s0, nb, seq, c, pack):
    rows = p_tiles.shape[1]
    pr = _Problem(c, pack)
    wst = pr.f32(np.concatenate([pr.tri] + pr.pivots + [pr.whole], axis=0))
    nkt = GLA_KEY // LANES
    nvt = GLA_VAL // LANES
    nc = seq // c
    ns = SUBLANES * (PACK_GROUPS if pack > 1 else 1)
    assert nb % ns == 0
    tiles = lambda nt, blk: pl.BlockSpec((nt, c * ns, LANES), lambda b, n: (blk, b * nc + n, 0))
    const = lambda x: pl.BlockSpec(x.shape, lambda b, n: (0,) * x.ndim)
    state = pl.BlockSpec((ns, GLA_HEADS, GLA_DK, GLA_DV), lambda b, n: (b, 0, 0, 0))
    masks, seqm = pr.f32(pr.masks), pr.f32(pr.seq_lanes)
    o, s_out = pl.pallas_call(
        functools.partial(_gla_chunk_kernel, p=pr.p, pack=pack), grid=(nb // ns, nc),
        in_specs=[tiles(nkt, 0), tiles(nkt, 1), tiles(nvt, 1), tiles(nkt, 0),
                  const(wst), const(masks), const(seqm), state],
        out_specs=[tiles(nvt, 0), state],
        out_shape=(SDS((nvt, rows, LANES), F32), SDS((nb, GLA_HEADS, GLA_DK, GLA_DV), F32)),
        compiler_params=_cparams(2), name="gla_chunks",
    )(p_tiles, p_tiles, p_tiles, g_tiles, wst, masks, seqm, s0)
    return o, s_out


def _chunking(seq):
    return (CHUNK, 1) if seq % CHUNK == 0 else (seq, SUBLANES)


def gla_layer(h, nb, seq, gain, s0, w_in, w_gate_up, b_gate, norm_o, w_out):
    c, pack = _chunking(seq)
    n_main = 2 * GLA_KEY + 2 * GLA_VAL
    p_tiles = norm_matmul(h, gain, w_in[:, :n_main].astype(BF16), GLA_TN)
    w_gl = jnp.pad(w_in[:, n_main:], ((0, 0), (0, LANES - GLA_RANK))).astype(BF16)
    w_gate_pad = jnp.pad(w_gate_up, ((0, LANES - GLA_RANK), (0, 0))).astype(BF16)
    g_tiles = gla_gate(h, gain, w_gl, w_gate_pad, b_gate)
    o_tiles, s_out = gla_chunks(p_tiles, g_tiles, s0, nb, seq, c, pack)
    h = gated_out(o_tiles, p_tiles, 2, norm_o, w_out.astype(BF16), h, GLA_HEADS, GLA_DV, TM)
    return h, s_out


def _gdn_conv_kernel(x_ref, cw_ref, c0_ref, o_ref, nc_ref, xp_ref, *, tm):
    grp = pl.program_id(0)
    i = pl.program_id(2)
    nb = SUBLANES
    halo = (GDN_CONV - 1) * nb
    scale = jnp.where(grp == 0, GDN_DK ** -0.5, 1.0)
    for t in range(x_ref.shape[0]):
        lanes = slice(t * LANES, (t + 1) * LANES)

        @pl.when(i == 0)
        def _():
            xp_ref[t, 0:halo, :] = c0_ref[:, lanes]

        xp_ref[t, halo:halo + tm, :] = x_ref[t]
        acc = cw_ref[0:1, lanes] * xp_ref[t, 0:tm, :]
        for j in range(1, GDN_CONV):
            acc = acc + cw_ref[j:j + 1, lanes] * xp_ref[t, j * nb:j * nb + tm, :]
        y = acc * _sigmoid(acc)

        inv = lax.rsqrt(jnp.sum(y * y, axis=-1, keepdims=True) + NORM_EPS) * scale
        o_ref[t] = y * jnp.where(grp < 2, inv, 1.0)
        tail = xp_ref[t, tm:tm + halo, :]
        xp_ref[t, 0:halo, :] = tail

        @pl.when(i == pl.num_programs(2) - 1)
        def _():
            nc_ref[:, lanes] = tail


def gdn_conv(p_tiles, conv_w, conv0, nb):
    rows = p_tiles.shape[1]
    ngrp = nb // SUBLANES
    halo = (GDN_CONV - 1) * SUBLANES
    tm = min(TM, rows // ngrp)
    nblk = rows // ngrp // tm
    gt = GDN_TN // LANES
    ng = GDN_CONV_DIM // GDN_TN
    assert tm >= halo
    return pl.pallas_call(
        functools.partial(_gdn_conv_kernel, tm=tm), grid=(ng, ngrp, nblk),
        in_specs=[pl.BlockSpec((gt, tm, LANES), lambda j, g, i: (j, g * nblk + i, 0)),
                  pl.BlockSpec((GDN_CONV, GDN_TN), lambda j, g, i: (0, j)),
                  pl.BlockSpec((halo, GDN_TN), lambda j, g, i: (g, j))],
        out_specs=[pl.BlockSpec((gt, tm, LANES), lambda j, g, i: (j, g * nblk + i, 0)),
                   pl.BlockSpec((halo, GDN_TN), lambda j, g, i: (g, j))],
        out_shape=(SDS((ng * gt, rows, LANES), F32), SDS((ngrp * halo, GDN_CONV_DIM), F32)),
        scratch_shapes=[pltpu.VMEM((gt, halo + tm, LANES), F32)],
        compiler_params=_cparams(3), name="gdn_conv",
    )(p_tiles, conv_w, conv0)


def _gdn_chunk_kernel(q_ref, k_ref, v_ref, ba_ref, alog_ref, dtb_ref, tril_ref, mask_ref, maskb_ref, seqm_ref,
                      s0_ref, o_ref, s_ref, *, p, pack):
    @pl.when(pl.program_id(2) == 0)
    def _():
        s_ref[...] = s0_ref[...]

    tri = tril_ref[0:p, :]
    eye_l = _eye(LANES)
    eye_c = mask_ref[0]
    strict = tri - eye_c
    nlv = mask_ref.shape[0]
    rep = GDN_V_HEADS // GDN_QK_HEADS

    def load_heads(rows, seqs):
        ba = ba_ref[0, rows, :]
        beta_all = _sigmoid(ba)
        g_all = -jnp.exp(alog_ref[0]) * _softplus(ba + dtb_ref[0])
        gcl = _dot_exact01(tril_ref[...], g_all)
        gc_all, ge_all = gcl[0:p], gcl[p:2 * p]
        gr_all = _dot_nt_exact01(eye_l, gc_all)
        heads = []
        for qh in range(GDN_HALF // rep):
            q = q_ref[qh, rows, :]
            k = k_ref[qh, rows, :]
            kb = k.astype(BF16)
            kq = _dot_nt(jnp.concatenate([kb, q.astype(BF16)], axis=0), kb)
            kk, qk = kq[:p], kq[p:]
            for j in range(rep):
                hh = qh * rep + j
                beta = beta_all[:, hh:hh + 1]
                g_col = gc_all[:, GDN_HALF + hh:GDN_HALF + hh + 1]
                g_end = ge_all[:, GDN_HALF + hh:GDN_HALF + hh + 1]
                g_row = gr_all[GDN_HALF + hh:GDN_HALF + hh + 1, :]
                decay = tri * jnp.exp(jnp.minimum(g_col - g_row, 0.0))
                m = strict * (kk * decay * beta)
                heads.append(dict(hh=hh, rows=rows, seqs=seqs, q=q, k=k, v=v_ref[hh, rows, :], beta=beta,
                                  g_col=g_col, g_end=g_end,
                                  s=jnp.concatenate([s_ref[i, hh] for i in seqs], axis=0),
                                  mb=m.astype(BF16), qkd=(qk * decay).astype(BF16),
                                  t=eye_c - mask_ref[1] * m))
        return heads

    def solve(heads):
        for lv in range(2, nlv):
            for hd in heads:
                tb = hd["t"].astype(BF16)
                hd["t"] = hd["t"] - _dot(_dot(tb, maskb_ref[lv] * hd["mb"]).astype(BF16), tb)
        for hd in heads:
            e_g = jnp.exp(hd["g_col"])
            rhs = jnp.concatenate([hd["v"] * hd["beta"], hd["k"] * (hd["beta"] * e_g)], axis=1).astype(BF16)
            hd["uw"] = _dot(hd["t"].astype(BF16), rhs)
            hd["q_dec"] = hd["q"] * e_g
        for hd in heads:
            uw = hd["uw"]
            lhs = jnp.concatenate([uw[:, GDN_DV:], hd["q_dec"]], axis=0)
            if pack > 1:
                lhs = jnp.concatenate([lhs] * pack, axis=1) * jnp.concatenate([seqm_ref[...]] * 2, axis=0)
            ws = _dot(lhs.astype(BF16), hd["s"].astype(BF16))
            hd["vnb"] = (uw[:, :GDN_DV] - ws[:p]).astype(BF16)
            hd["o_inter"] = ws[p:]
        for hd in heads:
            g_end = hd["g_end"]
            k_dec = hd["k"] * jnp.exp(g_end - hd["g_col"])
            if pack > 1:
                k_dec = jnp.concatenate([k_dec] * pack, axis=1) * seqm_ref[...]
            hd["o"] = hd["o_inter"] + _dot(hd["qkd"], hd["vnb"])
            dec = jnp.concatenate([jnp.broadcast_to(jnp.exp(g_end[i:i + 1, :]), (GDN_DK, GDN_DV))
                                   for i in range(pack)], axis=0)
            hd["s_new"] = dec * hd["s"] + _dot_tn(k_dec.astype(BF16), hd["vnb"])
        for hd in heads:
            o_ref[hd["hh"], hd["rows"], :] = hd["o"]
            for i, j in enumerate(hd["seqs"]):
                s_ref[j, hd["hh"]] = hd["s_new"][i * GDN_DK:(i + 1) * GDN_DK]

    if pack == 1:
        def per_pair(jp, carry):
            b = 2 * jp
            solve(load_heads(pl.ds(b, p, stride=SUBLANES), [b])
                  + load_heads(pl.ds(b + 1, p, stride=SUBLANES), [b + 1]))
            return carry

        lax.fori_loop(0, SUBLANES // 2, per_pair, 0)
    else:
        solve(sum([load_heads(slice(i * p, (i + 1) * p), list(range(i * SUBLANES, (i + 1) * SUBLANES)))
                   for i in range(s_ref.shape[0] // SUBLANES)], []))


def gdn_chunks(qkv_tiles, p_tiles, a_log, dt_bias, s0, nb, seq, c, pack):
    ba0 = (GDN_CONV_DIM + GDN_VAL) // LANES
    rows = qkv_tiles.shape[1]
    nqk = GDN_HALF // (GDN_V_HEADS // GDN_QK_HEADS)
    nc = seq // c
    pr = _Problem(c, pack)
    pad = lambda x: jnp.pad(x.reshape(2, 1, GDN_HALF), ((0, 0), (0, 0), (GDN_HALF, LANES - 2 * GDN_HALF)))
    masks, seqm = pr.f32(pr.masks), pr.f32(pr.seq_lanes)
    tril = pr.f32(np.concatenate([pr.tri, pr.whole], axis=0))
    ns = SUBLANES * (PACK_GROUPS if pack > 1 else 1)
    assert nb % ns == 0
    tiles = lambda nt, blk: pl.BlockSpec((nt, c * ns, LANES), lambda b, hf, n: (blk(hf), b * nc + n, 0))
    const = lambda x: pl.BlockSpec(x.shape, lambda b, hf, n: (0,) * x.ndim)
    state = pl.BlockSpec((ns, GDN_HALF, GDN_DK, GDN_DV), lambda b, hf, n: (b, hf, 0, 0))
    o, s_out = pl.pallas_call(
        functools.partial(_gdn_chunk_kernel, p=pr.p, pack=pack), grid=(nb // ns, 2, nc),
        in_specs=[tiles(nqk, lambda hf: hf), tiles(nqk, lambda hf: 2 + hf),
                  tiles(GDN_HALF, lambda hf: 2 + hf), tiles(1, lambda hf: ba0 + hf),
                  pl.BlockSpec((1, 1, LANES), lambda b, hf, n: (hf, 0, 0)),
                  pl.BlockSpec((1, 1, LANES), lambda b, hf, n: (hf, 0, 0)),
                  const(tril), const(masks), const(masks), const(seqm), state],
        out_specs=[tiles(GDN_HALF, lambda hf: hf), state],
        out_shape=(SDS((GDN_V_HEADS, rows, LANES), F32), SDS((nb, GDN_V_HEADS, GDN_DK, GDN_DV), F32)),
        compiler_params=_cparams(3, vmem_mib=56), name="gdn_chunks",
    )(qkv_tiles, qkv_tiles, qkv_tiles, p_tiles, pad(a_log), pad(dt_bias), tril, masks, masks.astype(BF16), seqm, s0)
    return o, s_out


def gdn_layer(h, nb, seq, gain, s0, conv0, w_in, conv_w, a_log, dt_bias, norm_o, w_out):
    c, pack = _chunking(seq)
    n_main = GDN_CONV_DIM + GDN_VAL
    w_b = w_in[:, n_main:n_main + GDN_V_HEADS].reshape(D_MODEL, 2, GDN_HALF)
    w_a = w_in[:, n_main + GDN_V_HEADS:].reshape(D_MODEL, 2, GDN_HALF)
    w_ba = jnp.pad(jnp.concatenate([w_b, w_a], axis=2), ((0, 0), (0, 0), (0, LANES - 2 * GDN_HALF)))
    w_all = jnp.concatenate([w_in[:, :n_main], w_ba.reshape(D_MODEL, 2 * LANES)], axis=1).astype(BF16)
    p_tiles = norm_matmul(h, gain, w_all, GDN_PROJ_TN)
    ngrp = nb // SUBLANES
    conv0_tm = conv0.reshape(ngrp, SUBLANES, GDN_CONV - 1, GDN_CONV_DIM).transpose(0, 2, 1, 3)
    conv0_tm = conv0_tm.reshape(ngrp * (GDN_CONV - 1) * SUBLANES, GDN_CONV_DIM)
    qkv_tiles, new_conv = gdn_conv(p_tiles, conv_w, conv0_tm, nb)
    o_tiles, s_out = gdn_chunks(qkv_tiles, p_tiles, a_log, dt_bias, s0, nb, seq, c, pack)
    h = gated_out(o_tiles, p_tiles, 2, norm_o, w_out.astype(BF16), h, GDN_V_HEADS, GDN_DV, TM // 2)
    new_conv = new_conv.reshape(ngrp, GDN_CONV - 1, SUBLANES, GDN_CONV_DIM)
    return h, s_out, new_conv.transpose(0, 2, 1, 3).reshape(nb, GDN_CONV - 1, GDN_CONV_DIM)


def _trunk(x, s5_re, s5_im, gla_s, gdn_s, gdn_conv_s, w):
    nb, seq, d = x.shape
    h, hn = norm_in(x, w["norm_mix"][0])
    h, s5r0, s5i0 = s5_layer(h, hn, nb, s5_re[0], s5_im[0], w["s5_a_re"][0], w["s5_a_im"][0],
                             w["s5_log_dt"][0], w["s5_b_re"][0], w["s5_b_im"][0], w["s5_c_re"][0],
                             w["s5_c_im"][0], w["s5_d"][0], w["s5_w_glu"][0])
    h = ffn(h, w["norm_ffn"][0], w["w_up"][0], w["w_down"][0])
    h, gla_o = gla_layer(h, nb, seq, w["norm_mix"][1], gla_s[0], w["gla_w_in"][0], w["gla_w_gate_up"][0],
                         w["gla_b_gate"][0], w["gla_norm"][0], w["gla_w_out"][0])
    h = ffn(h, w["norm_ffn"][1], w["w_up"][1], w["w_down"][1])
    h, gdn_o, conv_o = gdn_layer(h, nb, seq, w["norm_mix"][2], gdn_s[0], gdn_conv_s[0], w["gdn_w_in"][0],
                                 w["gdn_conv_w"][0], w["gdn_a_log"][0], w["gdn_dt_bias"][0],
                                 w["gdn_norm"][0], w["gdn_w_out"][0])
    h, hn = ffn(h, w["norm_ffn"][2], w["w_up"][2], w["w_down"][2], next_gain=w["norm_mix"][3])
    h, s5r1, s5i1 = s5_layer(h, hn, nb, s5_re[1], s5_im[1], w["s5_a_re"][1], w["s5_a_im"][1],
                             w["s5_log_dt"][1], w["s5_b_re"][1], w["s5_b_im"][1], w["s5_c_re"][1],
                             w["s5_c_im"][1], w["s5_d"][1], w["s5_w_glu"][1])
    h = ffn(h, w["norm_ffn"][3], w["w_up"][3], w["w_down"][3])
    y = norm_out(h, w["norm_final"], nb, seq)
    return (y, jnp.stack([s5r0, s5r1]), jnp.stack([s5i0, s5i1]), gla_o[None], gdn_o[None], conv_o[None])


def kernel(x_prompt, x_sample, state_s5_re, state_s5_im, state_gla, state_gdn, state_gdn_conv, norm_mix, norm_ffn, norm_final, w_up, w_down, s5_a_re, s5_a_im, s5_log_dt, s5_b_re, s5_b_im, s5_c_re, s5_c_im, s5_d, s5_w_glu, gla_w_in, gla_w_gate_up, gla_b_gate, gla_norm, gla_w_out, gdn_w_in, gdn_conv_w, gdn_a_log, gdn_dt_bias, gdn_norm, gdn_w_out):
    w = dict(norm_mix=norm_mix, norm_ffn=norm_ffn, norm_final=norm_final,
             w_up=w_up.astype(BF16), w_down=w_down.astype(BF16),
             s5_a_re=s5_a_re, s5_a_im=s5_a_im, s5_log_dt=s5_log_dt, s5_b_re=s5_b_re, s5_b_im=s5_b_im,
             s5_c_re=s5_c_re, s5_c_im=s5_c_im, s5_d=s5_d, s5_w_glu=s5_w_glu,
             gla_w_in=gla_w_in, gla_w_gate_up=gla_w_gate_up, gla_b_gate=gla_b_gate, gla_norm=gla_norm,
             gla_w_out=gla_w_out, gdn_w_in=gdn_w_in, gdn_conv_w=gdn_conv_w, gdn_a_log=gdn_a_log,
             gdn_dt_bias=gdn_dt_bias, gdn_norm=gdn_norm, gdn_w_out=gdn_w_out)
    bp = x_prompt.shape[0]
    dt = x_prompt.dtype
    z_s5 = jnp.zeros((state_s5_re.shape[0], bp) + state_s5_re.shape[2:], dt)
    z_gla = jnp.zeros((state_gla.shape[0], bp) + state_gla.shape[2:], dt)
    z_gdn = jnp.zeros((state_gdn.shape[0], bp) + state_gdn.shape[2:], dt)
    z_conv = jnp.zeros((state_gdn_conv.shape[0], bp) + state_gdn_conv.shape[2:], dt)
    out_p = _trunk(x_prompt, z_s5, z_s5, z_gla, z_gdn, z_conv, w)
    out_s = _trunk(x_sample, state_s5_re, state_s5_im, state_gla, state_gdn, state_gdn_conv, w)
    return (out_p[0], out_s[0]) + out_p[1:] + out_s[1:]
```

```python
import functools
import math

import numpy as np
import jax
import jax.numpy as jnp
from jax import lax
from jax.experimental import pallas as pl
from jax.experimental.pallas import tpu as pltpu

F32 = jnp.float32
BF16 = jnp.bfloat16
SDS = jax.ShapeDtypeStruct

D_MODEL = 1024
NORM_EPS = 1e-6

S5_GROUP = 16
S5_STATE = 64
S5_GROUPS = D_MODEL // S5_GROUP
S5_GB = 16
S5_NGB = S5_GROUPS // S5_GB
S5_BC = S5_GB * S5_GROUP
S5_BS = S5_GB * S5_STATE

GLA_HEADS = 4
GLA_DK = 128
GLA_DV = 256
GLA_KEY = GLA_HEADS * GLA_DK
GLA_VAL = GLA_HEADS * GLA_DV
GLA_RANK = 16
GLA_TAU = 16.0
GLA_TN = 1024

GDN_DK = 128
GDN_DV = 128
GDN_QK_HEADS = 8
GDN_V_HEADS = 16
GDN_KEY = GDN_QK_HEADS * GDN_DK
GDN_VAL = GDN_V_HEADS * GDN_DV
GDN_HALF = GDN_V_HEADS // 2
GDN_CONV = 4
GDN_CONV_DIM = 2 * GDN_KEY + GDN_VAL
GDN_TN = 1024
GDN_PROJ_TN = 1280

PACK_GROUPS = 2
CHUNK = 128
LANES = 128
SUBLANES = 8
TM = 1024
FFN_TF = 1024
MIB = 1024 * 1024


def _cparams(n_axes, vmem_mib=48):
    return pltpu.CompilerParams(dimension_semantics=("arbitrary",) * n_axes,
                                vmem_limit_bytes=vmem_mib * MIB)


def _rms(x, gain):
    ms = jnp.mean(x * x, axis=-1, keepdims=True)
    return x * lax.rsqrt(ms + NORM_EPS) * gain


def _sigmoid(x):
    return 1.0 / (1.0 + jnp.exp(-x))


def _softplus(x):
    return jnp.maximum(x, 0.0) + jnp.log1p(jnp.exp(-jnp.abs(x)))


def _gelu_tanh(x):
    c = math.sqrt(2.0 / math.pi)
    return x * (0.5 * (1.0 + jnp.tanh(c * (x + 0.044715 * (x * x * x)))))


def _dot(a, b):
    return jnp.dot(a, b, preferred_element_type=F32)


def _dot_nt(a, b):
    return lax.dot_general(a, b, (((1,), (1,)), ((), ())), preferred_element_type=F32)


def _dot_tn(a, b):
    return lax.dot_general(a, b, (((0,), (0,)), ((), ())), preferred_element_type=F32)


def _eye(n):
    return (lax.broadcasted_iota(jnp.int32, (n, n), 0)
            == lax.broadcasted_iota(jnp.int32, (n, n), 1)).astype(F32)


def _split3(x):
    x1 = x.astype(BF16)
    r1 = x - x1.astype(F32)
    x2 = r1.astype(BF16)
    x3 = (r1 - x2.astype(F32)).astype(BF16)
    return x1, x2, x3


def _dot_exact01(m01, x):
    mb = m01.astype(BF16)
    x1, x2, x3 = _split3(x)
    return _dot(mb, x1) + _dot(mb, x2) + _dot(mb, x3)


def _dot_nt_exact01(m01, x):
    mb = m01.astype(BF16)
    x1, x2, x3 = _split3(x)
    return _dot_nt(mb, x1) + _dot_nt(mb, x2) + _dot_nt(mb, x3)


def _norm_in_kernel(x_ref, g_ref, h_ref, hn_ref):
    for j in range(SUBLANES):
        x = x_ref[j]
        h_ref[:, j, :] = x
        hn_ref[:, j, :] = _rms(x, g_ref[...])


def norm_in(x, gain):
    nb, seq, d = x.shape
    tt = min(seq, TM // SUBLANES)
    nblk = seq // tt
    out = SDS((nb // SUBLANES * seq, SUBLANES, d), F32)
    h, hn = pl.pallas_call(
        _norm_in_kernel, grid=(nb // SUBLANES, nblk),
        in_specs=[pl.BlockSpec((SUBLANES, tt, d), lambda g, i: (g, i, 0)),
                  pl.BlockSpec((1, d), lambda g, i: (0, 0))],
        out_specs=[pl.BlockSpec((tt, SUBLANES, d), lambda g, i: (g * nblk + i, 0, 0))] * 2,
        out_shape=(out, out), compiler_params=_cparams(2), name="norm_in",
    )(x, gain.reshape(1, d))
    return h.reshape(seq * nb, d), hn.reshape(seq * nb, d)


def _norm_out_kernel(h_ref, g_ref, y_ref):
    for j in range(SUBLANES):
        y_ref[j] = _rms(h_ref[:, j, :], g_ref[...])


def norm_out(h, gain, nb, seq):
    d = h.shape[1]
    tt = min(seq, TM // SUBLANES)
    nblk = seq // tt
    return pl.pallas_call(
        _norm_out_kernel, grid=(nb // SUBLANES, nblk),
        in_specs=[pl.BlockSpec((tt, SUBLANES, d), lambda g, i: (g * nblk + i, 0, 0)),
                  pl.BlockSpec((1, d), lambda g, i: (0, 0))],
        out_specs=pl.BlockSpec((SUBLANES, tt, d), lambda g, i: (g, i, 0)),
        out_shape=SDS((nb, seq, d), F32), compiler_params=_cparams(2), name="norm_out",
    )(h.reshape(nb // SUBLANES * seq, SUBLANES, d), gain.reshape(1, d))


def _norm_matmul_kernel(h_ref, g_ref, w_ref, o_ref, hn_ref):
    @pl.when(pl.program_id(1) == 0)
    def _():
        hn_ref[...] = _rms(h_ref[...], g_ref[...]).astype(BF16)

    res = _dot(hn_ref[...], w_ref[...])
    for t in range(o_ref.shape[0]):
        o_ref[t] = res[:, t * LANES:(t + 1) * LANES]


def norm_matmul(h, gain, w, tn):
    rows, d = h.shape
    n = w.shape[1]
    return pl.pallas_call(
        _norm_matmul_kernel, grid=(rows // TM, n // tn),
        in_specs=[pl.BlockSpec((TM, d), lambda i, j: (i, 0)),
                  pl.BlockSpec((1, d), lambda i, j: (0, 0)),
                  pl.BlockSpec((d, tn), lambda i, j: (0, j))],
        out_specs=pl.BlockSpec((tn // LANES, TM, LANES), lambda i, j: (j, i, 0)),
        out_shape=SDS((n // LANES, rows, LANES), F32),
        scratch_shapes=[pltpu.VMEM((TM, d), BF16)],
        compiler_params=_cparams(2), name="norm_matmul",
    )(h, gain.reshape(1, d), w)


def _ffn_kernel(h_ref, hnext_ref, g_ref, wu_ref, wd_ref, ng_ref, *rest, emit_norm):
    o_ref, hn_ref, acc_ref = rest[0], rest[-2], rest[-1]
    i = pl.program_id(0)
    j = pl.program_id(1)
    last = pl.num_programs(1) - 1
    slot = i % 2

    @pl.when((i == 0) & (j == 0))
    def _():
        hn_ref[0] = _rms(h_ref[...], g_ref[...]).astype(BF16)

    def partial():
        a = jnp.square(jnp.maximum(_dot(hn_ref[slot], wu_ref[...]), 0.0)).astype(BF16)
        return _dot(a, wd_ref[...])

    @pl.when(j == 0)
    def _():
        acc_ref[...] = partial()

    @pl.when((j > 0) & (j < last))
    def _():
        acc_ref[...] += partial()

    @pl.when(j == last)
    def _():
        hn_ref[1 - slot] = _rms(hnext_ref[...], g_ref[...]).astype(BF16)
        out = h_ref[...] + acc_ref[...] + partial()
        o_ref[...] = out
        if emit_norm:
            rest[1][...] = _rms(out, ng_ref[...])


def ffn(h, gain, w_up, w_down, next_gain=None):
    rows, d = h.shape
    f = w_up.shape[1]
    nblk = rows // TM
    assert f // FFN_TF >= 2
    emit_norm = next_gain is not None
    row_block = pl.BlockSpec((TM, d), lambda i, j: (i, 0))
    vec = pl.BlockSpec((1, d), lambda i, j: (0, 0))
    out = pl.pallas_call(
        functools.partial(_ffn_kernel, emit_norm=emit_norm), grid=(nblk, f // FFN_TF),
        in_specs=[row_block,
                  pl.BlockSpec((TM, d), lambda i, j: (jnp.minimum(i + 1, nblk - 1), 0)),
                  vec,
                  pl.BlockSpec((d, FFN_TF), lambda i, j: (0, j)),
                  pl.BlockSpec((FFN_TF, d), lambda i, j: (j, 0)),
                  vec],
        out_specs=[row_block] * (2 if emit_norm else 1),
        out_shape=[SDS((rows, d), F32)] * (2 if emit_norm else 1),
        scratch_shapes=[pltpu.VMEM((2, TM, d), BF16), pltpu.VMEM((TM, d), F32)],
        compiler_params=_cparams(2, vmem_mib=56 if emit_norm else 48), name="ffn",
    )(h, h, gain.reshape(1, d), w_up, w_down, (next_gain if emit_norm else gain).reshape(1, d))
    return (out[0], out[1]) if emit_norm else out[0]


def _gated_out_kernel(o_ref, z_ref, gn_ref, w_ref, h_ref, out_ref, *, nheads, hd):
    tph = hd // LANES
    kstep = 2 * LANES
    acc = h_ref[...]
    for k0 in range(0, nheads * hd, kstep):
        tiles = range(k0 // LANES, (k0 + kstep) // LANES)
        o = jnp.concatenate([o_ref[t] for t in tiles], axis=1)
        z = jnp.concatenate([z_ref[t] for t in tiles], axis=1)
        on = jnp.concatenate([_rms(o[:, i * hd:(i + 1) * hd], gn_ref[...]) for i in range(kstep // hd)], axis=1)
        a = (on * (z * _sigmoid(z))).astype(BF16)
        acc = acc + _dot(a, w_ref[k0:k0 + kstep, :])
    out_ref[...] = acc


def gated_out(o_tiles, p_tiles, z_block, gain, w_out, h, nheads, hd, tm):
    rows, d = h.shape
    kdim = nheads * hd
    nt = kdim // LANES
    return pl.pallas_call(
        functools.partial(_gated_out_kernel, nheads=nheads, hd=hd), grid=(rows // tm,),
        in_specs=[pl.BlockSpec((nt, tm, LANES), lambda i: (0, i, 0)),
                  pl.BlockSpec((nt, tm, LANES), lambda i: (z_block, i, 0)),
                  pl.BlockSpec((1, hd), lambda i: (0, 0)),
                  pl.BlockSpec((kdim, d), lambda i: (0, 0)),
                  pl.BlockSpec((tm, d), lambda i: (i, 0))],
        out_specs=pl.BlockSpec((tm, d), lambda i: (i, 0)),
        out_shape=SDS((rows, d), F32),
        compiler_params=_cparams(1), name="gated_out",
    )(o_tiles, p_tiles, gain.reshape(1, hd), w_out, h)


def _s5_discretize_kernel(are_ref, aim_ref, ldt_ref, bre_ref, bim_ref,
                          abr_ref, abi_ref, bbr_ref, bbi_ref):
    a_re, a_im = are_ref[...], aim_ref[...]
    dt = jnp.exp(ldt_ref[...])
    mag = jnp.exp(a_re * dt)
    ab_re = mag * jnp.cos(a_im * dt)
    ab_im = mag * jnp.sin(a_im * dt)
    den = a_re * a_re + a_im * a_im
    c_re = ((ab_re - 1.0) * a_re + ab_im * a_im) / den
    c_im = (ab_im * a_re - (ab_re - 1.0) * a_im) / den
    abr_ref[...] = ab_re
    abi_ref[...] = ab_im
    bbr_ref[...] = c_re * bre_ref[...] - c_im * bim_ref[...]
    bbi_ref[...] = c_re * bim_ref[...] + c_im * bre_ref[...]


def s5_discretize(a_re, a_im, log_dt, b_re, b_im):
    g, p, c = S5_GROUPS, S5_STATE, S5_GROUP
    expand = lambda v: jnp.broadcast_to(v[..., None], (g, p, c)).reshape(g, p * c)
    ldt = jnp.broadcast_to(log_dt[:, None], (g, p * c))
    shp = SDS((g, p * c), F32)
    ab_re, ab_im, bb_re, bb_im = pl.pallas_call(
        _s5_discretize_kernel, out_shape=(shp, shp, shp, shp), name="s5_discretize",
    )(expand(a_re), expand(a_im), ldt, b_re.reshape(g, p * c), b_im.reshape(g, p * c))
    ab_re = ab_re.reshape(g, p, c)[:, :, 0].reshape(S5_NGB, 1, S5_BS)
    ab_im = ab_im.reshape(g, p, c)[:, :, 0].reshape(S5_NGB, 1, S5_BS)
    eye = jnp.eye(S5_GB, dtype=F32)

    def block_diag_in(bb):
        bb = bb.reshape(S5_NGB, S5_GB, p, c)
        return jnp.einsum("bgpc,gh->bgchp", bb, eye).reshape(S5_NGB, S5_BC, S5_BS)

    b_blk = jnp.concatenate([block_diag_in(bb_re.reshape(g, p, c)),
                             block_diag_in(bb_im.reshape(g, p, c))], axis=-1).astype(BF16)
    return ab_re, ab_im, b_blk


def s5_block_diag_out(c_par):
    eye = jnp.eye(S5_GB, dtype=F32)
    cc = c_par.reshape(S5_NGB, S5_GB, S5_GROUP, S5_STATE)
    return jnp.einsum("bgcp,gh->bgphc", cc, eye).reshape(S5_NGB, S5_BS, S5_BC).astype(BF16)


def _s5_scan_kernel(u_ref, b_ref, cre_ref, cim_ref, are_ref, aim_ref, d_ref, s0r_ref, s0i_ref,
                    g_ref, slr_ref, sli_ref, xr_ref, xi_ref, str_ref, sti_ref, *, ngrp, tc):
    n = pl.program_id(1)

    @pl.when(n == 0)
    def _():
        str_ref[...] = s0r_ref[...]
        sti_ref[...] = s0i_ref[...]

    sub = SUBLANES
    a_re = jnp.broadcast_to(are_ref[0], (sub, S5_BS))
    a_im = jnp.broadcast_to(aim_ref[0], (sub, S5_BS))
    sb = 2 * LANES
    x_re = x_im = None
    for k in range(u_ref.shape[0] // sb):
        rows = slice(k * sb, (k + 1) * sb)
        u = u_ref[rows, :]
        ub = u.astype(BF16)
        xr_ref[rows, :] = _dot(ub, b_ref[0, :, :S5_BS])
        xi_ref[rows, :] = _dot(ub, b_ref[0, :, S5_BS:])
        for slab in range(k * sb // sub, (k + 1) * sb // sub):
            grp, t = divmod(slab, tc)
            srows = slice(grp * sub, (grp + 1) * sub)
            if t == 0:
                x_re, x_im = str_ref[srows, :], sti_ref[srows, :]
            r8 = slice(slab * sub, (slab + 1) * sub)
            x_re, x_im = (a_re * x_re - a_im * x_im + xr_ref[r8, :],
                          a_re * x_im + a_im * x_re + xi_ref[r8, :])
            xr_ref[r8, :] = x_re
            xi_ref[r8, :] = x_im
            if t == tc - 1:
                str_ref[srows, :] = x_re
                sti_ref[srows, :] = x_im
        y = _dot(xr_ref[rows, :].astype(BF16), cre_ref[0]) - _dot(xi_ref[rows, :].astype(BF16), cim_ref[0])
        g_ref[rows, :] = _gelu_tanh(y + d_ref[...] * u).astype(BF16)

    @pl.when(n == pl.num_programs(1) - 1)
    def _():
        slr_ref[...] = str_ref[...]
        sli_ref[...] = sti_ref[...]


def s5_scan(hn, nb, b_blk, c_re_blk, c_im_blk, ab_re, ab_im, d_skip, s0_re, s0_im):
    rows, d = hn.shape
    tc = min(rows // nb, TM // SUBLANES)
    ngrp = TM // (tc * SUBLANES)
    assert ngrp == 1 or ngrp * SUBLANES == nb
    st = SDS((nb, S5_GROUPS * S5_STATE), F32)
    return pl.pallas_call(
        functools.partial(_s5_scan_kernel, ngrp=ngrp, tc=tc), grid=(S5_NGB, rows // TM),
        in_specs=[pl.BlockSpec((TM, S5_BC), lambda gb, n: (n, gb)),
                  pl.BlockSpec((1, S5_BC, 2 * S5_BS), lambda gb, n: (gb, 0, 0)),
                  pl.BlockSpec((1, S5_BS, S5_BC), lambda gb, n: (gb, 0, 0)),
                  pl.BlockSpec((1, S5_BS, S5_BC), lambda gb, n: (gb, 0, 0)),
                  pl.BlockSpec((1, 1, S5_BS), lambda gb, n: (gb, 0, 0)),
                  pl.BlockSpec((1, 1, S5_BS), lambda gb, n: (gb, 0, 0)),
                  pl.BlockSpec((1, S5_BC), lambda gb, n: (0, gb)),
                  pl.BlockSpec((nb, S5_BS), lambda gb, n: (0, gb)),
                  pl.BlockSpec((nb, S5_BS), lambda gb, n: (0, gb))],
        out_specs=[pl.BlockSpec((TM, S5_BC), lambda gb, n: (n, gb)),
                   pl.BlockSpec((nb, S5_BS), lambda gb, n: (0, gb)),
                   pl.BlockSpec((nb, S5_BS), lambda gb, n: (0, gb))],
        out_shape=(SDS((rows, d), BF16), st, st),
        scratch_shapes=[pltpu.VMEM((TM, S5_BS), F32), pltpu.VMEM((TM, S5_BS), F32),
                        pltpu.VMEM((nb, S5_BS), F32), pltpu.VMEM((nb, S5_BS), F32)],
        compiler_params=_cparams(2), name="s5_scan",
    )(hn, b_blk, c_re_blk, c_im_blk, ab_re, ab_im, d_skip.reshape(1, d), s0_re, s0_im)


def _glu_out_kernel(g_ref, w_ref, h_ref, o_ref):
    gv = _dot(g_ref[...], w_ref[...])
    o_ref[...] = h_ref[...] + gv[:, :D_MODEL] * _sigmoid(gv[:, D_MODEL:])


def glu_out(g, w_glu, h):
    rows, d = h.shape
    tm = TM // 2
    return pl.pallas_call(
        _glu_out_kernel, grid=(rows // tm,),
        in_specs=[pl.BlockSpec((tm, d), lambda i: (i, 0)),
                  pl.BlockSpec((d, 2 * d), lambda i: (0, 0)),
                  pl.BlockSpec((tm, d), lambda i: (i, 0))],
        out_specs=pl.BlockSpec((tm, d), lambda i: (i, 0)),
        out_shape=SDS((rows, d), F32), compiler_params=_cparams(1), name="glu_out",
    )(g, w_glu, h)


def s5_layer(h, hn, nb, s0_re, s0_im, a_re, a_im, log_dt, b_re, b_im, c_re, c_im, d_skip, w_glu):
    ab_re, ab_im, b_blk = s5_discretize(a_re, a_im, log_dt, b_re, b_im)
    g, sl_re, sl_im = s5_scan(hn, nb, b_blk, s5_block_diag_out(c_re), s5_block_diag_out(c_im),
                              ab_re, ab_im, d_skip, s0_re.reshape(nb, -1), s0_im.reshape(nb, -1))
    shape = (nb, S5_GROUPS, S5_STATE)
    return glu_out(g, w_glu.astype(BF16), h), sl_re.reshape(shape), sl_im.reshape(shape)


class _Problem:
    def __init__(self, c, pack):
        p = c * pack
        r = np.arange(p)
        seq, time = r % pack, r // pack
        same = seq[:, None] == seq[None, :]
        self.p, self.pack = p, pack
        self.tri = same & (time[None, :] <= time[:, None])
        self.whole = same
        masks, pivots, bcast = [r[:, None] == r[None, :]], [], []
        for sz in [2 ** i for i in range(1, int(math.log2(p)) + 1)]:
            blk, off = r // sz, r % sz
            m = same & (blk[:, None] == blk[None, :]) & (off[:, None] >= sz // 2) & (off[None, :] < sz // 2)
            if not m.any():
                continue
            lower = same & (blk[:, None] == blk[None, :]) & (off[None, :] < sz // 2)
            piv = np.where(lower.any(1), (lower * r[None, :]).max(1), r)
            masks.append(m)
            if pack == 1 and sz >= SUBLANES:
                bcast.append(sz)
            else:
                pivots.append(self.tri[piv])
        self.masks = np.stack(masks)
        self.pivots = pivots
        self.bcast = bcast
        self.seq_lanes = np.repeat(seq[:, None] == np.arange(pack)[None, :], LANES, axis=1)

    def f32(self, x):
        return jnp.asarray(np.asarray(x, np.float32))


def _gla_gate_kernel(h_ref, gn_ref, wgl_ref, w_ref, b_ref, o_ref):
    hn = _rms(h_ref[...], gn_ref[...]).astype(BF16)
    gl = _dot(hn, wgl_ref[...])
    x = _dot(gl.astype(BF16), w_ref[...]) + b_ref[...]
    g = -_softplus(-x) * (1.0 / GLA_TAU)
    for t in range(o_ref.shape[0]):
        o_ref[t] = g[:, t * LANES:(t + 1) * LANES]


def gla_gate(h, gain, w_gl_pad, w_gate_pad, b_gate):
    rows, d = h.shape
    nt = GLA_KEY // LANES
    return pl.pallas_call(
        _gla_gate_kernel, grid=(rows // TM,),
        in_specs=[pl.BlockSpec((TM, d), lambda i: (i, 0)),
                  pl.BlockSpec((1, d), lambda i: (0, 0)),
                  pl.BlockSpec((d, LANES), lambda i: (0, 0)),
                  pl.BlockSpec((LANES, GLA_KEY), lambda i: (0, 0)),
                  pl.BlockSpec((1, GLA_KEY), lambda i: (0, 0))],
        out_specs=pl.BlockSpec((nt, TM, LANES), lambda i: (0, i, 0)),
        out_shape=SDS((nt, rows, LANES), F32), compiler_params=_cparams(1), name="gla_gate",
    )(h, gain.reshape(1, d), w_gl_pad, w_gate_pad, b_gate.reshape(1, GLA_KEY))


def _gla_chunk_kernel(q_ref, k_ref, v_ref, g_ref, wst_ref, mask_ref, seqm_ref, s0_ref, o_ref, s_ref, *, p, pack,
                      bcast):
    @pl.when(pl.program_id(1) == 0)
    def _():
        s_ref[...] = s0_ref[...]

    wst = wst_ref[...]
    eye = _eye(GLA_DK)
    nlev = mask_ref.shape[0] - 1
    nmat = nlev - len(bcast)
    tpv = GLA_DV // LANES

    def pivot(wk, lv):
        if lv <= nmat:
            return wk["bp"][lv * p:(lv + 1) * p]
        sz = bcast[lv - nmat - 1]
        b = wk["b"]
        return jnp.concatenate([jnp.broadcast_to(b[r0 + sz // 2 - 1:r0 + sz // 2, :], (sz, LANES))
                                for r0 in range(0, p, sz)], axis=0)

    def problems(rows_seqs):
        loaded = [(hh, rows, seqs, q_ref[hh, rows, :], k_ref[hh, rows, :],
                   [v_ref[hh * tpv + t, rows, :] for t in range(tpv)],
                   g_ref[hh, rows, :], [s_ref[j, hh] for j in seqs])
                  for rows, seqs in rows_seqs for hh in range(GLA_HEADS)]
        results = []
        work = []
        for hh, rows, seqs, q, k, v, g, ss in loaded:
            q = q * (GLA_DK ** -0.5)
            bp = _dot_exact01(wst, g)
            work.append(dict(hh=hh, rows=rows, seqs=seqs, q=q, k=k, v=v, ss=ss, bp=bp, b=bp[0:p],
                             att=mask_ref[0] * _dot_nt(q.astype(BF16), k.astype(BF16))))
        for lv in range(1, nlev + 1):
            for wk in work:
                e = jnp.exp(-jnp.abs(wk["b"] - pivot(wk, lv)))
                wk["att"] = wk["att"] + mask_ref[lv] * _dot_nt((wk["q"] * e).astype(BF16),
                                                               (wk["k"] * e).astype(BF16))
        for wk in work:
            q, k, v, ss, bp, b, att = (wk[n] for n in ("q", "k", "v", "ss", "bp", "b", "att"))
            vb = jnp.concatenate(v, axis=1).astype(BF16)
            b_end = bp[(nmat + 1) * p:(nmat + 2) * p] if pack > 1 else jnp.broadcast_to(b[p - 1:p, :], (p, LANES))
            s_all = jnp.concatenate(ss, axis=0)
            q_dec = q * jnp.exp(b)
            k_dec = k * jnp.exp(b_end - b)
            if pack > 1:
                q_dec = jnp.concatenate([q_dec] * pack, axis=1) * seqm_ref[...]
                k_dec = jnp.concatenate([k_dec] * pack, axis=1) * seqm_ref[...]
            o = _dot(q_dec.astype(BF16), s_all.astype(BF16)) + _dot(att.astype(BF16), vb)
            last = jnp.concatenate([b_end[0:pack]] * (GLA_DK // pack), axis=0) if pack > 1 else (
                jnp.broadcast_to(b_end[0:1], (GLA_DK, GLA_DK)))
            col = jnp.exp(_dot_nt_exact01(eye, last))
            dec = jnp.concatenate([jnp.broadcast_to(col[:, i:i + 1], (GLA_DK, GLA_DV)) for i in range(pack)], axis=0)
            results.append((wk, o, dec * s_all + _dot_tn(k_dec.astype(BF16), vb)))
        for wk, o, s_new in results:
            hh = wk["hh"]
            for t in range(tpv):
                o_ref[hh * tpv + t, wk["rows"], :] = o[:, t * LANES:(t + 1) * LANES]
            for i, j in enumerate(wk["seqs"]):
                s_ref[j, hh] = s_new[i * GLA_DK:(i + 1) * GLA_DK]

    if pack == 1:
        def per_pair(jp, carry):
            j = 2 * jp
            problems([(pl.ds(j, p, stride=SUBLANES), [j]), (pl.ds(j + 1, p, stride=SUBLANES), [j + 1])])
            return carry

        lax.fori_loop(0, SUBLANES // 2, per_pair, 0)
    else:
        problems([(slice(i * p, (i + 1) * p), list(range(i * SUBLANES, (i + 1) * SUBLANES)))
                  for i in range(s_ref.shape[0] // SUBLANES)])


def gla_chunks(p_tiles, g_tiles, s0, nb, seq, c, pack):
    rows = p_tiles.shape[1]
    pr = _Problem(c, pack)
    wst = pr.f32(np.concatenate([pr.tri] + pr.pivots + ([pr.whole] if pack > 1 else []), axis=0))
    nkt = GLA_KEY // LANES
    nvt = GLA_VAL // LANES
    nc = seq // c
    ns = SUBLANES * (PACK_GROUPS if pack > 1 else 1)
    assert nb % ns == 0
    tiles = lambda nt, blk: pl.BlockSpec((nt, c * ns, LANES), lambda b, n: (blk, b * nc + n, 0))
    const = lambda x: pl.BlockSpec(x.shape, lambda b, n: (0,) * x.ndim)
    state = pl.BlockSpec((ns, GLA_HEADS, GLA_DK, GLA_DV), lambda b, n: (b, 0, 0, 0))
    masks, seqm = pr.f32(pr.masks), pr.f32(pr.seq_lanes)
    o, s_out = pl.pallas_call(
        functools.partial(_gla_chunk_kernel, p=pr.p, pack=pack, bcast=tuple(pr.bcast)), grid=(nb // ns, nc),
        in_specs=[tiles(nkt, 0), tiles(nkt, 1), tiles(nvt, 1), tiles(nkt, 0),
                  const(wst), const(masks), const(seqm), state],
        out_specs=[tiles(nvt, 0), state],
        out_shape=(SDS((nvt, rows, LANES), F32), SDS((nb, GLA_HEADS, GLA_DK, GLA_DV), F32)),
        compiler_params=_cparams(2), name="gla_chunks",
    )(p_tiles, p_tiles, p_tiles, g_tiles, wst, masks, seqm, s0)
    return o, s_out


def _chunking(seq):
    return (CHUNK, 1) if seq % CHUNK == 0 else (seq, SUBLANES)


def gla_layer(h, nb, seq, gain, s0, w_in, w_gate_up, b_gate, norm_o, w_out):
    c, pack = _chunking(seq)
    n_main = 2 * GLA_KEY + 2 * GLA_VAL
    p_tiles = norm_matmul(h, gain, w_in[:, :n_main].astype(BF16), GLA_TN)
    w_gl = jnp.pad(w_in[:, n_main:], ((0, 0), (0, LANES - GLA_RANK))).astype(BF16)
    w_gate_pad = jnp.pad(w_gate_up, ((0, LANES - GLA_RANK), (0, 0))).astype(BF16)
    g_tiles = gla_gate(h, gain, w_gl, w_gate_pad, b_gate)
    o_tiles, s_out = gla_chunks(p_tiles, g_tiles, s0, nb, seq, c, pack)
    h = gated_out(o_tiles, p_tiles, 2, norm_o, w_out.astype(BF16), h, GLA_HEADS, GLA_DV, TM)
    return h, s_out


def _gdn_conv_kernel(x_ref, cw_ref, c0_ref, o_ref, nc_ref, xp_ref, *, tm):
    grp = pl.program_id(0)
    i = pl.program_id(2)
    nb = SUBLANES
    halo = (GDN_CONV - 1) * nb
    scale = jnp.where(grp == 0, GDN_DK ** -0.5, 1.0)
    for t in range(x_ref.shape[0]):
        lanes = slice(t * LANES, (t + 1) * LANES)

        @pl.when(i == 0)
        def _():
            xp_ref[t, 0:halo, :] = c0_ref[:, lanes]

        xp_ref[t, halo:halo + tm, :] = x_ref[t]
        acc = cw_ref[0:1, lanes] * xp_ref[t, 0:tm, :]
        for j in range(1, GDN_CONV):
            acc = acc + cw_ref[j:j + 1, lanes] * xp_ref[t, j * nb:j * nb + tm, :]
        y = acc * _sigmoid(acc)

        inv = lax.rsqrt(jnp.sum(y * y, axis=-1, keepdims=True) + NORM_EPS) * scale
        o_ref[t] = y * jnp.where(grp < 2, inv, 1.0)
        tail = xp_ref[t, tm:tm + halo, :]
        xp_ref[t, 0:halo, :] = tail

        @pl.when(i == pl.num_programs(2) - 1)
        def _():
            nc_ref[:, lanes] = tail


def gdn_conv(p_tiles, conv_w, conv0, nb):
    rows = p_tiles.shape[1]
    ngrp = nb // SUBLANES
    halo = (GDN_CONV - 1) * SUBLANES
    tm = min(TM, rows // ngrp)
    nblk = rows // ngrp // tm
    gt = GDN_TN // LANES
    ng = GDN_CONV_DIM // GDN_TN
    assert tm >= halo
    return pl.pallas_call(
        functools.partial(_gdn_conv_kernel, tm=tm), grid=(ng, ngrp, nblk),
        in_specs=[pl.BlockSpec((gt, tm, LANES), lambda j, g, i: (j, g * nblk + i, 0)),
                  pl.BlockSpec((GDN_CONV, GDN_TN), lambda j, g, i: (0, j)),
                  pl.BlockSpec((halo, GDN_TN), lambda j, g, i: (g, j))],
        out_specs=[pl.BlockSpec((gt, tm, LANES), lambda j, g, i: (j, g * nblk + i, 0)),
                   pl.BlockSpec((halo, GDN_TN), lambda j, g, i: (g, j))],
        out_shape=(SDS((ng * gt, rows, LANES), F32), SDS((ngrp * halo, GDN_CONV_DIM), F32)),
        scratch_shapes=[pltpu.VMEM((gt, halo + tm, LANES), F32)],
        compiler_params=_cparams(3), name="gdn_conv",
    )(p_tiles, conv_w, conv0)


def _gdn_chunk_kernel(q_ref, k_ref, v_ref, ba_ref, alog_ref, dtb_ref, tril_ref, mask_ref, maskb_ref, seqm_ref,
                      s0_ref, o_ref, s_ref, *, p, pack):
    @pl.when(pl.program_id(2) == 0)
    def _():
        s_ref[...] = s0_ref[...]

    tri = tril_ref[0:p, :]
    eye_l = _eye(LANES)
    eye_c = mask_ref[0]
    strict = tri - eye_c
    nlv = mask_ref.shape[0]
    rep = GDN_V_HEADS // GDN_QK_HEADS

    def load_heads(rows, seqs):
        ba = ba_ref[0, rows, :]
        beta_all = _sigmoid(ba)
        g_all = -jnp.exp(alog_ref[0]) * _softplus(ba + dtb_ref[0])
        gcl = _dot_exact01(tril_ref[...], g_all)
        gc_all, ge_all = gcl[0:p], gcl[p:2 * p]
        gr_all = _dot_nt_exact01(eye_l, gc_all)
        heads = []
        for qh in range(GDN_HALF // rep):
            q = q_ref[qh, rows, :]
            k = k_ref[qh, rows, :]
            kb = k.astype(BF16)
            kq = _dot_nt(jnp.concatenate([kb, q.astype(BF16)], axis=0), kb)
            kk, qk = kq[:p], kq[p:]
            for j in range(rep):
                hh = qh * rep + j
                beta = beta_all[:, hh:hh + 1]
                g_col = gc_all[:, GDN_HALF + hh:GDN_HALF + hh + 1]
                g_end = ge_all[:, GDN_HALF + hh:GDN_HALF + hh + 1]
                g_row = gr_all[GDN_HALF + hh:GDN_HALF + hh + 1, :]
                decay = tri * jnp.exp(jnp.minimum(g_col - g_row, 0.0))
                m = strict * (kk * decay * beta)
                heads.append(dict(hh=hh, rows=rows, seqs=seqs, q=q, k=k, v=v_ref[hh, rows, :], beta=beta,
                                  g_col=g_col, g_end=g_end,
                                  s=jnp.concatenate([s_ref[i, hh] for i in seqs], axis=0),
                                  mb=m.astype(BF16), qkd=(qk * decay).astype(BF16),
                                  t=eye_c - mask_ref[1] * m))
        return heads

    def solve(heads):
        for lv in range(2, nlv):
            for hd in heads:
                tb = hd["t"].astype(BF16)
                hd["t"] = hd["t"] - _dot(_dot(tb, maskb_ref[lv] * hd["mb"]).astype(BF16), tb)
        for hd in heads:
            e_g = jnp.exp(hd["g_col"])
            rhs = jnp.concatenate([hd["v"] * hd["beta"], hd["k"] * (hd["beta"] * e_g)], axis=1).astype(BF16)
            hd["uw"] = _dot(hd["t"].astype(BF16), rhs)
            hd["q_dec"] = hd["q"] * e_g
        for hd in heads:
            uw = hd["uw"]
            lhs = jnp.concatenate([uw[:, GDN_DV:], hd["q_dec"]], axis=0)
            if pack > 1:
                lhs = jnp.concatenate([lhs] * pack, axis=1) * jnp.concatenate([seqm_ref[...]] * 2, axis=0)
            ws = _dot(lhs.astype(BF16), hd["s"].astype(BF16))
            hd["vnb"] = (uw[:, :GDN_DV] - ws[:p]).astype(BF16)
            hd["o_inter"] = ws[p:]
        for hd in heads:
            g_end = hd["g_end"]
            k_dec = hd["k"] * jnp.exp(g_end - hd["g_col"])
            if pack > 1:
                k_dec = jnp.concatenate([k_dec] * pack, axis=1) * seqm_ref[...]
            hd["o"] = hd["o_inter"] + _dot(hd["qkd"], hd["vnb"])
            dec = jnp.concatenate([jnp.broadcast_to(jnp.exp(g_end[i:i + 1, :]), (GDN_DK, GDN_DV))
                                   for i in range(pack)], axis=0)
            hd["s_new"] = dec * hd["s"] + _dot_tn(k_dec.astype(BF16), hd["vnb"])
        for hd in heads:
            o_ref[hd["hh"], hd["rows"], :] = hd["o"]
            for i, j in enumerate(hd["seqs"]):
                s_ref[j, hd["hh"]] = hd["s_new"][i * GDN_DK:(i + 1) * GDN_DK]

    if pack == 1:
        def per_pair(jp, carry):
            b = 2 * jp
            solve(load_heads(pl.ds(b, p, stride=SUBLANES), [b])
                  + load_heads(pl.ds(b + 1, p, stride=SUBLANES), [b + 1]))
            return carry

        lax.fori_loop(0, SUBLANES // 2, per_pair, 0)
    else:
        solve(sum([load_heads(slice(i * p, (i + 1) * p), list(range(i * SUBLANES, (i + 1) * SUBLANES)))
                   for i in range(s_ref.shape[0] // SUBLANES)], []))


def gdn_chunks(qkv_tiles, p_tiles, a_log, dt_bias, s0, nb, seq, c, pack):
    ba0 = (GDN_CONV_DIM + GDN_VAL) // LANES
    rows = qkv_tiles.shape[1]
    nqk = GDN_HALF // (GDN_V_HEADS // GDN_QK_HEADS)
    nc = seq // c
    pr = _Problem(c, pack)
    pad = lambda x: jnp.pad(x.reshape(2, 1, GDN_HALF), ((0, 0), (0, 0), (GDN_HALF, LANES - 2 * GDN_HALF)))
    masks, seqm = pr.f32(pr.masks), pr.f32(pr.seq_lanes)
    tril = pr.f32(np.concatenate([pr.tri, pr.whole], axis=0))
    ns = SUBLANES * (PACK_GROUPS if pack > 1 else 1)
    assert nb % ns == 0
    tiles = lambda nt, blk: pl.BlockSpec((nt, c * ns, LANES), lambda b, hf, n: (blk(hf), b * nc + n, 0))
    const = lambda x: pl.BlockSpec(x.shape, lambda b, hf, n: (0,) * x.ndim)
    state = pl.BlockSpec((ns, GDN_HALF, GDN_DK, GDN_DV), lambda b, hf, n: (b, hf, 0, 0))
    o, s_out = pl.pallas_call(
        functools.partial(_gdn_chunk_kernel, p=pr.p, pack=pack), grid=(nb // ns, 2, nc),
        in_specs=[tiles(nqk, lambda hf: hf), tiles(nqk, lambda hf: 2 + hf),
                  tiles(GDN_HALF, lambda hf: 2 + hf), tiles(1, lambda hf: ba0 + hf),
                  pl.BlockSpec((1, 1, LANES), lambda b, hf, n: (hf, 0, 0)),
                  pl.BlockSpec((1, 1, LANES), lambda b, hf, n: (hf, 0, 0)),
                  const(tril), const(masks), const(masks), const(seqm), state],
        out_specs=[tiles(GDN_HALF, lambda hf: hf), state],
        out_shape=(SDS((GDN_V_HEADS, rows, LANES), F32), SDS((nb, GDN_V_HEADS, GDN_DK, GDN_DV), F32)),
        compiler_params=_cparams(3, vmem_mib=56), name="gdn_chunks",
    )(qkv_tiles, qkv_tiles, qkv_tiles, p_tiles, pad(a_log), pad(dt_bias), tril, masks, masks.astype(BF16), seqm, s0)
    return o, s_out


def gdn_layer(h, nb, seq, gain, s0, conv0, w_in, conv_w, a_log, dt_bias, norm_o, w_out):
    c, pack = _chunking(seq)
    n_main = GDN_CONV_DIM + GDN_VAL
    w_b = w_in[:, n_main:n_main + GDN_V_HEADS].reshape(D_MODEL, 2, GDN_HALF)
    w_a = w_in[:, n_main + GDN_V_HEADS:].reshape(D_MODEL, 2, GDN_HALF)
    w_ba = jnp.pad(jnp.concatenate([w_b, w_a], axis=2), ((0, 0), (0, 0), (0, LANES - 2 * GDN_HALF)))
    w_all = jnp.concatenate([w_in[:, :n_main], w_ba.reshape(D_MODEL, 2 * LANES)], axis=1).astype(BF16)
    p_tiles = norm_matmul(h, gain, w_all, GDN_PROJ_TN)
    ngrp = nb // SUBLANES
    conv0_tm = conv0.reshape(ngrp, SUBLANES, GDN_CONV - 1, GDN_CONV_DIM).transpose(0, 2, 1, 3)
    conv0_tm = conv0_tm.reshape(ngrp * (GDN_CONV - 1) * SUBLANES, GDN_CONV_DIM)
    qkv_tiles, new_conv = gdn_conv(p_tiles, conv_w, conv0_tm, nb)
    o_tiles, s_out = gdn_chunks(qkv_tiles, p_tiles, a_log, dt_bias, s0, nb, seq, c, pack)
    h = gated_out(o_tiles, p_tiles, 2, norm_o, w_out.astype(BF16), h, GDN_V_HEADS, GDN_DV, TM // 2)
    new_conv = new_conv.reshape(ngrp, GDN_CONV - 1, SUBLANES, GDN_CONV_DIM)
    return h, s_out, new_conv.transpose(0, 2, 1, 3).reshape(nb, GDN_CONV - 1, GDN_CONV_DIM)


def _trunk(x, s5_re, s5_im, gla_s, gdn_s, gdn_conv_s, w):
    nb, seq, d = x.shape
    h, hn = norm_in(x, w["norm_mix"][0])
    h, s5r0, s5i0 = s5_layer(h, hn, nb, s5_re[0], s5_im[0], w["s5_a_re"][0], w["s5_a_im"][0],
                             w["s5_log_dt"][0], w["s5_b_re"][0], w["s5_b_im"][0], w["s5_c_re"][0],
                             w["s5_c_im"][0], w["s5_d"][0], w["s5_w_glu"][0])
    h = ffn(h, w["norm_ffn"][0], w["w_up"][0], w["w_down"][0])
    h, gla_o = gla_layer(h, nb, seq, w["norm_mix"][1], gla_s[0], w["gla_w_in"][0], w["gla_w_gate_up"][0],
                         w["gla_b_gate"][0], w["gla_norm"][0], w["gla_w_out"][0])
    h = ffn(h, w["norm_ffn"][1], w["w_up"][1], w["w_down"][1])
    h, gdn_o, conv_o = gdn_layer(h, nb, seq, w["norm_mix"][2], gdn_s[0], gdn_conv_s[0], w["gdn_w_in"][0],
                                 w["gdn_conv_w"][0], w["gdn_a_log"][0], w["gdn_dt_bias"][0],
                                 w["gdn_norm"][0], w["gdn_w_out"][0])
    h, hn = ffn(h, w["norm_ffn"][2], w["w_up"][2], w["w_down"][2], next_gain=w["norm_mix"][3])
    h, s5r1, s5i1 = s5_layer(h, hn, nb, s5_re[1], s5_im[1], w["s5_a_re"][1], w["s5_a_im"][1],
                             w["s5_log_dt"][1], w["s5_b_re"][1], w["s5_b_im"][1], w["s5_c_re"][1],
                             w["s5_c_im"][1], w["s5_d"][1], w["s5_w_glu"][1])
    h = ffn(h, w["norm_ffn"][3], w["w_up"][3], w["w_down"][3])
    y = norm_out(h, w["norm_final"], nb, seq)
    return (y, jnp.stack([s5r0, s5r1]), jnp.stack([s5i0, s5i1]), gla_o[None], gdn_o[None], conv_o[None])


def kernel(x_prompt, x_sample, state_s5_re, state_s5_im, state_gla, state_gdn, state_gdn_conv, norm_mix, norm_ffn, norm_final, w_up, w_down, s5_a_re, s5_a_im, s5_log_dt, s5_b_re, s5_b_im, s5_c_re, s5_c_im, s5_d, s5_w_glu, gla_w_in, gla_w_gate_up, gla_b_gate, gla_norm, gla_w_out, gdn_w_in, gdn_conv_w, gdn_a_log, gdn_dt_bias, gdn_norm, gdn_w_out):
    w = dict(norm_mix=norm_mix, norm_ffn=norm_ffn, norm_final=norm_final,
             w_up=w_up.astype(BF16), w_down=w_down.astype(BF16),
             s5_a_re=s5_a_re, s5_a_im=s5_a_im, s5_log_dt=s5_log_dt, s5_b_re=s5_b_re, s5_b_im=s5_b_im,
             s5_c_re=s5_c_re, s5_c_im=s5_c_im, s5_d=s5_d, s5_w_glu=s5_w_glu,
             gla_w_in=gla_w_in, gla_w_gate_up=gla_w_gate_up, gla_b_gate=gla_b_gate, gla_norm=gla_norm,
             gla_w_out=gla_w_out, gdn_w_in=gdn_w_in, gdn_conv_w=gdn_conv_w, gdn_a_log=gdn_a_log,
             gdn_dt_bias=gdn_dt_bias, gdn_norm=gdn_norm, gdn_w_out=gdn_w_out)
    bp = x_prompt.shape[0]
    dt = x_prompt.dtype
    z_s5 = jnp.zeros((state_s5_re.shape[0], bp) + state_s5_re.shape[2:], dt)
    z_gla = jnp.zeros((state_gla.shape[0], bp) + state_gla.shape[2:], dt)
    z_gdn = jnp.zeros((state_gdn.shape[0], bp) + state_gdn.shape[2:], dt)
    z_conv = jnp.zeros((state_gdn_conv.shape[0], bp) + state_gdn_conv.shape[2:], dt)
    out_p = _trunk(x_prompt, z_s5, z_s5, z_gla, z_gdn, z_conv, w)
    out_s = _trunk(x_sample, state_s5_re, state_s5_im, state_gla, state_gdn, state_gdn_conv, w)
    return (out_p[0], out_s[0]) + out_p[1:] + out_s[1:]
```

```python
import functools
import math

import numpy as np
import jax
import jax.numpy as jnp
from jax import lax
from jax.experimental import pallas as pl
from jax.experimental.pallas import tpu as pltpu

F32 = jnp.float32
BF16 = jnp.bfloat16
SDS = jax.ShapeDtypeStruct

D_MODEL = 1024
NORM_EPS = 1e-6

S5_GROUP = 16
S5_STATE = 64
S5_GROUPS = D_MODEL // S5_GROUP
S5_GB = 16
S5_NGB = S5_GROUPS // S5_GB
S5_BC = S5_GB * S5_GROUP
S5_BS = S5_GB * S5_STATE

GLA_HEADS = 4
GLA_DK = 128
GLA_DV = 256
GLA_KEY = GLA_HEADS * GLA_DK
GLA_VAL = GLA_HEADS * GLA_DV
GLA_RANK = 16
GLA_TAU = 16.0
GLA_TN = 1024

GDN_DK = 128
GDN_DV = 128
GDN_QK_HEADS = 8
GDN_V_HEADS = 16
GDN_KEY = GDN_QK_HEADS * GDN_DK
GDN_VAL = GDN_V_HEADS * GDN_DV
GDN_HALF = GDN_V_HEADS // 2
GDN_CONV = 4
GDN_CONV_DIM = 2 * GDN_KEY + GDN_VAL
GDN_TN = 1024
GDN_PROJ_TN = 1280

PACK_GROUPS = 2
CHUNK = 128
LANES = 128
SUBLANES = 8
TM = 1024
FFN_TF = 1024
MIB = 1024 * 1024


def _cparams(n_axes, vmem_mib=48):
    return pltpu.CompilerParams(dimension_semantics=("arbitrary",) * n_axes,
                                vmem_limit_bytes=vmem_mib * MIB)


def _rms(x, gain):
    ms = jnp.mean(x * x, axis=-1, keepdims=True)
    return x * lax.rsqrt(ms + NORM_EPS) * gain


def _sigmoid(x):
    return 1.0 / (1.0 + jnp.exp(-x))


def _softplus(x):
    return jnp.maximum(x, 0.0) + jnp.log1p(jnp.exp(-jnp.abs(x)))


def _gelu_tanh(x):
    c = math.sqrt(2.0 / math.pi)
    return x * (0.5 * (1.0 + jnp.tanh(c * (x + 0.044715 * (x * x * x)))))


def _dot(a, b):
    return jnp.dot(a, b, preferred_element_type=F32)


def _dot_nt(a, b):
    return lax.dot_general(a, b, (((1,), (1,)), ((), ())), preferred_element_type=F32)


def _dot_tn(a, b):
    return lax.dot_general(a, b, (((0,), (0,)), ((), ())), preferred_element_type=F32)


def _eye(n):
    return (lax.broadcasted_iota(jnp.int32, (n, n), 0)
            == lax.broadcasted_iota(jnp.int32, (n, n), 1)).astype(F32)


def _split3(x):
    x1 = x.astype(BF16)
    r1 = x - x1.astype(F32)
    x2 = r1.astype(BF16)
    x3 = (r1 - x2.astype(F32)).astype(BF16)
    return x1, x2, x3


def _dot_exact01(m01, x):
    mb = m01.astype(BF16)
    x1, x2, x3 = _split3(x)
    return _dot(mb, x1) + _dot(mb, x2) + _dot(mb, x3)


def _dot_nt_exact01(m01, x):
    mb = m01.astype(BF16)
    x1, x2, x3 = _split3(x)
    return _dot_nt(mb, x1) + _dot_nt(mb, x2) + _dot_nt(mb, x3)


def _norm_in_kernel(x_ref, g_ref, h_ref, hn_ref):
    for j in range(SUBLANES):
        x = x_ref[j]
        h_ref[:, j, :] = x
        hn_ref[:, j, :] = _rms(x, g_ref[...])


def norm_in(x, gain):
    nb, seq, d = x.shape
    tt = min(seq, TM // SUBLANES)
    nblk = seq // tt
    out = SDS((nb // SUBLANES * seq, SUBLANES, d), F32)
    h, hn = pl.pallas_call(
        _norm_in_kernel, grid=(nb // SUBLANES, nblk),
        in_specs=[pl.BlockSpec((SUBLANES, tt, d), lambda g, i: (g, i, 0)),
                  pl.BlockSpec((1, d), lambda g, i: (0, 0))],
        out_specs=[pl.BlockSpec((tt, SUBLANES, d), lambda g, i: (g * nblk + i, 0, 0))] * 2,
        out_shape=(out, out), compiler_params=_cparams(2), name="norm_in",
    )(x, gain.reshape(1, d))
    return h.reshape(seq * nb, d), hn.reshape(seq * nb, d)


def _norm_out_kernel(h_ref, g_ref, y_ref):
    for j in range(SUBLANES):
        y_ref[j] = _rms(h_ref[:, j, :], g_ref[...])


def norm_out(h, gain, nb, seq):
    d = h.shape[1]
    tt = min(seq, TM // SUBLANES)
    nblk = seq // tt
    return pl.pallas_call(
        _norm_out_kernel, grid=(nb // SUBLANES, nblk),
        in_specs=[pl.BlockSpec((tt, SUBLANES, d), lambda g, i: (g * nblk + i, 0, 0)),
                  pl.BlockSpec((1, d), lambda g, i: (0, 0))],
        out_specs=pl.BlockSpec((SUBLANES, tt, d), lambda g, i: (g, i, 0)),
        out_shape=SDS((nb, seq, d), F32), compiler_params=_cparams(2), name="norm_out",
    )(h.reshape(nb // SUBLANES * seq, SUBLANES, d), gain.reshape(1, d))


def _norm_matmul_kernel(h_ref, g_ref, w_ref, o_ref, hn_ref):
    @pl.when(pl.program_id(1) == 0)
    def _():
        hn_ref[...] = _rms(h_ref[...], g_ref[...]).astype(BF16)

    res = _dot(hn_ref[...], w_ref[...])
    for t in range(o_ref.shape[0]):
        o_ref[t] = res[:, t * LANES:(t + 1) * LANES]


def norm_matmul(h, gain, w, tn):
    rows, d = h.shape
    n = w.shape[1]
    return pl.pallas_call(
        _norm_matmul_kernel, grid=(rows // TM, n // tn),
        in_specs=[pl.BlockSpec((TM, d), lambda i, j: (i, 0)),
                  pl.BlockSpec((1, d), lambda i, j: (0, 0)),
                  pl.BlockSpec((d, tn), lambda i, j: (0, j))],
        out_specs=pl.BlockSpec((tn // LANES, TM, LANES), lambda i, j: (j, i, 0)),
        out_shape=SDS((n // LANES, rows, LANES), F32),
        scratch_shapes=[pltpu.VMEM((TM, d), BF16)],
        compiler_params=_cparams(2), name="norm_matmul",
    )(h, gain.reshape(1, d), w)


def _ffn_kernel(h_ref, hnext_ref, g_ref, wu_ref, wd_ref, ng_ref, *rest, emit_norm):
    o_ref, hn_ref, acc_ref = rest[0], rest[-2], rest[-1]
    i = pl.program_id(0)
    j = pl.program_id(1)
    last = pl.num_programs(1) - 1
    slot = i % 2

    @pl.when((i == 0) & (j == 0))
    def _():
        hn_ref[0] = _rms(h_ref[...], g_ref[...]).astype(BF16)

    def partial():
        a = jnp.square(jnp.maximum(_dot(hn_ref[slot], wu_ref[...]), 0.0)).astype(BF16)
        return _dot(a, wd_ref[...])

    @pl.when(j == 0)
    def _():
        acc_ref[...] = partial()

    @pl.when((j > 0) & (j < last))
    def _():
        acc_ref[...] += partial()

    @pl.when(j == last)
    def _():
        hn_ref[1 - slot] = _rms(hnext_ref[...], g_ref[...]).astype(BF16)
        out = h_ref[...] + acc_ref[...] + partial()
        o_ref[...] = out
        if emit_norm:
            rest[1][...] = _rms(out, ng_ref[...])


def ffn(h, gain, w_up, w_down, next_gain=None):
    rows, d = h.shape
    f = w_up.shape[1]
    nblk = rows // TM
    assert f // FFN_TF >= 2
    emit_norm = next_gain is not None
    row_block = pl.BlockSpec((TM, d), lambda i, j: (i, 0))
    vec = pl.BlockSpec((1, d), lambda i, j: (0, 0))
    out = pl.pallas_call(
        functools.partial(_ffn_kernel, emit_norm=emit_norm), grid=(nblk, f // FFN_TF),
        in_specs=[row_block,
                  pl.BlockSpec((TM, d), lambda i, j: (jnp.minimum(i + 1, nblk - 1), 0)),
                  vec,
                  pl.BlockSpec((d, FFN_TF), lambda i, j: (0, j)),
                  pl.BlockSpec((FFN_TF, d), lambda i, j: (j, 0)),
                  vec],
        out_specs=[row_block] * (2 if emit_norm else 1),
        out_shape=[SDS((rows, d), F32)] * (2 if emit_norm else 1),
        scratch_shapes=[pltpu.VMEM((2, TM, d), BF16), pltpu.VMEM((TM, d), F32)],
        compiler_params=_cparams(2, vmem_mib=56 if emit_norm else 48), name="ffn",
    )(h, h, gain.reshape(1, d), w_up, w_down, (next_gain if emit_norm else gain).reshape(1, d))
    return (out[0], out[1]) if emit_norm else out[0]


def _gated_out_kernel(o_ref, z_ref, gn_ref, w_ref, h_ref, out_ref, *, nheads, hd):
    kstep = 2 * LANES
    acc = h_ref[...]
    for k0 in range(0, nheads * hd, kstep):
        tiles = range(k0 // LANES, (k0 + kstep) // LANES)
        o = jnp.concatenate([o_ref[t] for t in tiles], axis=1)
        z = jnp.concatenate([z_ref[t] for t in tiles], axis=1)
        on = jnp.concatenate([_rms(o[:, i * hd:(i + 1) * hd], gn_ref[...]) for i in range(kstep // hd)], axis=1)
        a = (on * (z * _sigmoid(z))).astype(BF16)
        acc = acc + _dot(a, w_ref[k0:k0 + kstep, :])
    out_ref[...] = acc


def gated_out(o_tiles, p_tiles, z_block, gain, w_out, h, nheads, hd, tm):
    rows, d = h.shape
    kdim = nheads * hd
    nt = kdim // LANES
    return pl.pallas_call(
        functools.partial(_gated_out_kernel, nheads=nheads, hd=hd), grid=(rows // tm,),
        in_specs=[pl.BlockSpec((nt, tm, LANES), lambda i: (0, i, 0)),
                  pl.BlockSpec((nt, tm, LANES), lambda i: (z_block, i, 0)),
                  pl.BlockSpec((1, hd), lambda i: (0, 0)),
                  pl.BlockSpec((kdim, d), lambda i: (0, 0)),
                  pl.BlockSpec((tm, d), lambda i: (i, 0))],
        out_specs=pl.BlockSpec((tm, d), lambda i: (i, 0)),
        out_shape=SDS((rows, d), F32),
        compiler_params=_cparams(1), name="gated_out",
    )(o_tiles, p_tiles, gain.reshape(1, hd), w_out, h)


def _s5_discretize_kernel(are_ref, aim_ref, ldt_ref, bre_ref, bim_ref,
                          abr_ref, abi_ref, bbr_ref, bbi_ref):
    a_re, a_im = are_ref[...], aim_ref[...]
    dt = jnp.exp(ldt_ref[...])
    mag = jnp.exp(a_re * dt)
    ab_re = mag * jnp.cos(a_im * dt)
    ab_im = mag * jnp.sin(a_im * dt)
    den = a_re * a_re + a_im * a_im
    c_re = ((ab_re - 1.0) * a_re + ab_im * a_im) / den
    c_im = (ab_im * a_re - (ab_re - 1.0) * a_im) / den
    abr_ref[...] = ab_re
    abi_ref[...] = ab_im
    bbr_ref[...] = c_re * bre_ref[...] - c_im * bim_ref[...]
    bbi_ref[...] = c_re * bim_ref[...] + c_im * bre_ref[...]


def s5_discretize(a_re, a_im, log_dt, b_re, b_im):
    g, p, c = S5_GROUPS, S5_STATE, S5_GROUP
    expand = lambda v: jnp.broadcast_to(v[..., None], (g, p, c)).reshape(g, p * c)
    ldt = jnp.broadcast_to(log_dt[:, None], (g, p * c))
    shp = SDS((g, p * c), F32)
    ab_re, ab_im, bb_re, bb_im = pl.pallas_call(
        _s5_discretize_kernel, out_shape=(shp, shp, shp, shp), name="s5_discretize",
    )(expand(a_re), expand(a_im), ldt, b_re.reshape(g, p * c), b_im.reshape(g, p * c))
    ab_re = ab_re.reshape(g, p, c)[:, :, 0].reshape(S5_NGB, 1, S5_BS)
    ab_im = ab_im.reshape(g, p, c)[:, :, 0].reshape(S5_NGB, 1, S5_BS)
    eye = jnp.eye(S5_GB, dtype=F32)

    def block_diag_in(bb):
        bb = bb.reshape(S5_NGB, S5_GB, p, c)
        return jnp.einsum("bgpc,gh->bgchp", bb, eye).reshape(S5_NGB, S5_BC, S5_BS)

    b_blk = jnp.concatenate([block_diag_in(bb_re.reshape(g, p, c)),
                             block_diag_in(bb_im.reshape(g, p, c))], axis=-1).astype(BF16)
    return ab_re, ab_im, b_blk


def s5_block_diag_out(c_par):
    eye = jnp.eye(S5_GB, dtype=F32)
    cc = c_par.reshape(S5_NGB, S5_GB, S5_GROUP, S5_STATE)
    return jnp.einsum("bgcp,gh->bgphc", cc, eye).reshape(S5_NGB, S5_BS, S5_BC).astype(BF16)


def _s5_scan_kernel(u_ref, b_ref, cre_ref, cim_ref, are_ref, aim_ref, d_ref, s0r_ref, s0i_ref,
                    g_ref, slr_ref, sli_ref, xr_ref, xi_ref, str_ref, sti_ref, *, tc):
    n = pl.program_id(1)

    @pl.when(n == 0)
    def _():
        str_ref[...] = s0r_ref[...]
        sti_ref[...] = s0i_ref[...]

    sub = SUBLANES
    a_re = jnp.broadcast_to(are_ref[0], (sub, S5_BS))
    a_im = jnp.broadcast_to(aim_ref[0], (sub, S5_BS))
    sb = 2 * LANES
    x_re = x_im = None
    for k in range(u_ref.shape[0] // sb):
        rows = slice(k * sb, (k + 1) * sb)
        u = u_ref[rows, :]
        ub = u.astype(BF16)
        xr_ref[rows, :] = _dot(ub, b_ref[0, :, :S5_BS])
        xi_ref[rows, :] = _dot(ub, b_ref[0, :, S5_BS:])
        for slab in range(k * sb // sub, (k + 1) * sb // sub):
            grp, t = divmod(slab, tc)
            srows = slice(grp * sub, (grp + 1) * sub)
            if t == 0:
                x_re, x_im = str_ref[srows, :], sti_ref[srows, :]
            r8 = slice(slab * sub, (slab + 1) * sub)
            x_re, x_im = (a_re * x_re - a_im * x_im + xr_ref[r8, :],
                          a_re * x_im + a_im * x_re + xi_ref[r8, :])
            xr_ref[r8, :] = x_re
            xi_ref[r8, :] = x_im
            if t == tc - 1:
                str_ref[srows, :] = x_re
                sti_ref[srows, :] = x_im
        y = _dot(xr_ref[rows, :].astype(BF16), cre_ref[0]) - _dot(xi_ref[rows, :].astype(BF16), cim_ref[0])
        g_ref[rows, :] = _gelu_tanh(y + d_ref[...] * u).astype(BF16)

    @pl.when(n == pl.num_programs(1) - 1)
    def _():
        slr_ref[...] = str_ref[...]
        sli_ref[...] = sti_ref[...]


def s5_scan(hn, nb, b_blk, c_re_blk, c_im_blk, ab_re, ab_im, d_skip, s0_re, s0_im):
    rows, d = hn.shape
    tc = min(rows // nb, TM // SUBLANES)
    ngrp = TM // (tc * SUBLANES)
    assert ngrp == 1 or ngrp * SUBLANES == nb
    st = SDS((nb, S5_GROUPS * S5_STATE), F32)
    return pl.pallas_call(
        functools.partial(_s5_scan_kernel, tc=tc), grid=(S5_NGB, rows // TM),
        in_specs=[pl.BlockSpec((TM, S5_BC), lambda gb, n: (n, gb)),
                  pl.BlockSpec((1, S5_BC, 2 * S5_BS), lambda gb, n: (gb, 0, 0)),
                  pl.BlockSpec((1, S5_BS, S5_BC), lambda gb, n: (gb, 0, 0)),
                  pl.BlockSpec((1, S5_BS, S5_BC), lambda gb, n: (gb, 0, 0)),
                  pl.BlockSpec((1, 1, S5_BS), lambda gb, n: (gb, 0, 0)),
                  pl.BlockSpec((1, 1, S5_BS), lambda gb, n: (gb, 0, 0)),
                  pl.BlockSpec((1, S5_BC), lambda gb, n: (0, gb)),
                  pl.BlockSpec((nb, S5_BS), lambda gb, n: (0, gb)),
                  pl.BlockSpec((nb, S5_BS), lambda gb, n: (0, gb))],
        out_specs=[pl.BlockSpec((TM, S5_BC), lambda gb, n: (n, gb)),
                   pl.BlockSpec((nb, S5_BS), lambda gb, n: (0, gb)),
                   pl.BlockSpec((nb, S5_BS), lambda gb, n: (0, gb))],
        out_shape=(SDS((rows, d), BF16), st, st),
        scratch_shapes=[pltpu.VMEM((TM, S5_BS), F32), pltpu.VMEM((TM, S5_BS), F32),
                        pltpu.VMEM((nb, S5_BS), F32), pltpu.VMEM((nb, S5_BS), F32)],
        compiler_params=_cparams(2), name="s5_scan",
    )(hn, b_blk, c_re_blk, c_im_blk, ab_re, ab_im, d_skip.reshape(1, d), s0_re, s0_im)


def _glu_out_kernel(g_ref, w_ref, h_ref, o_ref):
    gv = _dot(g_ref[...], w_ref[...])
    o_ref[...] = h_ref[...] + gv[:, :D_MODEL] * _sigmoid(gv[:, D_MODEL:])


def glu_out(g, w_glu, h):
    rows, d = h.shape
    tm = TM
    return pl.pallas_call(
        _glu_out_kernel, grid=(rows // tm,),
        in_specs=[pl.BlockSpec((tm, d), lambda i: (i, 0)),
                  pl.BlockSpec((d, 2 * d), lambda i: (0, 0)),
                  pl.BlockSpec((tm, d), lambda i: (i, 0))],
        out_specs=pl.BlockSpec((tm, d), lambda i: (i, 0)),
        out_shape=SDS((rows, d), F32), compiler_params=_cparams(1), name="glu_out",
    )(g, w_glu, h)


def s5_layer(h, hn, nb, s0_re, s0_im, a_re, a_im, log_dt, b_re, b_im, c_re, c_im, d_skip, w_glu):
    ab_re, ab_im, b_blk = s5_discretize(a_re, a_im, log_dt, b_re, b_im)
    g, sl_re, sl_im = s5_scan(hn, nb, b_blk, s5_block_diag_out(c_re), s5_block_diag_out(c_im),
                              ab_re, ab_im, d_skip, s0_re.reshape(nb, -1), s0_im.reshape(nb, -1))
    shape = (nb, S5_GROUPS, S5_STATE)
    return glu_out(g, w_glu.astype(BF16), h), sl_re.reshape(shape), sl_im.reshape(shape)


class _Problem:
    def __init__(self, c, pack):
        p = c * pack
        r = np.arange(p)
        seq, time = r % pack, r // pack
        same = seq[:, None] == seq[None, :]
        self.p, self.pack = p, pack
        self.tri = same & (time[None, :] <= time[:, None])
        self.whole = same
        masks, pivots, bcast = [r[:, None] == r[None, :]], [], []
        for sz in [2 ** i for i in range(1, int(math.log2(p)) + 1)]:
            blk, off = r // sz, r % sz
            m = same & (blk[:, None] == blk[None, :]) & (off[:, None] >= sz // 2) & (off[None, :] < sz // 2)
            if not m.any():
                continue
            lower = same & (blk[:, None] == blk[None, :]) & (off[None, :] < sz // 2)
            piv = np.where(lower.any(1), (lower * r[None, :]).max(1), r)
            masks.append(m)
            if pack == 1 and sz >= SUBLANES:
                bcast.append(sz)
            else:
                pivots.append(self.tri[piv])
        self.masks = np.stack(masks)
        self.pivots = pivots
        self.bcast = bcast
        self.seq_lanes = np.repeat(seq[:, None] == np.arange(pack)[None, :], LANES, axis=1)

    def f32(self, x):
        return jnp.asarray(np.asarray(x, np.float32))


def _gla_gate_kernel(h_ref, gn_ref, wgl_ref, w_ref, b_ref, o_ref):
    hn = _rms(h_ref[...], gn_ref[...]).astype(BF16)
    gl = _dot(hn, wgl_ref[...])
    x = _dot(gl.astype(BF16), w_ref[...]) + b_ref[...]
    g = -_softplus(-x) * (1.0 / GLA_TAU)
    for t in range(o_ref.shape[0]):
        o_ref[t] = g[:, t * LANES:(t + 1) * LANES]


def gla_gate(h, gain, w_gl_pad, w_gate_pad, b_gate):
    rows, d = h.shape
    nt = GLA_KEY // LANES
    return pl.pallas_call(
        _gla_gate_kernel, grid=(rows // TM,),
        in_specs=[pl.BlockSpec((TM, d), lambda i: (i, 0)),
                  pl.BlockSpec((1, d), lambda i: (0, 0)),
                  pl.BlockSpec((d, LANES), lambda i: (0, 0)),
                  pl.BlockSpec((LANES, GLA_KEY), lambda i: (0, 0)),
                  pl.BlockSpec((1, GLA_KEY), lambda i: (0, 0))],
        out_specs=pl.BlockSpec((nt, TM, LANES), lambda i: (0, i, 0)),
        out_shape=SDS((nt, rows, LANES), F32), compiler_params=_cparams(1), name="gla_gate",
    )(h, gain.reshape(1, d), w_gl_pad, w_gate_pad, b_gate.reshape(1, GLA_KEY))


def _gla_chunk_kernel(q_ref, k_ref, v_ref, g_ref, wst_ref, mask_ref, seqm_ref, s0_ref, o_ref, s_ref, *, p, pack,
                      bcast):
    @pl.when(pl.program_id(1) == 0)
    def _():
        s_ref[...] = s0_ref[...]

    wst = wst_ref[...]
    eye = _eye(GLA_DK)
    nlev = mask_ref.shape[0] - 1
    nmat = nlev - len(bcast)
    tpv = GLA_DV // LANES

    def pivot(wk, lv):
        if lv <= nmat:
            return wk["bp"][lv * p:(lv + 1) * p]
        sz = bcast[lv - nmat - 1]
        b = wk["b"]
        return jnp.concatenate([jnp.broadcast_to(b[r0 + sz // 2 - 1:r0 + sz // 2, :], (sz, LANES))
                                for r0 in range(0, p, sz)], axis=0)

    def problems(rows_seqs):
        loaded = [(hh, rows, seqs, q_ref[hh, rows, :], k_ref[hh, rows, :],
                   [v_ref[hh * tpv + t, rows, :] for t in range(tpv)],
                   g_ref[hh, rows, :], [s_ref[j, hh] for j in seqs])
                  for rows, seqs in rows_seqs for hh in range(GLA_HEADS)]
        results = []
        work = []
        for hh, rows, seqs, q, k, v, g, ss in loaded:
            q = q * (GLA_DK ** -0.5)
            bp = _dot_exact01(wst, g)
            work.append(dict(hh=hh, rows=rows, seqs=seqs, q=q, k=k, v=v, ss=ss, bp=bp, b=bp[0:p],
                             att=mask_ref[0] * _dot_nt(q.astype(BF16), k.astype(BF16))))
        for lv in range(1, nlev + 1):
            for wk in work:
                e = jnp.exp(-jnp.abs(wk["b"] - pivot(wk, lv)))
                wk["att"] = wk["att"] + mask_ref[lv] * _dot_nt((wk["q"] * e).astype(BF16),
                                                               (wk["k"] * e).astype(BF16))
        for wk in work:
            q, k, v, ss, bp, b, att = (wk[n] for n in ("q", "k", "v", "ss", "bp", "b", "att"))
            vb = jnp.concatenate(v, axis=1).astype(BF16)
            b_end = bp[(nmat + 1) * p:(nmat + 2) * p] if pack > 1 else jnp.broadcast_to(b[p - 1:p, :], (p, LANES))
            s_all = jnp.concatenate(ss, axis=0)
            q_dec = q * jnp.exp(b)
            k_dec = k * jnp.exp(b_end - b)
            if pack > 1:
                q_dec = jnp.concatenate([q_dec] * pack, axis=1) * seqm_ref[...]
                k_dec = jnp.concatenate([k_dec] * pack, axis=1) * seqm_ref[...]
            o = _dot(q_dec.astype(BF16), s_all.astype(BF16)) + _dot(att.astype(BF16), vb)
            last = jnp.concatenate([b_end[0:pack]] * (GLA_DK // pack), axis=0) if pack > 1 else (
                jnp.broadcast_to(b_end[0:1], (GLA_DK, GLA_DK)))
            col = jnp.exp(_dot_nt_exact01(eye, last))
            dec = jnp.concatenate([jnp.broadcast_to(col[:, i:i + 1], (GLA_DK, GLA_DV)) for i in range(pack)], axis=0)
            results.append((wk, o, dec * s_all + _dot_tn(k_dec.astype(BF16), vb)))
        for wk, o, s_new in results:
            hh = wk["hh"]
            for t in range(tpv):
                o_ref[hh * tpv + t, wk["rows"], :] = o[:, t * LANES:(t + 1) * LANES]
            for i, j in enumerate(wk["seqs"]):
                s_ref[j, hh] = s_new[i * GLA_DK:(i + 1) * GLA_DK]

    if pack == 1:
        def per_pair(jp, carry):
            j = 2 * jp
            problems([(pl.ds(j, p, stride=SUBLANES), [j]), (pl.ds(j + 1, p, stride=SUBLANES), [j + 1])])
            return carry

        lax.fori_loop(0, SUBLANES // 2, per_pair, 0)
    else:
        problems([(slice(i * p, (i + 1) * p), list(range(i * SUBLANES, (i + 1) * SUBLANES)))
                  for i in range(s_ref.shape[0] // SUBLANES)])


def gla_chunks(p_tiles, g_tiles, s0, nb, seq, c, pack):
    rows = p_tiles.shape[1]
    pr = _Problem(c, pack)
    wst = pr.f32(np.concatenate([pr.tri] + pr.pivots + ([pr.whole] if pack > 1 else []), axis=0))
    nkt = GLA_KEY // LANES
    nvt = GLA_VAL // LANES
    nc = seq // c
    ns = SUBLANES * (PACK_GROUPS if pack > 1 else 1)
    assert nb % ns == 0
    tiles = lambda nt, blk: pl.BlockSpec((nt, c * ns, LANES), lambda b, n: (blk, b * nc + n, 0))
    const = lambda x: pl.BlockSpec(x.shape, lambda b, n: (0,) * x.ndim)
    state = pl.BlockSpec((ns, GLA_HEADS, GLA_DK, GLA_DV), lambda b, n: (b, 0, 0, 0))
    masks, seqm = pr.f32(pr.masks), pr.f32(pr.seq_lanes)
    o, s_out = pl.pallas_call(
        functools.partial(_gla_chunk_kernel, p=pr.p, pack=pack, bcast=tuple(pr.bcast)), grid=(nb // ns, nc),
        in_specs=[tiles(nkt, 0), tiles(nkt, 1), tiles(nvt, 1), tiles(nkt, 0),
                  const(wst), const(masks), const(seqm), state],
        out_specs=[tiles(nvt, 0), state],
        out_shape=(SDS((nvt, rows, LANES), F32), SDS((nb, GLA_HEADS, GLA_DK, GLA_DV), F32)),
        compiler_params=_cparams(2), name="gla_chunks",
    )(p_tiles, p_tiles, p_tiles, g_tiles, wst, masks, seqm, s0)
    return o, s_out


def _chunking(seq):
    return (CHUNK, 1) if seq % CHUNK == 0 else (seq, SUBLANES)


def gla_layer(h, nb, seq, gain, s0, w_in, w_gate_up, b_gate, norm_o, w_out):
    c, pack = _chunking(seq)
    n_main = 2 * GLA_KEY + 2 * GLA_VAL
    p_tiles = norm_matmul(h, gain, w_in[:, :n_main].astype(BF16), GLA_TN)
    w_gl = jnp.pad(w_in[:, n_main:], ((0, 0), (0, LANES - GLA_RANK))).astype(BF16)
    w_gate_pad = jnp.pad(w_gate_up, ((0, LANES - GLA_RANK), (0, 0))).astype(BF16)
    g_tiles = gla_gate(h, gain, w_gl, w_gate_pad, b_gate)
    o_tiles, s_out = gla_chunks(p_tiles, g_tiles, s0, nb, seq, c, pack)
    h = gated_out(o_tiles, p_tiles, 2, norm_o, w_out.astype(BF16), h, GLA_HEADS, GLA_DV, TM)
    return h, s_out


def _gdn_conv_kernel(x_ref, cw_ref, c0_ref, o_ref, nc_ref, xp_ref, *, tm):
    grp = pl.program_id(0)
    i = pl.program_id(2)
    nb = SUBLANES
    halo = (GDN_CONV - 1) * nb
    scale = jnp.where(grp == 0, GDN_DK ** -0.5, 1.0)
    for t in range(x_ref.shape[0]):
        lanes = slice(t * LANES, (t + 1) * LANES)

        @pl.when(i == 0)
        def _():
            xp_ref[t, 0:halo, :] = c0_ref[:, lanes]

        xp_ref[t, halo:halo + tm, :] = x_ref[t]
        acc = cw_ref[0:1, lanes] * xp_ref[t, 0:tm, :]
        for j in range(1, GDN_CONV):
            acc = acc + cw_ref[j:j + 1, lanes] * xp_ref[t, j * nb:j * nb + tm, :]
        y = acc * _sigmoid(acc)

        inv = lax.rsqrt(jnp.sum(y * y, axis=-1, keepdims=True) + NORM_EPS) * scale
        o_ref[t] = y * jnp.where(grp < 2, inv, 1.0)
        tail = xp_ref[t, tm:tm + halo, :]
        xp_ref[t, 0:halo, :] = tail

        @pl.when(i == pl.num_programs(2) - 1)
        def _():
            nc_ref[:, lanes] = tail


def gdn_conv(p_tiles, conv_w, conv0, nb):
    rows = p_tiles.shape[1]
    ngrp = nb // SUBLANES
    halo = (GDN_CONV - 1) * SUBLANES
    tm = min(TM, rows // ngrp)
    nblk = rows // ngrp // tm
    gt = GDN_TN // LANES
    ng = GDN_CONV_DIM // GDN_TN
    assert tm >= halo
    return pl.pallas_call(
        functools.partial(_gdn_conv_kernel, tm=tm), grid=(ng, ngrp, nblk),
        in_specs=[pl.BlockSpec((gt, tm, LANES), lambda j, g, i: (j, g * nblk + i, 0)),
                  pl.BlockSpec((GDN_CONV, GDN_TN), lambda j, g, i: (0, j)),
                  pl.BlockSpec((halo, GDN_TN), lambda j, g, i: (g, j))],
        out_specs=[pl.BlockSpec((gt, tm, LANES), lambda j, g, i: (j, g * nblk + i, 0)),
                   pl.BlockSpec((halo, GDN_TN), lambda j, g, i: (g, j))],
        out_shape=(SDS((ng * gt, rows, LANES), F32), SDS((ngrp * halo, GDN_CONV_DIM), F32)),
        scratch_shapes=[pltpu.VMEM((gt, halo + tm, LANES), F32)],
        compiler_params=_cparams(3), name="gdn_conv",
    )(p_tiles, conv_w, conv0)


def _gdn_chunk_kernel(q_ref, k_ref, v_ref, ba_ref, alog_ref, dtb_ref, tril_ref, mask_ref, maskb_ref, seqm_ref,
                      s0_ref, o_ref, s_ref, *, p, pack):
    @pl.when(pl.program_id(2) == 0)
    def _():
        s_ref[...] = s0_ref[...]

    tri = tril_ref[0:p, :]
    eye_l = _eye(LANES)
    eye_c = mask_ref[0]
    strict = tri - eye_c
    nlv = mask_ref.shape[0]
    rep = GDN_V_HEADS // GDN_QK_HEADS

    def load_heads(rows, seqs):
        ba = ba_ref[0, rows, :]
        beta_all = _sigmoid(ba)
        g_all = -jnp.exp(alog_ref[0]) * _softplus(ba + dtb_ref[0])
        gcl = _dot_exact01(tril_ref[...], g_all)
        gc_all, ge_all = gcl[0:p], gcl[p:2 * p]
        gr_all = _dot_nt_exact01(eye_l, gc_all)
        heads = []
        for qh in range(GDN_HALF // rep):
            q = q_ref[qh, rows, :]
            k = k_ref[qh, rows, :]
            kb = k.astype(BF16)
            kq = _dot_nt(jnp.concatenate([kb, q.astype(BF16)], axis=0), kb)
            kk, qk = kq[:p], kq[p:]
            for j in range(rep):
                hh = qh * rep + j
                beta = beta_all[:, hh:hh + 1]
                g_col = gc_all[:, GDN_HALF + hh:GDN_HALF + hh + 1]
                g_end = ge_all[:, GDN_HALF + hh:GDN_HALF + hh + 1]
                g_row = gr_all[GDN_HALF + hh:GDN_HALF + hh + 1, :]
                decay = tri * jnp.exp(jnp.minimum(g_col - g_row, 0.0))
                m = strict * (kk * decay * beta)
                heads.append(dict(hh=hh, rows=rows, seqs=seqs, q=q, k=k, v=v_ref[hh, rows, :], beta=beta,
                                  g_col=g_col, g_end=g_end,
                                  s=jnp.concatenate([s_ref[i, hh] for i in seqs], axis=0),
                                  mb=m.astype(BF16), qkd=(qk * decay).astype(BF16),
                                  t=eye_c - mask_ref[1] * m))
        return heads

    def solve(heads):
        for lv in range(2, nlv):
            for hd in heads:
                tb = hd["t"].astype(BF16)
                hd["t"] = hd["t"] - _dot(_dot(tb, maskb_ref[lv] * hd["mb"]).astype(BF16), tb)
        for hd in heads:
            e_g = jnp.exp(hd["g_col"])
            rhs = jnp.concatenate([hd["v"] * hd["beta"], hd["k"] * (hd["beta"] * e_g)], axis=1).astype(BF16)
            hd["uw"] = _dot(hd["t"].astype(BF16), rhs)
            hd["q_dec"] = hd["q"] * e_g
        for hd in heads:
            uw = hd["uw"]
            lhs = jnp.concatenate([uw[:, GDN_DV:], hd["q_dec"]], axis=0)
            if pack > 1:
                lhs = jnp.concatenate([lhs] * pack, axis=1) * jnp.concatenate([seqm_ref[...]] * 2, axis=0)
            ws = _dot(lhs.astype(BF16), hd["s"].astype(BF16))
            hd["vnb"] = (uw[:, :GDN_DV] - ws[:p]).astype(BF16)
            hd["o_inter"] = ws[p:]
        for hd in heads:
            g_end = hd["g_end"]
            k_dec = hd["k"] * jnp.exp(g_end - hd["g_col"])
            if pack > 1:
                k_dec = jnp.concatenate([k_dec] * pack, axis=1) * seqm_ref[...]
            hd["o"] = hd["o_inter"] + _dot(hd["qkd"], hd["vnb"])
            dec = jnp.concatenate([jnp.broadcast_to(jnp.exp(g_end[i:i + 1, :]), (GDN_DK, GDN_DV))
                                   for i in range(pack)], axis=0)
            hd["s_new"] = dec * hd["s"] + _dot_tn(k_dec.astype(BF16), hd["vnb"])
        for hd in heads:
            o_ref[hd["hh"], hd["rows"], :] = hd["o"]
            for i, j in enumerate(hd["seqs"]):
                s_ref[j, hd["hh"]] = hd["s_new"][i * GDN_DK:(i + 1) * GDN_DK]

    if pack == 1:
        def per_pair(jp, carry):
            b = 2 * jp
            solve(load_heads(pl.ds(b, p, stride=SUBLANES), [b])
                  + load_heads(pl.ds(b + 1, p, stride=SUBLANES), [b + 1]))
            return carry

        lax.fori_loop(0, SUBLANES // 2, per_pair, 0)
    else:
        solve(sum([load_heads(slice(i * p, (i + 1) * p), list(range(i * SUBLANES, (i + 1) * SUBLANES)))
                   for i in range(s_ref.shape[0] // SUBLANES)], []))


def gdn_chunks(qkv_tiles, p_tiles, a_log, dt_bias, s0, nb, seq, c, pack):
    ba0 = (GDN_CONV_DIM + GDN_VAL) // LANES
    rows = qkv_tiles.shape[1]
    nqk = GDN_HALF // (GDN_V_HEADS // GDN_QK_HEADS)
    nc = seq // c
    pr = _Problem(c, pack)
    pad = lambda x: jnp.pad(x.reshape(2, 1, GDN_HALF), ((0, 0), (0, 0), (GDN_HALF, LANES - 2 * GDN_HALF)))
    masks, seqm = pr.f32(pr.masks), pr.f32(pr.seq_lanes)
    tril = pr.f32(np.concatenate([pr.tri, pr.whole], axis=0))
    ns = SUBLANES * (PACK_GROUPS if pack > 1 else 1)
    assert nb % ns == 0
    tiles = lambda nt, blk: pl.BlockSpec((nt, c * ns, LANES), lambda b, hf, n: (blk(hf), b * nc + n, 0))
    const = lambda x: pl.BlockSpec(x.shape, lambda b, hf, n: (0,) * x.ndim)
    state = pl.BlockSpec((ns, GDN_HALF, GDN_DK, GDN_DV), lambda b, hf, n: (b, hf, 0, 0))
    o, s_out = pl.pallas_call(
        functools.partial(_gdn_chunk_kernel, p=pr.p, pack=pack), grid=(nb // ns, 2, nc),
        in_specs=[tiles(nqk, lambda hf: hf), tiles(nqk, lambda hf: 2 + hf),
                  tiles(GDN_HALF, lambda hf: 2 + hf), tiles(1, lambda hf: ba0 + hf),
                  pl.BlockSpec((1, 1, LANES), lambda b, hf, n: (hf, 0, 0)),
                  pl.BlockSpec((1, 1, LANES), lambda b, hf, n: (hf, 0, 0)),
                  const(tril), const(masks), const(masks), const(seqm), state],
        out_specs=[tiles(GDN_HALF, lambda hf: hf), state],
        out_shape=(SDS((GDN_V_HEADS, rows, LANES), F32), SDS((nb, GDN_V_HEADS, GDN_DK, GDN_DV), F32)),
        compiler_params=_cparams(3, vmem_mib=56), name="gdn_chunks",
    )(qkv_tiles, qkv_tiles, qkv_tiles, p_tiles, pad(a_log), pad(dt_bias), tril, masks, masks.astype(BF16), seqm, s0)
    return o, s_out


def gdn_layer(h, nb, seq, gain, s0, conv0, w_in, conv_w, a_log, dt_bias, norm_o, w_out):
    c, pack = _chunking(seq)
    n_main = GDN_CONV_DIM + GDN_VAL
    w_b = w_in[:, n_main:n_main + GDN_V_HEADS].reshape(D_MODEL, 2, GDN_HALF)
    w_a = w_in[:, n_main + GDN_V_HEADS:].reshape(D_MODEL, 2, GDN_HALF)
    w_ba = jnp.pad(jnp.concatenate([w_b, w_a], axis=2), ((0, 0), (0, 0), (0, LANES - 2 * GDN_HALF)))
    w_all = jnp.concatenate([w_in[:, :n_main], w_ba.reshape(D_MODEL, 2 * LANES)], axis=1).astype(BF16)
    p_tiles = norm_matmul(h, gain, w_all, GDN_PROJ_TN)
    ngrp = nb // SUBLANES
    conv0_tm = conv0.reshape(ngrp, SUBLANES, GDN_CONV - 1, GDN_CONV_DIM).transpose(0, 2, 1, 3)
    conv0_tm = conv0_tm.reshape(ngrp * (GDN_CONV - 1) * SUBLANES, GDN_CONV_DIM)
    qkv_tiles, new_conv = gdn_conv(p_tiles, conv_w, conv0_tm, nb)
    o_tiles, s_out = gdn_chunks(qkv_tiles, p_tiles, a_log, dt_bias, s0, nb, seq, c, pack)
    h = gated_out(o_tiles, p_tiles, 2, norm_o, w_out.astype(BF16), h, GDN_V_HEADS, GDN_DV, TM // 2)
    new_conv = new_conv.reshape(ngrp, GDN_CONV - 1, SUBLANES, GDN_CONV_DIM)
    return h, s_out, new_conv.transpose(0, 2, 1, 3).reshape(nb, GDN_CONV - 1, GDN_CONV_DIM)


def _trunk(x, s5_re, s5_im, gla_s, gdn_s, gdn_conv_s, w):
    nb, seq, d = x.shape
    h, hn = norm_in(x, w["norm_mix"][0])
    h, s5r0, s5i0 = s5_layer(h, hn, nb, s5_re[0], s5_im[0], w["s5_a_re"][0], w["s5_a_im"][0],
                             w["s5_log_dt"][0], w["s5_b_re"][0], w["s5_b_im"][0], w["s5_c_re"][0],
                             w["s5_c_im"][0], w["s5_d"][0], w["s5_w_glu"][0])
    h = ffn(h, w["norm_ffn"][0], w["w_up"][0], w["w_down"][0])
    h, gla_o = gla_layer(h, nb, seq, w["norm_mix"][1], gla_s[0], w["gla_w_in"][0], w["gla_w_gate_up"][0],
                         w["gla_b_gate"][0], w["gla_norm"][0], w["gla_w_out"][0])
    h = ffn(h, w["norm_ffn"][1], w["w_up"][1], w["w_down"][1])
    h, gdn_o, conv_o = gdn_layer(h, nb, seq, w["norm_mix"][2], gdn_s[0], gdn_conv_s[0], w["gdn_w_in"][0],
                                 w["gdn_conv_w"][0], w["gdn_a_log"][0], w["gdn_dt_bias"][0],
                                 w["gdn_norm"][0], w["gdn_w_out"][0])
    h, hn = ffn(h, w["norm_ffn"][2], w["w_up"][2], w["w_down"][2], next_gain=w["norm_mix"][3])
    h, s5r1, s5i1 = s5_layer(h, hn, nb, s5_re[1], s5_im[1], w["s5_a_re"][1], w["s5_a_im"][1],
                             w["s5_log_dt"][1], w["s5_b_re"][1], w["s5_b_im"][1], w["s5_c_re"][1],
                             w["s5_c_im"][1], w["s5_d"][1], w["s5_w_glu"][1])
    h = ffn(h, w["norm_ffn"][3], w["w_up"][3], w["w_down"][3])
    y = norm_out(h, w["norm_final"], nb, seq)
    return (y, jnp.stack([s5r0, s5r1]), jnp.stack([s5i0, s5i1]), gla_o[None], gdn_o[None], conv_o[None])


def kernel(x_prompt, x_sample, state_s5_re, state_s5_im, state_gla, state_gdn, state_gdn_conv, norm_mix, norm_ffn, norm_final, w_up, w_down, s5_a_re, s5_a_im, s5_log_dt, s5_b_re, s5_b_im, s5_c_re, s5_c_im, s5_d, s5_w_glu, gla_w_in, gla_w_gate_up, gla_b_gate, gla_norm, gla_w_out, gdn_w_in, gdn_conv_w, gdn_a_log, gdn_dt_bias, gdn_norm, gdn_w_out):
    w = dict(norm_mix=norm_mix, norm_ffn=norm_ffn, norm_final=norm_final,
             w_up=w_up.astype(BF16), w_down=w_down.astype(BF16),
             s5_a_re=s5_a_re, s5_a_im=s5_a_im, s5_log_dt=s5_log_dt, s5_b_re=s5_b_re, s5_b_im=s5_b_im,
             s5_c_re=s5_c_re, s5_c_im=s5_c_im, s5_d=s5_d, s5_w_glu=s5_w_glu,
             gla_w_in=gla_w_in, gla_w_gate_up=gla_w_gate_up, gla_b_gate=gla_b_gate, gla_norm=gla_norm,
             gla_w_out=gla_w_out, gdn_w_in=gdn_w_in, gdn_conv_w=gdn_conv_w, gdn_a_log=gdn_a_log,
             gdn_dt_bias=gdn_dt_bias, gdn_norm=gdn_norm, gdn_w_out=gdn_w_out)
    bp = x_prompt.shape[0]
    dt = x_prompt.dtype
    z_s5 = jnp.zeros((state_s5_re.shape[0], bp) + state_s5_re.shape[2:], dt)
    z_gla = jnp.zeros((state_gla.shape[0], bp) + state_gla.shape[2:], dt)
    z_gdn = jnp.zeros((state_gdn.shape[0], bp) + state_gdn.shape[2:], dt)
    z_conv = jnp.zeros((state_gdn_conv.shape[0], bp) + state_gdn_conv.shape[2:], dt)
    out_p = _trunk(x_prompt, z_s5, z_s5, z_gla, z_gdn, z_conv, w)
    out_s = _trunk(x_sample, state_s5_re, state_s5_im, state_gla, state_gdn, state_gdn_conv, w)
    return (out_p[0], out_s[0]) + out_p[1:] + out_s[1:]
```
